```python
import math
import jax
import jax.numpy as jnp
from jax import lax
import numpy as np

D_MODEL = 2048
BATCH = 4
SEQ = 4096
DEPTH = 4

BLOCK = 128
N_BRANCH = 4
BRANCH_W = D_MODEL // 4
A_HEADS = 4
A_HD = BRANCH_W // (2 * A_HEADS)
B_WIDTH = BRANCH_W
B_BLOCKS = 8
B_CONV = 4
B_C = 8.0
C_HEADS = 4
C_HD = BRANCH_W // C_HEADS
D_HEADS = 8
D_KV = 2
D_GROUP = D_HEADS // D_KV
D_HD = BRANCH_W // D_HEADS
WINDOW = 128
D_FF = -(-8 * D_MODEL // (3 * 256)) * 256

IN_SIZES = (
    2 * A_HEADS * A_HD, 2 * A_HEADS * A_HD, 2 * A_HEADS * A_HD,
    B_WIDTH, B_WIDTH,
    C_HEADS * C_HD, C_HEADS * C_HD, C_HEADS * C_HD, C_HEADS,
    D_HEADS * D_HD, D_KV * D_HD, D_KV * D_HD,
    N_BRANCH * D_MODEL,
)
IN_COLS = sum(IN_SIZES)
RMS_EPS = 1e-6
NEG_INF = -1e30

kernel_name = 'hybrid_gated_diff_rglru_fox_swa_block'


def rms_norm(x, g):
    xf = x.astype(jnp.float32)
    y = xf * lax.rsqrt(jnp.mean(xf * xf, axis=-1, keepdims=True) + RMS_EPS)
    return (y * g.astype(jnp.float32)).astype(x.dtype)


def alibi_slopes(n):
    return jnp.exp2(-8.0 * jnp.arange(1, n + 1, dtype=jnp.float32) / n)


def diff_attention(q, k, v, lam, slopes):
    B, S, H, _, dh = q.shape
    nb = S // BLOCK
    scale = dh ** -0.5
    qb = q.reshape(B, nb, BLOCK, H, 2, dh).transpose(1, 0, 2, 3, 4, 5)
    kpos = jnp.arange(S)

    def one_block(args):
        qblk, i = args
        qpos = i * BLOCK + jnp.arange(BLOCK)
        dist = (qpos[:, None] - kpos[None, :]).astype(jnp.float32)
        logits = jnp.einsum('bqhmd,bshmd->bmhqs', qblk, k).astype(jnp.float32) * scale
        logits = logits - slopes[:, None, None] * dist
        logits = jnp.where(dist >= 0, logits, NEG_INF)
        p = jax.nn.softmax(logits, axis=-1)
        w = p[:, 0] - lam * p[:, 1]
        return jnp.einsum('bhqs,bshe->bqhe', w.astype(v.dtype), v)

    out = lax.map(one_block, (qb, jnp.arange(nb)))
    return out.transpose(1, 0, 2, 3, 4).reshape(B, S, H, 2 * dh)


def rg_lru_branch(xb, gb, conv_w, conv_b, wa, ba, wx, bx, lam):
    B, S, C = xb.shape
    xc = lax.conv_general_dilated(xb, conv_w[:, None, :], window_strides=(1,),
                                  padding=[(B_CONV - 1, 0)],
                                  dimension_numbers=('NWC', 'WIO', 'NWC'),
                                  feature_group_count=C) + conv_b
    xg = xc.reshape(B, S, B_BLOCKS, C // B_BLOCKS)
    r = jax.nn.sigmoid(jnp.einsum('bsnc,ncd->bsnd', xg, wa).reshape(B, S, C) + ba)
    i = jax.nn.sigmoid(jnp.einsum('bsnc,ncd->bsnd', xg, wx).reshape(B, S, C) + bx)
    log_a = -B_C * r.astype(jnp.float32) * jax.nn.softplus(-lam.astype(jnp.float32))
    a = jnp.exp(log_a)
    u = jnp.sqrt(-jnp.expm1(2.0 * log_a)) * (i * xc).astype(jnp.float32)

    def combine(left, right):
        a1, b1 = left
        a2, b2 = right
        return a1 * a2, a2 * b1 + b2

    _, h = lax.associative_scan(combine, (a, u), axis=1)
    return jax.nn.gelu(gb) * h.astype(xb.dtype)


def forgetting_attention(q, k, v, logf):
    B, S, H, dh = q.shape
    nb = S // BLOCK
    scale = dh ** -0.5
    cum = jnp.cumsum(logf, axis=1).transpose(0, 2, 1)
    qb = q.reshape(B, nb, BLOCK, H, dh).transpose(1, 0, 2, 3, 4)
    cqb = cum.reshape(B, H, nb, BLOCK).transpose(2, 0, 1, 3)
    kpos = jnp.arange(S)

    def one_block(args):
        qblk, cq, i = args
        qpos = i * BLOCK + jnp.arange(BLOCK)
        causal = qpos[:, None] >= kpos[None, :]
        logits = jnp.einsum('bqhd,bshd->bhqs', qblk, k).astype(jnp.float32) * scale
        logits = logits + cq[..., None] - cum[:, :, None, :]
        logits = jnp.where(causal, logits, NEG_INF)
        p = jax.nn.softmax(logits, axis=-1)
        return jnp.einsum('bhqs,bshd->bqhd', p.astype(v.dtype), v)

    out = lax.map(one_block, (qb, cqb, jnp.arange(nb)))
    return out.transpose(1, 0, 2, 3, 4).reshape(B, S, H * dh)


def sliding_window_sink_attention(q, k, v, sinks, slopes):
    B, S, KV, G, dh = q.shape
    nb = S // BLOCK
    scale = dh ** -0.5
    pad = ((0, 0), (BLOCK, 0), (0, 0), (0, 0))

    def band(t):
        tp = jnp.pad(t, pad)
        prev = tp[:, :S].reshape(B, nb, BLOCK, KV, dh)
        cur = tp[:, BLOCK:].reshape(B, nb, BLOCK, KV, dh)
        return jnp.concatenate([prev, cur], axis=2)

    kb, vb = band(k), band(v)
    qb = q.reshape(B, nb, BLOCK, KV, G, dh)
    qi = jnp.arange(BLOCK)
    kj = jnp.arange(2 * BLOCK) - BLOCK
    dist = (qi[:, None] - kj[None, :]).astype(jnp.float32)
    key_pos = jnp.arange(nb)[:, None] * BLOCK + kj[None, :]
    valid = ((dist >= 0) & (dist < WINDOW))[None] & (key_pos >= 0)[:, None, :]
    logits = jnp.einsum('bnqkgd,bnskd->bnkgqs', qb, kb).astype(jnp.float32) * scale
    logits = logits - slopes.reshape(KV, G)[:, :, None, None] * dist
    logits = jnp.where(valid[None, :, None, None], logits, NEG_INF)
    sink = sinks.astype(jnp.float32).reshape(KV, G)[None, None, :, :, None, None]
    m = jnp.maximum(jnp.max(logits, axis=-1, keepdims=True), sink)
    e = jnp.exp(logits - m)
    p = e / (jnp.sum(e, axis=-1, keepdims=True) + jnp.exp(sink - m))
    out = jnp.einsum('bnkgqs,bnskd->bnqkgd', p.astype(v.dtype), vb)
    return out.reshape(B, S, KV * G * dh)


def setup_inputs(seed: int = 0) -> dict:
    key = jax.random.key(seed)
    ks = jax.random.split(key, 26)
    f32 = jnp.float32

    def nrm(k, shape, scale):
        return jax.random.normal(k, shape, f32) * scale

    bw = B_WIDTH // B_BLOCKS
    a8 = jax.random.uniform(ks[15], (DEPTH, B_WIDTH), f32, 0.9, 0.999)
    a0 = a8 ** (1.0 / B_C)
    return {
        'x': nrm(ks[0], (BATCH, SEQ, D_MODEL), 1.0),
        'norm_mix': 1.0 + nrm(ks[1], (DEPTH, D_MODEL), 0.02),
        'w_in': nrm(ks[2], (DEPTH, D_MODEL, IN_COLS), D_MODEL ** -0.5),
        'b_gate': nrm(ks[3], (DEPTH, N_BRANCH * D_MODEL), 0.02),
        'diff_lq1': nrm(ks[4], (DEPTH, A_HD), 0.1),
        'diff_lk1': nrm(ks[5], (DEPTH, A_HD), 0.1),
        'diff_lq2': nrm(ks[6], (DEPTH, A_HD), 0.1),
        'diff_lk2': nrm(ks[7], (DEPTH, A_HD), 0.1),
        'diff_subln': 1.0 + nrm(ks[8], (DEPTH, 2 * A_HD), 0.02),
        'lru_conv_w': nrm(ks[9], (DEPTH, B_CONV, B_WIDTH), B_CONV ** -0.5),
        'lru_conv_b': nrm(ks[10], (DEPTH, B_WIDTH), 0.02),
        'lru_wa': nrm(ks[11], (DEPTH, B_BLOCKS, bw, bw), bw ** -0.5),
        'lru_ba': nrm(ks[12], (DEPTH, B_WIDTH), 0.02),
        'lru_wx': nrm(ks[13], (DEPTH, B_BLOCKS, bw, bw), bw ** -0.5),
        'lru_bx': nrm(ks[14], (DEPTH, B_WIDTH), 0.02),
        'lru_lambda': jnp.log(a0) - jnp.log1p(-a0),
        'fox_b_f': 2.0 + nrm(ks[16], (DEPTH, C_HEADS), 0.5),
        'swa_sinks': nrm(ks[17], (DEPTH, D_HEADS), 0.5),
        'w_branch': nrm(ks[18], (DEPTH, N_BRANCH, BRANCH_W, D_MODEL), BRANCH_W ** -0.5),
        'w_out': nrm(ks[19], (DEPTH, D_MODEL, D_MODEL), D_MODEL ** -0.5),
        'norm_ffn': 1.0 + nrm(ks[20], (DEPTH, D_MODEL), 0.02),
        'w_ffn_gate': nrm(ks[21], (DEPTH, D_MODEL, D_FF), D_MODEL ** -0.5),
        'w_ffn_up': nrm(ks[22], (DEPTH, D_MODEL, D_FF), D_MODEL ** -0.5),
        'w_ffn_down': nrm(ks[23], (DEPTH, D_FF, D_MODEL), D_FF ** -0.5),
        'norm_final': 1.0 + nrm(ks[24], (D_MODEL,), 0.02),
    }


def reference(x, norm_mix, w_in, b_gate, diff_lq1, diff_lk1, diff_lq2, diff_lk2, diff_subln,
              lru_conv_w, lru_conv_b, lru_wa, lru_ba, lru_wx, lru_bx, lru_lambda,
              fox_b_f, swa_sinks, w_branch, w_out, norm_ffn, w_ffn_gate, w_ffn_up, w_ffn_down,
              norm_final):
    B, S, _ = x.shape
    slopes_a = alibi_slopes(A_HEADS)
    slopes_d = alibi_slopes(D_HEADS)
    split_points = []
    acc = 0
    for size in IN_SIZES[:-1]:
        acc += size
        split_points.append(acc)

    h = x
    for l in range(DEPTH):
        u = rms_norm(h, norm_mix[l])
        z = u @ w_in[l]
        (aq, ak, av, bxin, bgate, cq, ck, cv, cf, dq, dk, dv, gz) = jnp.split(z, split_points, axis=-1)

        lam_init = 0.8 - 0.6 * math.exp(-0.3 * l)
        lam = (jnp.exp(jnp.sum(diff_lq1[l].astype(jnp.float32) * diff_lk1[l].astype(jnp.float32)))
               - jnp.exp(jnp.sum(diff_lq2[l].astype(jnp.float32) * diff_lk2[l].astype(jnp.float32)))
               + lam_init)
        ya = diff_attention(aq.reshape(B, S, A_HEADS, 2, A_HD), ak.reshape(B, S, A_HEADS, 2, A_HD),
                            av.reshape(B, S, A_HEADS, 2 * A_HD), lam, slopes_a)
        ya = (rms_norm(ya, diff_subln[l]) * (1.0 - lam_init)).reshape(B, S, BRANCH_W)

        yb = rg_lru_branch(bxin, bgate, lru_conv_w[l], lru_conv_b[l], lru_wa[l], lru_ba[l],
                           lru_wx[l], lru_bx[l], lru_lambda[l])

        logf = jax.nn.log_sigmoid((cf + fox_b_f[l]).astype(jnp.float32))
        yc = forgetting_attention(cq.reshape(B, S, C_HEADS, C_HD), ck.reshape(B, S, C_HEADS, C_HD),
                                  cv.reshape(B, S, C_HEADS, C_HD), logf)

        yd = sliding_window_sink_attention(dq.reshape(B, S, D_KV, D_GROUP, D_HD),
                                           dk.reshape(B, S, D_KV, D_HD), dv.reshape(B, S, D_KV, D_HD),
                                           swa_sinks[l], slopes_d)

        gates = jax.nn.sigmoid(gz + b_gate[l]).reshape(B, S, N_BRANCH, D_MODEL)
        branches = (ya, yb, yc, yd)
        mixed = gates[:, :, 0] * (branches[0] @ w_branch[l, 0])
        for n in range(1, N_BRANCH):
            mixed = mixed + gates[:, :, n] * (branches[n] @ w_branch[l, n])
        h = h + mixed @ w_out[l]

        v = rms_norm(h, norm_ffn[l])
        h = h + (jax.nn.silu(v @ w_ffn_gate[l]) * (v @ w_ffn_up[l])) @ w_ffn_down[l]

    return rms_norm(h, norm_final)
```

```python
import functools
import math

import jax
import jax.numpy as jnp
import numpy as np
from jax import lax
from jax.experimental import pallas as pl
from jax.experimental.pallas import tpu as pltpu

F32 = jnp.float32
BF16 = jnp.bfloat16

D_MODEL = 2048
N_BRANCH = 4
BRANCH_W = D_MODEL // 4
A_HEADS = 4
A_HD = BRANCH_W // (2 * A_HEADS)
B_BLOCKS = 8
B_CONV = 4
B_C = 8.0
C_HEADS = 4
C_HD = BRANCH_W // C_HEADS
D_HEADS = 8
D_KV = 2
D_GROUP = D_HEADS // D_KV
D_HD = BRANCH_W // D_HEADS
WINDOW = 128
SWA_BLOCK = 128
D_FF = -(-8 * D_MODEL // (3 * 256)) * 256
RMS_EPS = 1e-6
NEG_INF = -1e30

GZ0 = 0
AQ0 = GZ0 + N_BRANCH * D_MODEL
AK0 = AQ0 + BRANCH_W
AV0 = AK0 + BRANCH_W
BX0 = AV0 + BRANCH_W
BG0 = BX0 + BRANCH_W
CQ0 = BG0 + BRANCH_W
CK0 = CQ0 + BRANCH_W
CV0 = CK0 + BRANCH_W
DQ0 = CV0 + BRANCH_W
DK0 = DQ0 + BRANCH_W
DV0 = DK0 + D_KV * D_HD
NZ = DV0 + D_KV * D_HD
CF_PAD = 128

_R_CF0 = 8 * BRANCH_W
_R_DQ0 = _R_CF0 + C_HEADS
_R_GZ0 = _R_DQ0 + BRANCH_W + 2 * D_KV * D_HD

V7X_VMEM_LIMIT = 56 * 1024 * 1024

IN_TM, IN_TN = 1024, 768
A_TQ = 512
C_TQ = 1024
LRU_TS = 512
LRU_PAD = LRU_TS // 2
MERGE_TM = 256
FFN_TM, FFN_TF = 512, 512


def _cparams(sem):
    return pltpu.CompilerParams(dimension_semantics=sem, vmem_limit_bytes=V7X_VMEM_LIMIT)


def _rms(x, g):
    return x * lax.rsqrt(jnp.mean(x * x, axis=-1, keepdims=True) + RMS_EPS) * g


def _softplus(y):
    return jnp.maximum(y, 0.0) + jnp.log1p(jnp.exp(-jnp.abs(y)))


def _in_proj_kernel(h_ref, g_ref, w_ref, wcf_ref, z_ref, cf_ref, u_sc):
    @pl.when(pl.program_id(1) == 0)
    def _():
        u = _rms(h_ref[...], g_ref[...]).astype(BF16)
        u_sc[...] = u
        cf_ref[...] = jnp.dot(u, wcf_ref[...], preferred_element_type=F32)

    z_ref[...] = jnp.dot(u_sc[...], w_ref[...], preferred_element_type=F32).astype(BF16)


def _in_proj(h, g, w, wcf):
    t = h.shape[0]
    return pl.pallas_call(
        _in_proj_kernel,
        grid=(t // IN_TM, NZ // IN_TN),
        in_specs=[
            pl.BlockSpec((IN_TM, D_MODEL), lambda i, j: (i, 0)),
            pl.BlockSpec((1, D_MODEL), lambda i, j: (0, 0)),
            pl.BlockSpec((D_MODEL, IN_TN), lambda i, j: (0, j)),
            pl.BlockSpec((D_MODEL, CF_PAD), lambda i, j: (0, 0)),
        ],
        out_specs=[
            pl.BlockSpec((IN_TM, IN_TN), lambda i, j: (i, j)),
            pl.BlockSpec((IN_TM, CF_PAD), lambda i, j: (i, 0)),
        ],
        out_shape=[jax.ShapeDtypeStruct((t, NZ), BF16), jax.ShapeDtypeStruct((t, CF_PAD), F32)],
        scratch_shapes=[pltpu.VMEM((IN_TM, D_MODEL), BF16)],
        compiler_params=_cparams(("parallel", "arbitrary")),
        name="in_proj",
    )(h, g, w, wcf)


def _tri_tables(n):
    qi = [i for i in range(n) for _ in range(i + 1)]
    kj = [j for i in range(n) for j in range(i + 1)]
    return jnp.asarray(qi, jnp.int32), jnp.asarray(kj, jnp.int32)


def _flash_step(s, v, m_sc, l_sc, acc_sc):
    m_prev = m_sc[...]
    m_new = jnp.maximum(m_prev, jnp.max(s, axis=-1, keepdims=True))
    alpha = jnp.exp(m_prev - m_new)
    p = jnp.exp(s - m_new)
    l_sc[...] = alpha * l_sc[...] + jnp.sum(p, axis=-1, keepdims=True)
    acc_sc[...] = alpha * acc_sc[...] + jnp.dot(p.astype(BF16), v, preferred_element_type=F32)
    m_sc[...] = m_new


def _causal_mask(s, tq, tk):
    row = lax.broadcasted_iota(jnp.int32, s.shape, 0) % tq
    col = lax.broadcasted_iota(jnp.int32, s.shape, 1)
    return jnp.where(col <= row, s, NEG_INF)


def _diff_attn_kernel(qi_ref, kj_ref, slope_ref, lami_ref, q_ref, k_ref, v_ref, lv_ref, sg_ref, o_ref,
                      q2_sc, m_sc, l_sc, acc_sc):
    tq = A_TQ
    t = pl.program_id(2)
    i = qi_ref[t]
    j = kj_ref[t]

    @pl.when(j == 0)
    def _():
        q = q_ref[...]
        lane = lax.broadcasted_iota(jnp.int32, q.shape, 1)
        qs = q * jnp.asarray(A_HD ** -0.5, BF16)
        zero = jnp.zeros_like(qs)
        q2_sc[0:tq, :] = jnp.where(lane < A_HD, qs, zero)
        q2_sc[tq:2 * tq, :] = jnp.where(lane >= A_HD, qs, zero)
        m_sc[...] = jnp.full(m_sc.shape, -jnp.inf, F32)
        l_sc[...] = jnp.zeros(l_sc.shape, F32)
        acc_sc[...] = jnp.zeros(acc_sc.shape, F32)

    def scores():
        s = lax.dot_general(q2_sc[...], k_ref[...], (((1,), (1,)), ((), ())), preferred_element_type=F32)
        kpos = lax.broadcasted_iota(jnp.int32, (1, tq), 1) + (j - i) * tq
        return s + kpos.astype(F32) * slope_ref[pl.program_id(1)]

    @pl.when(j < i)
    def _():
        _flash_step(scores(), v_ref[...], m_sc, l_sc, acc_sc)

    @pl.when(j == i)
    def _():
        _flash_step(_causal_mask(scores(), tq, tq), v_ref[...], m_sc, l_sc, acc_sc)
        lv = lv_ref[...]
        lam = (jnp.exp(jnp.sum(lv[0:1] * lv[1:2], axis=-1, keepdims=True))
               - jnp.exp(jnp.sum(lv[2:3] * lv[3:4], axis=-1, keepdims=True)) + lami_ref[0])
        acc = acc_sc[...]
        l = l_sc[...]
        o = acc[0:tq] / l[0:tq] - lam * (acc[tq:2 * tq] / l[tq:2 * tq])
        o_ref[...] = (_rms(o, sg_ref[...]) * lami_ref[1]).astype(BF16)


def _diff_attn(z, lvec, subln, lam_init, batch, seq):
    nq = seq // A_TQ
    qi, kj = _tri_tables(nq)
    slopes = jnp.asarray(np.exp2(-8.0 * np.arange(1, A_HEADS + 1, dtype=np.float32) / A_HEADS), F32)
    qc, kc, vc = AQ0 // 128, AK0 // 128, AV0 // 128
    smem = pl.BlockSpec(memory_space=pltpu.SMEM)
    grid_spec = pltpu.PrefetchScalarGridSpec(
        num_scalar_prefetch=2,
        grid=(batch, A_HEADS, int(qi.shape[0])),
        in_specs=[
            smem, smem,
            pl.BlockSpec((A_TQ, 128), lambda b, h, t, qi, kj: (b * nq + qi[t], qc + h)),
            pl.BlockSpec((A_TQ, 128), lambda b, h, t, qi, kj: (b * nq + kj[t], kc + h)),
            pl.BlockSpec((A_TQ, 128), lambda b, h, t, qi, kj: (b * nq + kj[t], vc + h)),
            pl.BlockSpec((4, A_HD), lambda b, h, t, qi, kj: (0, 0)),
            pl.BlockSpec((1, 2 * A_HD), lambda b, h, t, qi, kj: (0, 0)),
        ],
        out_specs=pl.BlockSpec((A_TQ, 128), lambda b, h, t, qi, kj: (b * nq + qi[t], h)),
        scratch_shapes=[
            pltpu.VMEM((2 * A_TQ, 128), BF16),
            pltpu.VMEM((2 * A_TQ, 1), F32),
            pltpu.VMEM((2 * A_TQ, 1), F32),
            pltpu.VMEM((2 * A_TQ, 128), F32),
        ],
    )
    return pl.pallas_call(
        _diff_attn_kernel,
        grid_spec=grid_spec,
        out_shape=jax.ShapeDtypeStruct((batch * seq, BRANCH_W), BF16),
        compiler_params=_cparams(("parallel", "parallel", "arbitrary")),
        name="diff_attn",
    )(qi, kj, slopes, lam_init, z, z, z, lvec, subln)


def _fox_cum_kernel(cf_ref, bf_ref, o_ref):
    x = cf_ref[...] + bf_ref[...]
    logf = jnp.minimum(x, 0.0) - jnp.log1p(jnp.exp(-jnp.abs(x)))
    rows, seq = logf.shape
    lane = lax.broadcasted_iota(jnp.int32, (rows, 128), 1)
    carry = jnp.zeros((rows, 1), F32)
    for c in range(seq // 128):
        y = logf[:, c * 128:(c + 1) * 128]
        sh = 1
        while sh < 128:
            y = y + jnp.where(lane >= sh, pltpu.roll(y, sh, axis=1), 0.0)
            sh *= 2
        y = y + carry
        o_ref[:, c * 128:(c + 1) * 128] = y
        carry = y[:, 127:128]


def _fox_cum(cf_t, bf_t):
    rows, seq = cf_t.shape
    return pl.pallas_call(
        _fox_cum_kernel,
        grid=(1,),
        in_specs=[pl.BlockSpec((rows, seq), lambda i: (0, 0)), pl.BlockSpec((rows, 1), lambda i: (0, 0))],
        out_specs=pl.BlockSpec((rows, seq), lambda i: (0, 0)),
        out_shape=jax.ShapeDtypeStruct((rows, seq), F32),
        compiler_params=_cparams(("arbitrary",)),
        name="fox_cum",
    )(cf_t, bf_t)


def _fox_attn_kernel(qi_ref, kj_ref, q_ref, k_ref, v_ref, cum_ref, o_ref, m_sc, l_sc, acc_sc):
    tq = C_TQ
    t = pl.program_id(2)
    i = qi_ref[t]
    j = kj_ref[t]

    @pl.when(j == 0)
    def _():
        m_sc[...] = jnp.full(m_sc.shape, -jnp.inf, F32)
        l_sc[...] = jnp.zeros(l_sc.shape, F32)
        acc_sc[...] = jnp.zeros(acc_sc.shape, F32)

    def scores():
        s = lax.dot_general(q_ref[...], k_ref[...], (((1,), (1,)), ((), ())), preferred_element_type=F32)
        return s * (C_HD ** -0.5) - cum_ref[0]

    @pl.when(j < i)
    def _():
        _flash_step(scores(), v_ref[...], m_sc, l_sc, acc_sc)

    @pl.when(j == i)
    def _():
        _flash_step(_causal_mask(scores(), tq, tq), v_ref[...], m_sc, l_sc, acc_sc)
        o_ref[...] = (acc_sc[...] / l_sc[...]).astype(BF16)


def _fox_attn(z, cum, batch, seq):
    nq = seq // C_TQ
    qi, kj = _tri_tables(nq)
    qc, kc, vc = CQ0 // 128, CK0 // 128, CV0 // 128
    grid_spec = pltpu.PrefetchScalarGridSpec(
        num_scalar_prefetch=2,
        grid=(batch, C_HEADS, int(qi.shape[0])),
        in_specs=[
            pl.BlockSpec((C_TQ, 128), lambda b, h, t, qi, kj: (b * nq + qi[t], qc + h)),
            pl.BlockSpec((C_TQ, 128), lambda b, h, t, qi, kj: (b * nq + kj[t], kc + h)),
            pl.BlockSpec((C_TQ, 128), lambda b, h, t, qi, kj: (b * nq + kj[t], vc + h)),
            pl.BlockSpec((1, 1, C_TQ), lambda b, h, t, qi, kj: (b * C_HEADS + h, 0, kj[t])),
        ],
        out_specs=pl.BlockSpec((C_TQ, 128), lambda b, h, t, qi, kj: (b * nq + qi[t], h)),
        scratch_shapes=[
            pltpu.VMEM((C_TQ, 1), F32),
            pltpu.VMEM((C_TQ, 1), F32),
            pltpu.VMEM((C_TQ, 128), F32),
        ],
    )
    return pl.pallas_call(
        _fox_attn_kernel,
        grid_spec=grid_spec,
        out_shape=jax.ShapeDtypeStruct((batch * seq, BRANCH_W), BF16),
        compiler_params=_cparams(("parallel", "parallel", "arbitrary")),
        name="fox_attn",
    )(qi, kj, z, z, z, cum)


def _swa_kernel(sink_ref, q_ref, kc_ref, kp_ref, vc_ref, vp_ref, bias_ref, o_ref):
    q = q_ref[...] * jnp.asarray(D_HD ** -0.5, BF16)
    kk = jnp.concatenate([kp_ref[...], kc_ref[...]], axis=0)
    vv = jnp.concatenate([vp_ref[...], vc_ref[...]], axis=0)
    outs = []
    for kv in range(D_KV):
        k_h = kk[:, kv * D_HD:(kv + 1) * D_HD]
        v_h = vv[:, kv * D_HD:(kv + 1) * D_HD]
        for g in range(D_GROUP):
            hh = kv * D_GROUP + g
            q_h = q[:, hh * D_HD:(hh + 1) * D_HD]
            s = lax.dot_general(q_h, k_h, (((1,), (1,)), ((), ())), preferred_element_type=F32)
            lg = s + bias_ref[0, hh]
            sink = sink_ref[hh]
            m = jnp.maximum(jnp.max(lg, axis=-1, keepdims=True), sink)
            e = jnp.exp(lg - m)
            p = e / (jnp.sum(e, axis=-1, keepdims=True) + jnp.exp(sink - m))
            outs.append(jnp.dot(p.astype(BF16), v_h, preferred_element_type=F32))
    o_ref[...] = jnp.concatenate(outs, axis=1).astype(BF16)


def _swa_bias():
    slopes = np.exp2(-8.0 * np.arange(1, D_HEADS + 1, dtype=np.float32) / D_HEADS).astype(np.float32)
    qi = np.arange(SWA_BLOCK)
    kj = np.arange(2 * SWA_BLOCK) - SWA_BLOCK
    dist = (qi[:, None] - kj[None, :]).astype(np.float32)
    valid = (dist >= 0) & (dist < WINDOW)
    general = np.where(valid[None], -(slopes[:, None, None] * dist[None]), np.float32(NEG_INF))
    first = np.where((kj >= 0)[None, None, :], general, np.float32(NEG_INF))
    return jnp.asarray(np.stack([first, general]).astype(np.float32))


def _swa(z, sinks, batch, seq):
    nb = seq // SWA_BLOCK
    qc, kc, vc = DQ0 // BRANCH_W, DK0 // 128, DV0 // 128
    cur = lambda b, n: b * nb + n
    prev = lambda b, n: b * nb + jnp.maximum(n - 1, 0)
    return pl.pallas_call(
        _swa_kernel,
        grid=(batch, nb),
        in_specs=[
            pl.BlockSpec(memory_space=pltpu.SMEM),
            pl.BlockSpec((SWA_BLOCK, BRANCH_W), lambda b, n: (cur(b, n), qc)),
            pl.BlockSpec((SWA_BLOCK, 128), lambda b, n: (cur(b, n), kc)),
            pl.BlockSpec((SWA_BLOCK, 128), lambda b, n: (prev(b, n), kc)),
            pl.BlockSpec((SWA_BLOCK, 128), lambda b, n: (cur(b, n), vc)),
            pl.BlockSpec((SWA_BLOCK, 128), lambda b, n: (prev(b, n), vc)),
            pl.BlockSpec((1, D_HEADS, SWA_BLOCK, 2 * SWA_BLOCK), lambda b, n: (jnp.minimum(n, 1), 0, 0, 0)),
        ],
        out_specs=pl.BlockSpec((SWA_BLOCK, BRANCH_W), lambda b, n: (cur(b, n), 0)),
        out_shape=jax.ShapeDtypeStruct((batch * seq, BRANCH_W), BF16),
        compiler_params=_cparams(("parallel", "arbitrary")),
        name="swa",
    )(sinks, z, z, z, z, z, _swa_bias())


def _lru_kernel(x_ref, g_ref, cw_ref, cb_ref, wa_ref, ba_ref, wx_ref, bx_ref, lam_ref, o_ref,
                xpad_sc, apad_sc, upad_sc, hc_sc):
    ts, pad = LRU_TS, LRU_PAD

    @pl.when(pl.program_id(1) == 0)
    def _():
        xpad_sc[0:8, :] = jnp.zeros((8, BRANCH_W), F32)
        hc_sc[...] = jnp.zeros(hc_sc.shape, F32)
        apad_sc[0:pad, :] = jnp.ones((pad, BRANCH_W), F32)
        upad_sc[0:pad, :] = jnp.zeros((pad, BRANCH_W), F32)

    x = x_ref[...].astype(F32)
    xpad_sc[8:8 + ts, :] = x
    cw = cw_ref[...]
    xc = (cw[3:4] * x + cw[2:3] * xpad_sc[7:7 + ts, :] + cw[1:2] * xpad_sc[6:6 + ts, :]
          + cw[0:1] * xpad_sc[5:5 + ts, :] + cb_ref[...])
    xpad_sc[0:8, :] = x[ts - 8:ts]

    xcb = xc.astype(BF16)
    r = jax.nn.sigmoid(jnp.dot(xcb, wa_ref[...], preferred_element_type=F32) + ba_ref[...])
    gi = jax.nn.sigmoid(jnp.dot(xcb, wx_ref[...], preferred_element_type=F32) + bx_ref[...])
    log_a = (-B_C * r) * _softplus(-lam_ref[...])
    a = jnp.exp(log_a)
    u = jnp.sqrt(1.0 - jnp.exp(2.0 * log_a)) * (gi * xc)

    sh = 1
    while sh < ts:
        apad_sc[pad:pad + ts, :] = a
        upad_sc[pad:pad + ts, :] = u
        u = a * upad_sc[pad - sh:pad - sh + ts, :] + u
        a = a * apad_sc[pad - sh:pad - sh + ts, :]
        sh *= 2
    h = a * hc_sc[...] + u
    hc_sc[...] = h[ts - 1:ts]
    o_ref[...] = (jax.nn.gelu(g_ref[...].astype(F32)) * h).astype(BF16)


def _lru(z, cw, cb, wa, ba, wx, bx, lam, batch, seq):
    ns = seq // LRU_TS
    xc_, gc_ = BX0 // BRANCH_W, BG0 // BRANCH_W
    const = lambda shape: pl.BlockSpec(shape, lambda b, s: (0, 0))
    return pl.pallas_call(
        _lru_kernel,
        grid=(batch, ns),
        in_specs=[
            pl.BlockSpec((LRU_TS, BRANCH_W), lambda b, s: (b * ns + s, xc_)),
            pl.BlockSpec((LRU_TS, BRANCH_W), lambda b, s: (b * ns + s, gc_)),
            const((B_CONV, BRANCH_W)), const((1, BRANCH_W)),
            const((BRANCH_W, BRANCH_W)), const((1, BRANCH_W)),
            const((BRANCH_W, BRANCH_W)), const((1, BRANCH_W)),
            const((1, BRANCH_W)),
        ],
        out_specs=pl.BlockSpec((LRU_TS, BRANCH_W), lambda b, s: (b * ns + s, 0)),
        out_shape=jax.ShapeDtypeStruct((batch * seq, BRANCH_W), BF16),
        scratch_shapes=[
            pltpu.VMEM((LRU_TS + 8, BRANCH_W), F32),
            pltpu.VMEM((LRU_TS + LRU_PAD, BRANCH_W), F32),
            pltpu.VMEM((LRU_TS + LRU_PAD, BRANCH_W), F32),
            pltpu.VMEM((1, BRANCH_W), F32),
        ],
        compiler_params=_cparams(("parallel", "arbitrary")),
        name="lru",
    )(z, z, cw, cb, wa, ba, wx, bx, lam)


def _merge_kernel(h_ref, ya_ref, yb_ref, yc_ref, yd_ref, g0_ref, g1_ref, g2_ref, g3_ref, bg_ref, wb_ref, wo_ref,
                  o_ref):
    ys = (ya_ref, yb_ref, yc_ref, yd_ref)
    gs = (g0_ref, g1_ref, g2_ref, g3_ref)
    mixed = None
    for n in range(N_BRANCH):
        proj = jnp.dot(ys[n][...], wb_ref[n], preferred_element_type=F32)
        term = jax.nn.sigmoid(gs[n][...].astype(F32) + bg_ref[n:n + 1, :]) * proj
        mixed = term if mixed is None else mixed + term
    o_ref[...] = h_ref[...] + jnp.dot(mixed.astype(BF16), wo_ref[...], preferred_element_type=F32)


def _merge(h, ya, yb, yc, yd, z, bg, wb, wo):
    t = h.shape[0]
    row = lambda shape: pl.BlockSpec(shape, lambda i: (i, 0))
    gate = lambda n: pl.BlockSpec((MERGE_TM, D_MODEL), lambda i: (i, GZ0 // D_MODEL + n))
    return pl.pallas_call(
        _merge_kernel,
        grid=(t // MERGE_TM,),
        in_specs=[
            row((MERGE_TM, D_MODEL)),
            row((MERGE_TM, BRANCH_W)), row((MERGE_TM, BRANCH_W)), row((MERGE_TM, BRANCH_W)),
            row((MERGE_TM, BRANCH_W)),
            gate(0), gate(1), gate(2), gate(3),
            pl.BlockSpec((N_BRANCH, D_MODEL), lambda i: (0, 0)),
            pl.BlockSpec((N_BRANCH, BRANCH_W, D_MODEL), lambda i: (0, 0, 0), pipeline_mode=pl.Buffered(1)),
            pl.BlockSpec((D_MODEL, D_MODEL), lambda i: (0, 0), pipeline_mode=pl.Buffered(1)),
        ],
        out_specs=row((MERGE_TM, D_MODEL)),
        out_shape=jax.ShapeDtypeStruct((t, D_MODEL), F32),
        compiler_params=_cparams(("parallel",)),
        name="merge",
    )(h, ya, yb, yc, yd, z, z, z, z, bg, wb, wo)


def _ffn_kernel(h_ref, g_ref, wg_ref, wu_ref, wd_ref, gf_ref, o_ref, v_sc, *, final_norm):
    j = pl.program_id(1)

    @pl.when(j == 0)
    def _():
        x = h_ref[...]
        v_sc[...] = _rms(x, g_ref[...]).astype(BF16)
        o_ref[...] = x

    v = v_sc[...]
    a = jnp.dot(v, wg_ref[...], preferred_element_type=F32)
    b = jnp.dot(v, wu_ref[...], preferred_element_type=F32)
    t = (a * jax.nn.sigmoid(a)) * b
    o_ref[...] += jnp.dot(t.astype(BF16), wd_ref[...], preferred_element_type=F32)

    if final_norm:
        @pl.when(j == pl.num_programs(1) - 1)
        def _():
            o_ref[...] = _rms(o_ref[...], gf_ref[...])


def _ffn(h, g, wg, wu, wd, gf, final_norm):
    t = h.shape[0]
    return pl.pallas_call(
        functools.partial(_ffn_kernel, final_norm=final_norm),
        grid=(t // FFN_TM, D_FF // FFN_TF),
        in_specs=[
            pl.BlockSpec((FFN_TM, D_MODEL), lambda i, j: (i, 0)),
            pl.BlockSpec((1, D_MODEL), lambda i, j: (0, 0)),
            pl.BlockSpec((D_MODEL, FFN_TF), lambda i, j: (0, j)),
            pl.BlockSpec((D_MODEL, FFN_TF), lambda i, j: (0, j)),
            pl.BlockSpec((FFN_TF, D_MODEL), lambda i, j: (j, 0)),
            pl.BlockSpec((1, D_MODEL), lambda i, j: (0, 0)),
        ],
        out_specs=pl.BlockSpec((FFN_TM, D_MODEL), lambda i, j: (i, 0)),
        out_shape=jax.ShapeDtypeStruct((t, D_MODEL), F32),
        scratch_shapes=[pltpu.VMEM((FFN_TM, D_MODEL), BF16)],
        compiler_params=_cparams(("parallel", "arbitrary")),
        name="ffn_final" if final_norm else "ffn",
    )(h, g, wg, wu, wd, gf)


def _block_diag(w):
    depth, nb, c, d = w.shape
    eye = jnp.eye(nb, dtype=w.dtype)
    return jnp.einsum('lncd,nm->lncmd', w, eye).reshape(depth, nb * c, nb * d)


def kernel(x, norm_mix, w_in, b_gate, diff_lq1, diff_lk1, diff_lq2, diff_lk2, diff_subln, lru_conv_w, lru_conv_b,
           lru_wa, lru_ba, lru_wx, lru_bx, lru_lambda, fox_b_f, swa_sinks, w_branch, w_out, norm_ffn, w_ffn_gate,
           w_ffn_up, w_ffn_down, norm_final):
    batch, seq, d = x.shape
    depth = w_in.shape[0]
    assert d == D_MODEL and seq % max(C_TQ, LRU_TS) == 0 and (batch * seq) % IN_TM == 0
    t = batch * seq

    w_main = jnp.concatenate([w_in[:, :, _R_GZ0:], w_in[:, :, :_R_CF0], w_in[:, :, _R_DQ0:_R_GZ0]],
                             axis=2).astype(BF16)
    w_cf = jnp.pad(w_in[:, :, _R_CF0:_R_DQ0], ((0, 0), (0, 0), (0, CF_PAD - C_HEADS))).astype(BF16)
    wb = w_branch.astype(BF16)
    wo = w_out.astype(BF16)
    wg = w_ffn_gate.astype(BF16)
    wu = w_ffn_up.astype(BF16)
    wd = w_ffn_down.astype(BF16)
    wa = _block_diag(lru_wa).astype(BF16)
    wx = _block_diag(lru_wx).astype(BF16)
    lvec = jnp.stack([diff_lq1, diff_lk1, diff_lq2, diff_lk2], axis=1).astype(F32)
    bg = b_gate.reshape(depth, N_BRANCH, D_MODEL)
    gf = norm_final.reshape(1, D_MODEL)

    h = x.reshape(t, D_MODEL)
    for l in range(depth):
        lam_init = 0.8 - 0.6 * math.exp(-0.3 * l)
        z, cf = _in_proj(h, norm_mix[l].reshape(1, D_MODEL), w_main[l], w_cf[l])

        ya = _diff_attn(z, lvec[l], diff_subln[l].reshape(1, 2 * A_HD),
                        jnp.asarray([lam_init, 1.0 - lam_init], F32), batch, seq)
        yb = _lru(z, lru_conv_w[l], lru_conv_b[l].reshape(1, -1), wa[l], lru_ba[l].reshape(1, -1), wx[l],
                  lru_bx[l].reshape(1, -1), lru_lambda[l].reshape(1, -1), batch, seq)
        cf_t = cf[:, :C_HEADS].reshape(batch, seq, C_HEADS).transpose(0, 2, 1).reshape(batch * C_HEADS, seq)
        bf_t = jnp.tile(fox_b_f[l], batch).reshape(batch * C_HEADS, 1)
        cum = _fox_cum(cf_t, bf_t).reshape(batch * C_HEADS, 1, seq)
        yc = _fox_attn(z, cum, batch, seq)
        yd = _swa(z, swa_sinks[l], batch, seq)

        h = _merge(h, ya, yb, yc, yd, z, bg[l], wb[l], wo[l])
        h = _ffn(h, norm_ffn[l].reshape(1, D_MODEL), wg[l], wu[l], wd[l], gf, final_norm=(l == depth - 1))
    return h.reshape(batch, seq, D_MODEL)
```

```python
import functools
import math

import jax
import jax.numpy as jnp
import numpy as np
from jax import lax
from jax.experimental import pallas as pl
from jax.experimental.pallas import tpu as pltpu

F32 = jnp.float32
BF16 = jnp.bfloat16

D_MODEL = 2048
N_BRANCH = 4
BRANCH_W = D_MODEL // 4
A_HEADS = 4
A_HD = BRANCH_W // (2 * A_HEADS)
B_BLOCKS = 8
B_CONV = 4
B_C = 8.0
C_HEADS = 4
C_HD = BRANCH_W // C_HEADS
D_HEADS = 8
D_KV = 2
D_GROUP = D_HEADS // D_KV
D_HD = BRANCH_W // D_HEADS
WINDOW = 128
SWA_BLOCK = 128
D_FF = -(-8 * D_MODEL // (3 * 256)) * 256
RMS_EPS = 1e-6
NEG_INF = -1e30

GZ0 = 0
AQ0 = GZ0 + N_BRANCH * D_MODEL
AK0 = AQ0 + BRANCH_W
AV0 = AK0 + BRANCH_W
BX0 = AV0 + BRANCH_W
BG0 = BX0 + BRANCH_W
CQ0 = BG0 + BRANCH_W
CK0 = CQ0 + BRANCH_W
CV0 = CK0 + BRANCH_W
DQ0 = CV0 + BRANCH_W
DK0 = DQ0 + BRANCH_W
DV0 = DK0 + D_KV * D_HD
NZ = DV0 + D_KV * D_HD
CF_PAD = 128

_R_CF0 = 8 * BRANCH_W
_R_DQ0 = _R_CF0 + C_HEADS
_R_GZ0 = _R_DQ0 + BRANCH_W + 2 * D_KV * D_HD

V7X_VMEM_LIMIT = 56 * 1024 * 1024

IN_TM, IN_TN = 1024, 768
A_TQ = 512
FLASH_ROWS = 256
C_TQ = 1024
LRU_TS = 512
LRU_PAD = LRU_TS // 2
MERGE_TM = 256
FFN_TM, FFN_TF = 512, 512


def _cparams(sem):
    return pltpu.CompilerParams(dimension_semantics=sem, vmem_limit_bytes=V7X_VMEM_LIMIT)


def _rms(x, g):
    return x * lax.rsqrt(jnp.mean(x * x, axis=-1, keepdims=True) + RMS_EPS) * g


def _softplus(y):
    return jnp.maximum(y, 0.0) + jnp.log1p(jnp.exp(-jnp.abs(y)))


def _in_proj_kernel(h_ref, g_ref, w_ref, wcf_ref, z_ref, cf_ref, u_sc):
    @pl.when(pl.program_id(1) == 0)
    def _():
        u = _rms(h_ref[...], g_ref[...]).astype(BF16)
        u_sc[...] = u
        cf_ref[...] = jnp.dot(u, wcf_ref[...], preferred_element_type=F32)

    z_ref[...] = jnp.dot(u_sc[...], w_ref[...], preferred_element_type=F32).astype(BF16)


def _in_proj(h, g, w, wcf):
    t = h.shape[0]
    return pl.pallas_call(
        _in_proj_kernel,
        grid=(t // IN_TM, NZ // IN_TN),
        in_specs=[
            pl.BlockSpec((IN_TM, D_MODEL), lambda i, j: (i, 0)),
            pl.BlockSpec((1, D_MODEL), lambda i, j: (0, 0)),
            pl.BlockSpec((D_MODEL, IN_TN), lambda i, j: (0, j)),
            pl.BlockSpec((D_MODEL, CF_PAD), lambda i, j: (0, 0)),
        ],
        out_specs=[
            pl.BlockSpec((IN_TM, IN_TN), lambda i, j: (i, j)),
            pl.BlockSpec((IN_TM, CF_PAD), lambda i, j: (i, 0)),
        ],
        out_shape=[jax.ShapeDtypeStruct((t, NZ), BF16), jax.ShapeDtypeStruct((t, CF_PAD), F32)],
        scratch_shapes=[pltpu.VMEM((IN_TM, D_MODEL), BF16)],
        compiler_params=_cparams(("parallel", "arbitrary")),
        name="in_proj",
    )(h, g, w, wcf)


def _tri_tables(n):
    qi = [i for i in range(n) for _ in range(i + 1)]
    kj = [j for i in range(n) for j in range(i + 1)]
    return jnp.asarray(qi, jnp.int32), jnp.asarray(kj, jnp.int32)


def _flash_rows(q_ref, k_ref, v2_sc, bias, m_sc, acc_sc, r0, diag_q0, scale):
    rows = FLASH_ROWS
    nk = k_ref.shape[0] if diag_q0 is None else diag_q0 + rows
    s = lax.dot_general(q_ref[r0:r0 + rows, :], k_ref[0:nk, :], (((1,), (1,)), ((), ())),
                        preferred_element_type=F32)
    if scale is not None:
        s = s * scale
    s = s + bias[:, 0:nk]
    if diag_q0 is not None:
        qpos = lax.broadcasted_iota(jnp.int32, s.shape, 0) + diag_q0
        kpos = lax.broadcasted_iota(jnp.int32, s.shape, 1)
        s = jnp.where(kpos <= qpos, s, NEG_INF)
    m_prev = m_sc[r0:r0 + rows, :]
    m_new = jnp.maximum(m_prev, jnp.max(s, axis=-1, keepdims=True))
    alpha = jnp.exp(m_prev - m_new)
    p = jnp.exp(s - jnp.concatenate([m_new] * (nk // 128), axis=1))
    pv = jnp.dot(p.astype(BF16), v2_sc[0:nk, :], preferred_element_type=F32)
    acc_sc[r0:r0 + rows, :] = jnp.concatenate([alpha, alpha], axis=1) * acc_sc[r0:r0 + rows, :] + pv
    m_sc[r0:r0 + rows, :] = m_new


def _flash_block(q_ref, k_ref, v2_sc, bias, m_sc, acc_sc, n_rows, tq, diag, scale):
    for r0 in range(0, n_rows, FLASH_ROWS):
        _flash_rows(q_ref, k_ref, v2_sc, bias, m_sc, acc_sc, r0, (r0 % tq) if diag else None, scale)


def _flash_init(v2_sc, m_sc, acc_sc):
    v2_sc[:, 128:256] = jnp.ones((v2_sc.shape[0], 128), BF16)
    m_sc[...] = jnp.full(m_sc.shape, -jnp.inf, F32)
    acc_sc[...] = jnp.zeros(acc_sc.shape, F32)


def _diff_attn_kernel(qi_ref, kj_ref, slope_ref, lami_ref, q_ref, k_ref, v_ref, lv_ref, sg_ref, o_ref,
                      q2_sc, v2_sc, m_sc, acc_sc):
    tq = A_TQ
    t = pl.program_id(2)
    i = qi_ref[t]
    j = kj_ref[t]

    @pl.when(j == 0)
    def _():
        q = q_ref[...]
        lane = lax.broadcasted_iota(jnp.int32, q.shape, 1)
        qs = q * jnp.asarray(A_HD ** -0.5, BF16)
        zero = jnp.zeros_like(qs)
        q2_sc[0:tq, :] = jnp.where(lane < A_HD, qs, zero)
        q2_sc[tq:2 * tq, :] = jnp.where(lane >= A_HD, qs, zero)
        _flash_init(v2_sc, m_sc, acc_sc)

    v2_sc[:, 0:128] = v_ref[...]
    kpos = lax.broadcasted_iota(jnp.int32, (1, tq), 1) + (j - i) * tq
    bias = kpos.astype(F32) * slope_ref[pl.program_id(1)]

    @pl.when(j < i)
    def _():
        _flash_block(q2_sc, k_ref, v2_sc, bias, m_sc, acc_sc, 2 * tq, tq, False, None)

    @pl.when(j == i)
    def _():
        _flash_block(q2_sc, k_ref, v2_sc, bias, m_sc, acc_sc, 2 * tq, tq, True, None)
        lv = lv_ref[...]
        lam = (jnp.exp(jnp.sum(lv[0:1] * lv[1:2], axis=-1, keepdims=True))
               - jnp.exp(jnp.sum(lv[2:3] * lv[3:4], axis=-1, keepdims=True)) + lami_ref[0])
        acc = acc_sc[...]
        o = acc[0:tq, 0:128] / acc[0:tq, 128:256] - lam * (acc[tq:2 * tq, 0:128] / acc[tq:2 * tq, 128:256])
        o_ref[...] = (_rms(o, sg_ref[...]) * lami_ref[1]).astype(BF16)


def _diff_attn(z, lvec, subln, lam_init, batch, seq):
    nq = seq // A_TQ
    qi, kj = _tri_tables(nq)
    slopes = jnp.asarray(np.exp2(-8.0 * np.arange(1, A_HEADS + 1, dtype=np.float32) / A_HEADS), F32)
    qc, kc, vc = AQ0 // 128, AK0 // 128, AV0 // 128
    smem = pl.BlockSpec(memory_space=pltpu.SMEM)
    grid_spec = pltpu.PrefetchScalarGridSpec(
        num_scalar_prefetch=2,
        grid=(batch, A_HEADS, int(qi.shape[0])),
        in_specs=[
            smem, smem,
            pl.BlockSpec((A_TQ, 128), lambda b, h, t, qi, kj: (b * nq + qi[t], qc + h)),
            pl.BlockSpec((A_TQ, 128), lambda b, h, t, qi, kj: (b * nq + kj[t], kc + h)),
            pl.BlockSpec((A_TQ, 128), lambda b, h, t, qi, kj: (b * nq + kj[t], vc + h)),
            pl.BlockSpec((4, A_HD), lambda b, h, t, qi, kj: (0, 0)),
            pl.BlockSpec((1, 2 * A_HD), lambda b, h, t, qi, kj: (0, 0)),
        ],
        out_specs=pl.BlockSpec((A_TQ, 128), lambda b, h, t, qi, kj: (b * nq + qi[t], h)),
        scratch_shapes=[
            pltpu.VMEM((2 * A_TQ, 128), BF16),
            pltpu.VMEM((A_TQ, 256), BF16),
            pltpu.VMEM((2 * A_TQ, 128), F32),
            pltpu.VMEM((2 * A_TQ, 256), F32),
        ],
    )
    return pl.pallas_call(
        _diff_attn_kernel,
        grid_spec=grid_spec,
        out_shape=jax.ShapeDtypeStruct((batch * seq, BRANCH_W), BF16),
        compiler_params=_cparams(("parallel", "parallel", "arbitrary")),
        name="diff_attn",
    )(qi, kj, slopes, lam_init, z, z, z, lvec, subln)


def _fox_cum_kernel(cf_ref, bf_ref, o_ref):
    x = cf_ref[...] + bf_ref[...]
    logf = jnp.minimum(x, 0.0) - jnp.log1p(jnp.exp(-jnp.abs(x)))
    rows, seq = logf.shape
    lane = lax.broadcasted_iota(jnp.int32, (rows, 128), 1)
    carry = jnp.zeros((rows, 1), F32)
    for c in range(seq // 128):
        y = logf[:, c * 128:(c + 1) * 128]
        sh = 1
        while sh < 128:
            y = y + jnp.where(lane >= sh, pltpu.roll(y, sh, axis=1), 0.0)
            sh *= 2
        y = y + carry
        o_ref[:, c * 128:(c + 1) * 128] = y
        carry = y[:, 127:128]


def _fox_cum(cf_t, bf_t):
    rows, seq = cf_t.shape
    return pl.pallas_call(
        _fox_cum_kernel,
        grid=(1,),
        in_specs=[pl.BlockSpec((rows, seq), lambda i: (0, 0)), pl.BlockSpec((rows, 1), lambda i: (0, 0))],
        out_specs=pl.BlockSpec((rows, seq), lambda i: (0, 0)),
        out_shape=jax.ShapeDtypeStruct((rows, seq), F32),
        compiler_params=_cparams(("arbitrary",)),
        name="fox_cum",
    )(cf_t, bf_t)


def _fox_attn_kernel(qi_ref, kj_ref, q_ref, k_ref, v_ref, cum_ref, o_ref, v2_sc, m_sc, acc_sc):
    tq = C_TQ
    t = pl.program_id(2)
    i = qi_ref[t]
    j = kj_ref[t]

    @pl.when(j == 0)
    def _():
        _flash_init(v2_sc, m_sc, acc_sc)

    v2_sc[:, 0:128] = v_ref[...]
    bias = -cum_ref[0]
    scale = C_HD ** -0.5

    @pl.when(j < i)
    def _():
        _flash_block(q_ref, k_ref, v2_sc, bias, m_sc, acc_sc, tq, tq, False, scale)

    @pl.when(j == i)
    def _():
        _flash_block(q_ref, k_ref, v2_sc, bias, m_sc, acc_sc, tq, tq, True, scale)
        acc = acc_sc[...]
        o_ref[...] = (acc[:, 0:128] / acc[:, 128:256]).astype(BF16)


def _fox_attn(z, cum, batch, seq):
    nq = seq // C_TQ
    qi, kj = _tri_tables(nq)
    qc, kc, vc = CQ0 // 128, CK0 // 128, CV0 // 128
    grid_spec = pltpu.PrefetchScalarGridSpec(
        num_scalar_prefetch=2,
        grid=(batch, C_HEADS, int(qi.shape[0])),
        in_specs=[
            pl.BlockSpec((C_TQ, 128), lambda b, h, t, qi, kj: (b * nq + qi[t], qc + h)),
            pl.BlockSpec((C_TQ, 128), lambda b, h, t, qi, kj: (b * nq + kj[t], kc + h)),
            pl.BlockSpec((C_TQ, 128), lambda b, h, t, qi, kj: (b * nq + kj[t], vc + h)),
            pl.BlockSpec((1, 1, C_TQ), lambda b, h, t, qi, kj: (b * C_HEADS + h, 0, kj[t])),
        ],
        out_specs=pl.BlockSpec((C_TQ, 128), lambda b, h, t, qi, kj: (b * nq + qi[t], h)),
        scratch_shapes=[
            pltpu.VMEM((C_TQ, 256), BF16),
            pltpu.VMEM((C_TQ, 128), F32),
            pltpu.VMEM((C_TQ, 256), F32),
        ],
    )
    return pl.pallas_call(
        _fox_attn_kernel,
        grid_spec=grid_spec,
        out_shape=jax.ShapeDtypeStruct((batch * seq, BRANCH_W), BF16),
        compiler_params=_cparams(("parallel", "parallel", "arbitrary")),
        name="fox_attn",
    )(qi, kj, z, z, z, cum)


def _swa_kernel(sink_ref, q_ref, kc_ref, kp_ref, vc_ref, vp_ref, bias_ref, o_ref):
    q = q_ref[...] * jnp.asarray(D_HD ** -0.5, BF16)
    kk = jnp.concatenate([kp_ref[...], kc_ref[...]], axis=0)
    vv = jnp.concatenate([vp_ref[...], vc_ref[...]], axis=0)
    outs = []
    for kv in range(D_KV):
        k_h = kk[:, kv * D_HD:(kv + 1) * D_HD]
        v_h = vv[:, kv * D_HD:(kv + 1) * D_HD]
        for g in range(D_GROUP):
            hh = kv * D_GROUP + g
            q_h = q[:, hh * D_HD:(hh + 1) * D_HD]
            s = lax.dot_general(q_h, k_h, (((1,), (1,)), ((), ())), preferred_element_type=F32)
            lg = s + bias_ref[0, hh]
            sink = sink_ref[hh]
            m = jnp.maximum(jnp.max(lg, axis=-1, keepdims=True), sink)
            e = jnp.exp(lg - m)
            p = e / (jnp.sum(e, axis=-1, keepdims=True) + jnp.exp(sink - m))
            outs.append(jnp.dot(p.astype(BF16), v_h, preferred_element_type=F32))
    o_ref[...] = jnp.concatenate(outs, axis=1).astype(BF16)


def _swa_bias():
    slopes = np.exp2(-8.0 * np.arange(1, D_HEADS + 1, dtype=np.float32) / D_HEADS).astype(np.float32)
    qi = np.arange(SWA_BLOCK)
    kj = np.arange(2 * SWA_BLOCK) - SWA_BLOCK
    dist = (qi[:, None] - kj[None, :]).astype(np.float32)
    valid = (dist >= 0) & (dist < WINDOW)
    general = np.where(valid[None], -(slopes[:, None, None] * dist[None]), np.float32(NEG_INF))
    first = np.where((kj >= 0)[None, None, :], general, np.float32(NEG_INF))
    return jnp.asarray(np.stack([first, general]).astype(np.float32))


def _swa(z, sinks, batch, seq):
    nb = seq // SWA_BLOCK
    qc, kc, vc = DQ0 // BRANCH_W, DK0 // 128, DV0 // 128
    cur = lambda b, n: b * nb + n
    prev = lambda b, n: b * nb + jnp.maximum(n - 1, 0)
    return pl.pallas_call(
        _swa_kernel,
        grid=(batch, nb),
        in_specs=[
            pl.BlockSpec(memory_space=pltpu.SMEM),
            pl.BlockSpec((SWA_BLOCK, BRANCH_W), lambda b, n: (cur(b, n), qc)),
            pl.BlockSpec((SWA_BLOCK, 128), lambda b, n: (cur(b, n), kc)),
            pl.BlockSpec((SWA_BLOCK, 128), lambda b, n: (prev(b, n), kc)),
            pl.BlockSpec((SWA_BLOCK, 128), lambda b, n: (cur(b, n), vc)),
            pl.BlockSpec((SWA_BLOCK, 128), lambda b, n: (prev(b, n), vc)),
            pl.BlockSpec((1, D_HEADS, SWA_BLOCK, 2 * SWA_BLOCK), lambda b, n: (jnp.minimum(n, 1), 0, 0, 0)),
        ],
        out_specs=pl.BlockSpec((SWA_BLOCK, BRANCH_W), lambda b, n: (cur(b, n), 0)),
        out_shape=jax.ShapeDtypeStruct((batch * seq, BRANCH_W), BF16),
        compiler_params=_cparams(("parallel", "arbitrary")),
        name="swa",
    )(sinks, z, z, z, z, z, _swa_bias())


def _lru_kernel(x_ref, g_ref, cw_ref, cb_ref, wa_ref, ba_ref, wx_ref, bx_ref, lam_ref, o_ref,
                xpad_sc, apad_sc, upad_sc, hc_sc):
    ts, pad = LRU_TS, LRU_PAD

    @pl.when(pl.program_id(1) == 0)
    def _():
        xpad_sc[0:8, :] = jnp.zeros((8, BRANCH_W), F32)
        hc_sc[...] = jnp.zeros(hc_sc.shape, F32)
        apad_sc[0:pad, :] = jnp.ones((pad, BRANCH_W), F32)
        upad_sc[0:pad, :] = jnp.zeros((pad, BRANCH_W), F32)

    x = x_ref[...].astype(F32)
    xpad_sc[8:8 + ts, :] = x
    cw = cw_ref[...]
    xc = (cw[3:4] * x + cw[2:3] * xpad_sc[7:7 + ts, :] + cw[1:2] * xpad_sc[6:6 + ts, :]
          + cw[0:1] * xpad_sc[5:5 + ts, :] + cb_ref[...])
    xpad_sc[0:8, :] = x[ts - 8:ts]

    xcb = xc.astype(BF16)
    r = jax.nn.sigmoid(jnp.dot(xcb, wa_ref[...], preferred_element_type=F32) + ba_ref[...])
    gi = jax.nn.sigmoid(jnp.dot(xcb, wx_ref[...], preferred_element_type=F32) + bx_ref[...])
    log_a = (-B_C * r) * _softplus(-lam_ref[...])
    a = jnp.exp(log_a)
    u = jnp.sqrt(1.0 - jnp.exp(2.0 * log_a)) * (gi * xc)

    sh = 1
    while sh < ts:
        apad_sc[pad:pad + ts, :] = a
        upad_sc[pad:pad + ts, :] = u
        u = a * upad_sc[pad - sh:pad - sh + ts, :] + u
        a = a * apad_sc[pad - sh:pad - sh + ts, :]
        sh *= 2
    h = a * hc_sc[...] + u
    hc_sc[...] = h[ts - 1:ts]
    o_ref[...] = (jax.nn.gelu(g_ref[...].astype(F32)) * h).astype(BF16)


def _lru(z, cw, cb, wa, ba, wx, bx, lam, batch, seq):
    ns = seq // LRU_TS
    xc_, gc_ = BX0 // BRANCH_W, BG0 // BRANCH_W
    const = lambda shape: pl.BlockSpec(shape, lambda b, s: (0, 0))
    return pl.pallas_call(
        _lru_kernel,
        grid=(batch, ns),
        in_specs=[
            pl.BlockSpec((LRU_TS, BRANCH_W), lambda b, s: (b * ns + s, xc_)),
            pl.BlockSpec((LRU_TS, BRANCH_W), lambda b, s: (b * ns + s, gc_)),
            const((B_CONV, BRANCH_W)), const((1, BRANCH_W)),
            const((BRANCH_W, BRANCH_W)), const((1, BRANCH_W)),
            const((BRANCH_W, BRANCH_W)), const((1, BRANCH_W)),
            const((1, BRANCH_W)),
        ],
        out_specs=pl.BlockSpec((LRU_TS, BRANCH_W), lambda b, s: (b * ns + s, 0)),
        out_shape=jax.ShapeDtypeStruct((batch * seq, BRANCH_W), BF16),
        scratch_shapes=[
            pltpu.VMEM((LRU_TS + 8, BRANCH_W), F32),
            pltpu.VMEM((LRU_TS + LRU_PAD, BRANCH_W), F32),
            pltpu.VMEM((LRU_TS + LRU_PAD, BRANCH_W), F32),
            pltpu.VMEM((1, BRANCH_W), F32),
        ],
        compiler_params=_cparams(("parallel", "arbitrary")),
        name="lru",
    )(z, z, cw, cb, wa, ba, wx, bx, lam)


def _merge_kernel(h_ref, ya_ref, yb_ref, yc_ref, yd_ref, g0_ref, g1_ref, g2_ref, g3_ref, bg_ref, wb_ref, wo_ref,
                  o_ref):
    ys = (ya_ref, yb_ref, yc_ref, yd_ref)
    gs = (g0_ref, g1_ref, g2_ref, g3_ref)
    mixed = None
    for n in range(N_BRANCH):
        proj = jnp.dot(ys[n][...], wb_ref[n], preferred_element_type=F32)
        term = jax.nn.sigmoid(gs[n][...].astype(F32) + bg_ref[n:n + 1, :]) * proj
        mixed = term if mixed is None else mixed + term
    o_ref[...] = h_ref[...] + jnp.dot(mixed.astype(BF16), wo_ref[...], preferred_element_type=F32)


def _merge(h, ya, yb, yc, yd, z, bg, wb, wo):
    t = h.shape[0]
    row = lambda shape: pl.BlockSpec(shape, lambda i: (i, 0))
    gate = lambda n: pl.BlockSpec((MERGE_TM, D_MODEL), lambda i: (i, GZ0 // D_MODEL + n))
    return pl.pallas_call(
        _merge_kernel,
        grid=(t // MERGE_TM,),
        in_specs=[
            row((MERGE_TM, D_MODEL)),
            row((MERGE_TM, BRANCH_W)), row((MERGE_TM, BRANCH_W)), row((MERGE_TM, BRANCH_W)),
            row((MERGE_TM, BRANCH_W)),
            gate(0), gate(1), gate(2), gate(3),
            pl.BlockSpec((N_BRANCH, D_MODEL), lambda i: (0, 0)),
            pl.BlockSpec((N_BRANCH, BRANCH_W, D_MODEL), lambda i: (0, 0, 0), pipeline_mode=pl.Buffered(1)),
            pl.BlockSpec((D_MODEL, D_MODEL), lambda i: (0, 0), pipeline_mode=pl.Buffered(1)),
        ],
        out_specs=row((MERGE_TM, D_MODEL)),
        out_shape=jax.ShapeDtypeStruct((t, D_MODEL), F32),
        compiler_params=_cparams(("parallel",)),
        name="merge",
    )(h, ya, yb, yc, yd, z, z, z, z, bg, wb, wo)


def _ffn_kernel(h_ref, g_ref, wg_ref, wu_ref, wd_ref, gf_ref, o_ref, v_sc, *, final_norm):
    j = pl.program_id(1)

    @pl.when(j == 0)
    def _():
        x = h_ref[...]
        v_sc[...] = _rms(x, g_ref[...]).astype(BF16)
        o_ref[...] = x

    v = v_sc[...]
    a = jnp.dot(v, wg_ref[...], preferred_element_type=F32)
    b = jnp.dot(v, wu_ref[...], preferred_element_type=F32)
    t = (a * jax.nn.sigmoid(a)) * b
    o_ref[...] += jnp.dot(t.astype(BF16), wd_ref[...], preferred_element_type=F32)

    if final_norm:
        @pl.when(j == pl.num_programs(1) - 1)
        def _():
            o_ref[...] = _rms(o_ref[...], gf_ref[...])


def _ffn(h, g, wg, wu, wd, gf, final_norm):
    t = h.shape[0]
    return pl.pallas_call(
        functools.partial(_ffn_kernel, final_norm=final_norm),
        grid=(t // FFN_TM, D_FF // FFN_TF),
        in_specs=[
            pl.BlockSpec((FFN_TM, D_MODEL), lambda i, j: (i, 0)),
            pl.BlockSpec((1, D_MODEL), lambda i, j: (0, 0)),
            pl.BlockSpec((D_MODEL, FFN_TF), lambda i, j: (0, j)),
            pl.BlockSpec((D_MODEL, FFN_TF), lambda i, j: (0, j)),
            pl.BlockSpec((FFN_TF, D_MODEL), lambda i, j: (j, 0)),
            pl.BlockSpec((1, D_MODEL), lambda i, j: (0, 0)),
        ],
        out_specs=pl.BlockSpec((FFN_TM, D_MODEL), lambda i, j: (i, 0)),
        out_shape=jax.ShapeDtypeStruct((t, D_MODEL), F32),
        scratch_shapes=[pltpu.VMEM((FFN_TM, D_MODEL), BF16)],
        compiler_params=_cparams(("parallel", "arbitrary")),
        name="ffn_final" if final_norm else "ffn",
    )(h, g, wg, wu, wd, gf)


def _block_diag(w):
    depth, nb, c, d = w.shape
    eye = jnp.eye(nb, dtype=w.dtype)
    return jnp.einsum('lncd,nm->lncmd', w, eye).reshape(depth, nb * c, nb * d)


def kernel(x, norm_mix, w_in, b_gate, diff_lq1, diff_lk1, diff_lq2, diff_lk2, diff_subln, lru_conv_w, lru_conv_b,
           lru_wa, lru_ba, lru_wx, lru_bx, lru_lambda, fox_b_f, swa_sinks, w_branch, w_out, norm_ffn, w_ffn_gate,
           w_ffn_up, w_ffn_down, norm_final):
    batch, seq, d = x.shape
    depth = w_in.shape[0]
    assert d == D_MODEL and seq % max(C_TQ, LRU_TS) == 0 and (batch * seq) % IN_TM == 0
    t = batch * seq

    w_bf = w_in.astype(BF16)
    w_main = jnp.concatenate([w_bf[:, :, _R_GZ0:], w_bf[:, :, :_R_CF0], w_bf[:, :, _R_DQ0:_R_GZ0]], axis=2)
    w_cf = jnp.pad(w_bf[:, :, _R_CF0:_R_DQ0], ((0, 0), (0, 0), (0, CF_PAD - C_HEADS)))
    wb = w_branch.astype(BF16)
    wo = w_out.astype(BF16)
    wg = w_ffn_gate.astype(BF16)
    wu = w_ffn_up.astype(BF16)
    wd = w_ffn_down.astype(BF16)
    wa = _block_diag(lru_wa).astype(BF16)
    wx = _block_diag(lru_wx).astype(BF16)
    lvec = jnp.stack([diff_lq1, diff_lk1, diff_lq2, diff_lk2], axis=1).astype(F32)
    bg = b_gate.reshape(depth, N_BRANCH, D_MODEL)
    gf = norm_final.reshape(1, D_MODEL)

    h = x.reshape(t, D_MODEL)
    for l in range(depth):
        lam_init = 0.8 - 0.6 * math.exp(-0.3 * l)
        z, cf = _in_proj(h, norm_mix[l].reshape(1, D_MODEL), w_main[l], w_cf[l])

        ya = _diff_attn(z, lvec[l], diff_subln[l].reshape(1, 2 * A_HD),
                        jnp.asarray([lam_init, 1.0 - lam_init], F32), batch, seq)
        yb = _lru(z, lru_conv_w[l], lru_conv_b[l].reshape(1, -1), wa[l], lru_ba[l].reshape(1, -1), wx[l],
                  lru_bx[l].reshape(1, -1), lru_lambda[l].reshape(1, -1), batch, seq)
        cf_t = cf[:, :C_HEADS].reshape(batch, seq, C_HEADS).transpose(0, 2, 1).reshape(batch * C_HEADS, seq)
        bf_t = jnp.tile(fox_b_f[l], batch).reshape(batch * C_HEADS, 1)
        cum = _fox_cum(cf_t, bf_t).reshape(batch * C_HEADS, 1, seq)
        yc = _fox_attn(z, cum, batch, seq)
        yd = _swa(z, swa_sinks[l], batch, seq)

        h = _merge(h, ya, yb, yc, yd, z, bg[l], wb[l], wo[l])
        h = _ffn(h, norm_ffn[l].reshape(1, D_MODEL), wg[l], wu[l], wd[l], gf, final_norm=(l == depth - 1))
    return h.reshape(batch, seq, D_MODEL)
```

```python
import functools
import math

import jax
import jax.numpy as jnp
import numpy as np
from jax import lax
from jax.experimental import pallas as pl
from jax.experimental.pallas import tpu as pltpu

F32 = jnp.float32
BF16 = jnp.bfloat16

D_MODEL = 2048
N_BRANCH = 4
BRANCH_W = D_MODEL // 4
A_HEADS = 4
A_HD = BRANCH_W // (2 * A_HEADS)
B_BLOCKS = 8
B_CONV = 4
B_C = 8.0
C_HEADS = 4
C_HD = BRANCH_W // C_HEADS
D_HEADS = 8
D_KV = 2
D_GROUP = D_HEADS // D_KV
D_HD = BRANCH_W // D_HEADS
WINDOW = 128
SWA_BLOCK = 128
SWA_QB = 4
D_FF = -(-8 * D_MODEL // (3 * 256)) * 256
RMS_EPS = 1e-6
NEG_INF = -1e30

GZ0 = 0
AQ0 = GZ0 + N_BRANCH * D_MODEL
AK0 = AQ0 + BRANCH_W
AV0 = AK0 + BRANCH_W
BX0 = AV0 + BRANCH_W
BG0 = BX0 + BRANCH_W
CQ0 = BG0 + BRANCH_W
CK0 = CQ0 + BRANCH_W
CV0 = CK0 + BRANCH_W
DQ0 = CV0 + BRANCH_W
DK0 = DQ0 + BRANCH_W
DV0 = DK0 + D_KV * D_HD
NZ = DV0 + D_KV * D_HD
CF_PAD = 128

_R_CF0 = 8 * BRANCH_W
_R_DQ0 = _R_CF0 + C_HEADS
_R_GZ0 = _R_DQ0 + BRANCH_W + 2 * D_KV * D_HD

V7X_VMEM_LIMIT = 56 * 1024 * 1024

IN_TM, IN_TN = 1024, 768
A_TQ = 512
FLASH_ROWS = 256
ATTN_HPS = 2
C_TQ = 1024
LRU_TS = 512
LRU_PAD = LRU_TS // 2
MERGE_TM = 256
FFN_TM, FFN_TF = 512, 512


def _cparams(sem):
    return pltpu.CompilerParams(dimension_semantics=sem, vmem_limit_bytes=V7X_VMEM_LIMIT)


def _rms(x, g):
    return x * lax.rsqrt(jnp.mean(x * x, axis=-1, keepdims=True) + RMS_EPS) * g


def _softplus(y):
    return jnp.maximum(y, 0.0) + jnp.log1p(jnp.exp(-jnp.abs(y)))


def _in_proj_kernel(h_ref, g_ref, w_ref, wcf_ref, z_ref, cf_ref, u_sc):
    @pl.when(pl.program_id(1) == 0)
    def _():
        u = _rms(h_ref[...], g_ref[...]).astype(BF16)
        u_sc[...] = u
        cf_ref[...] = jnp.dot(u, wcf_ref[...], preferred_element_type=F32)

    z_ref[...] = jnp.dot(u_sc[...], w_ref[...], preferred_element_type=F32).astype(BF16)


def _in_proj(h, g, w, wcf, l):
    t = h.shape[0]
    return pl.pallas_call(
        _in_proj_kernel,
        grid=(t // IN_TM, NZ // IN_TN),
        in_specs=[
            pl.BlockSpec((IN_TM, D_MODEL), lambda i, j: (i, 0)),
            pl.BlockSpec((1, D_MODEL), lambda i, j: (0, 0)),
            pl.BlockSpec((None, D_MODEL, IN_TN), lambda i, j: (l, 0, j)),
            pl.BlockSpec((None, D_MODEL, CF_PAD), lambda i, j: (l, 0, 0)),
        ],
        out_specs=[
            pl.BlockSpec((IN_TM, IN_TN), lambda i, j: (i, j)),
            pl.BlockSpec((IN_TM, CF_PAD), lambda i, j: (i, 0)),
        ],
        out_shape=[jax.ShapeDtypeStruct((t, NZ), BF16), jax.ShapeDtypeStruct((t, CF_PAD), F32)],
        scratch_shapes=[pltpu.VMEM((IN_TM, D_MODEL), BF16)],
        compiler_params=_cparams(("parallel", "arbitrary")),
        name="in_proj",
    )(h, g, w, wcf)


def _tri_tables(n):
    qi = [i for i in range(n) for _ in range(i + 1)]
    kj = [j for i in range(n) for j in range(i + 1)]
    return jnp.asarray(qi, jnp.int32), jnp.asarray(kj, jnp.int32)


def _flash_rows(q_ref, k_ref, v2_sc, bias, m_sc, acc_sc, r0, diag_q0, scale):
    rows = FLASH_ROWS
    nk = k_ref.shape[0] if diag_q0 is None else diag_q0 + rows
    s = lax.dot_general(q_ref[r0:r0 + rows, :], k_ref[0:nk, :], (((1,), (1,)), ((), ())),
                        preferred_element_type=F32)
    if scale is not None:
        s = s * scale
    s = s + bias[:, 0:nk]
    if diag_q0 is not None:
        qpos = lax.broadcasted_iota(jnp.int32, s.shape, 0) + diag_q0
        kpos = lax.broadcasted_iota(jnp.int32, s.shape, 1)
        s = jnp.where(kpos <= qpos, s, NEG_INF)
    m_prev = m_sc[r0:r0 + rows, :]
    m_new = jnp.maximum(m_prev, jnp.max(s, axis=-1, keepdims=True))
    alpha = jnp.exp(m_prev - m_new)
    p = jnp.exp(s - jnp.concatenate([m_new] * (nk // 128), axis=1))
    pv = jnp.dot(p.astype(BF16), v2_sc[0:nk, :], preferred_element_type=F32)
    acc_sc[r0:r0 + rows, :] = jnp.concatenate([alpha, alpha], axis=1) * acc_sc[r0:r0 + rows, :] + pv
    m_sc[r0:r0 + rows, :] = m_new


def _flash_block(q_ref, k_ref, v2_sc, bias, m_sc, acc_sc, n_rows, tq, diag, scale):
    for r0 in range(0, n_rows, FLASH_ROWS):
        _flash_rows(q_ref, k_ref, v2_sc, bias, m_sc, acc_sc, r0, (r0 % tq) if diag else None, scale)


def _flash_init(v2_sc, m_sc, acc_sc):
    v2_sc[:, 128:256] = jnp.ones((v2_sc.shape[0], 128), BF16)
    m_sc[...] = jnp.full(m_sc.shape, -jnp.inf, F32)
    acc_sc[...] = jnp.zeros(acc_sc.shape, F32)


def _diff_attn_kernel(qi_ref, kj_ref, slope_ref, lami_ref, q_ref, k_ref, v_ref, lv_ref, sg_ref, o_ref,
                      q2_sc, v2_sc, m_sc, acc_sc):
    tq = A_TQ
    t = pl.program_id(2)
    i = qi_ref[t]
    j = kj_ref[t]
    heads = [(hh, slice(hh * 128, (hh + 1) * 128)) for hh in range(ATTN_HPS)]

    @pl.when(j == 0)
    def _():
        for hh, cols in heads:
            q = q_ref[:, cols]
            lane = lax.broadcasted_iota(jnp.int32, q.shape, 1)
            qs = q * jnp.asarray(A_HD ** -0.5, BF16)
            zero = jnp.zeros_like(qs)
            q2_sc[hh, 0:tq, :] = jnp.where(lane < A_HD, qs, zero)
            q2_sc[hh, tq:2 * tq, :] = jnp.where(lane >= A_HD, qs, zero)
            _flash_init(v2_sc.at[hh], m_sc.at[hh], acc_sc.at[hh])

    kpos = (lax.broadcasted_iota(jnp.int32, (1, tq), 1) + (j - i) * tq).astype(F32)

    def block(diag):
        for hh, cols in heads:
            v2_sc[hh, :, 0:128] = v_ref[:, cols]
            bias = kpos * slope_ref[pl.program_id(1) * ATTN_HPS + hh]
            _flash_block(q2_sc.at[hh], k_ref.at[:, cols], v2_sc.at[hh], bias, m_sc.at[hh], acc_sc.at[hh],
                         2 * tq, tq, diag, None)

    @pl.when(j < i)
    def _():
        block(False)

    @pl.when(j == i)
    def _():
        block(True)
        lv = lv_ref[...]
        lam = (jnp.exp(jnp.sum(lv[0:1] * lv[1:2], axis=-1, keepdims=True))
               - jnp.exp(jnp.sum(lv[2:3] * lv[3:4], axis=-1, keepdims=True)) + lami_ref[0])
        for hh, cols in heads:
            acc = acc_sc[hh]
            o = acc[0:tq, 0:128] / acc[0:tq, 128:256] - lam * (acc[tq:2 * tq, 0:128] / acc[tq:2 * tq, 128:256])
            o_ref[:, cols] = (_rms(o, sg_ref[...]) * lami_ref[1]).astype(BF16)


def _diff_attn(z, lvec, subln, lam_init, batch, seq):
    nq = seq // A_TQ
    qi, kj = _tri_tables(nq)
    slopes = jnp.asarray(np.exp2(-8.0 * np.arange(1, A_HEADS + 1, dtype=np.float32) / A_HEADS), F32)
    w = 128 * ATTN_HPS
    qc, kc, vc = AQ0 // w, AK0 // w, AV0 // w
    smem = pl.BlockSpec(memory_space=pltpu.SMEM)
    grid_spec = pltpu.PrefetchScalarGridSpec(
        num_scalar_prefetch=2,
        grid=(batch, A_HEADS // ATTN_HPS, int(qi.shape[0])),
        in_specs=[
            smem, smem,
            pl.BlockSpec((A_TQ, w), lambda b, h, t, qi, kj: (b * nq + qi[t], qc + h)),
            pl.BlockSpec((A_TQ, w), lambda b, h, t, qi, kj: (b * nq + kj[t], kc + h)),
            pl.BlockSpec((A_TQ, w), lambda b, h, t, qi, kj: (b * nq + kj[t], vc + h)),
            pl.BlockSpec((4, A_HD), lambda b, h, t, qi, kj: (0, 0)),
            pl.BlockSpec((1, 2 * A_HD), lambda b, h, t, qi, kj: (0, 0)),
        ],
        out_specs=pl.BlockSpec((A_TQ, w), lambda b, h, t, qi, kj: (b * nq + qi[t], h)),
        scratch_shapes=[
            pltpu.VMEM((ATTN_HPS, 2 * A_TQ, 128), BF16),
            pltpu.VMEM((ATTN_HPS, A_TQ, 256), BF16),
            pltpu.VMEM((ATTN_HPS, 2 * A_TQ, 128), F32),
            pltpu.VMEM((ATTN_HPS, 2 * A_TQ, 256), F32),
        ],
    )
    return pl.pallas_call(
        _diff_attn_kernel,
        grid_spec=grid_spec,
        out_shape=jax.ShapeDtypeStruct((batch * seq, BRANCH_W), BF16),
        compiler_params=_cparams(("parallel", "parallel", "arbitrary")),
        name="diff_attn",
    )(qi, kj, slopes, lam_init, z, z, z, lvec, subln)


def _fox_cum_kernel(cf_ref, bf_ref, o_ref):
    x = cf_ref[...] + bf_ref[...]
    logf = jnp.minimum(x, 0.0) - jnp.log1p(jnp.exp(-jnp.abs(x)))
    rows, seq = logf.shape
    lane = lax.broadcasted_iota(jnp.int32, (rows, 128), 1)
    carry = jnp.zeros((rows, 1), F32)
    for c in range(seq // 128):
        y = logf[:, c * 128:(c + 1) * 128]
        sh = 1
        while sh < 128:
            y = y + jnp.where(lane >= sh, pltpu.roll(y, sh, axis=1), 0.0)
            sh *= 2
        y = y + carry
        o_ref[:, c * 128:(c + 1) * 128] = y
        carry = y[:, 127:128]


def _fox_cum(cf_t, bf_t):
    rows, seq = cf_t.shape
    return pl.pallas_call(
        _fox_cum_kernel,
        grid=(1,),
        in_specs=[pl.BlockSpec((rows, seq), lambda i: (0, 0)), pl.BlockSpec((rows, 1), lambda i: (0, 0))],
        out_specs=pl.BlockSpec((rows, seq), lambda i: (0, 0)),
        out_shape=jax.ShapeDtypeStruct((rows, seq), F32),
        compiler_params=_cparams(("arbitrary",)),
        name="fox_cum",
    )(cf_t, bf_t)


def _fox_attn_kernel(qi_ref, kj_ref, q_ref, k_ref, v_ref, cum_ref, o_ref, v2_sc, m_sc, acc_sc):
    tq = C_TQ
    t = pl.program_id(2)
    i = qi_ref[t]
    j = kj_ref[t]
    heads = [(hh, slice(hh * 128, (hh + 1) * 128)) for hh in range(ATTN_HPS)]

    @pl.when(j == 0)
    def _():
        for hh, _ in heads:
            _flash_init(v2_sc.at[hh], m_sc.at[hh], acc_sc.at[hh])

    def block(diag):
        for hh, cols in heads:
            v2_sc[hh, :, 0:128] = v_ref[:, cols]
            _flash_block(q_ref.at[:, cols], k_ref.at[:, cols], v2_sc.at[hh], -cum_ref[hh], m_sc.at[hh],
                         acc_sc.at[hh], tq, tq, diag, C_HD ** -0.5)

    @pl.when(j < i)
    def _():
        block(False)

    @pl.when(j == i)
    def _():
        block(True)
        for hh, cols in heads:
            acc = acc_sc[hh]
            o_ref[:, cols] = (acc[:, 0:128] / acc[:, 128:256]).astype(BF16)


def _fox_attn(z, cum, batch, seq):
    nq = seq // C_TQ
    qi, kj = _tri_tables(nq)
    w = 128 * ATTN_HPS
    qc, kc, vc = CQ0 // w, CK0 // w, CV0 // w
    hg = C_HEADS // ATTN_HPS
    grid_spec = pltpu.PrefetchScalarGridSpec(
        num_scalar_prefetch=2,
        grid=(batch, hg, int(qi.shape[0])),
        in_specs=[
            pl.BlockSpec((C_TQ, w), lambda b, h, t, qi, kj: (b * nq + qi[t], qc + h)),
            pl.BlockSpec((C_TQ, w), lambda b, h, t, qi, kj: (b * nq + kj[t], kc + h)),
            pl.BlockSpec((C_TQ, w), lambda b, h, t, qi, kj: (b * nq + kj[t], vc + h)),
            pl.BlockSpec((ATTN_HPS, 1, C_TQ), lambda b, h, t, qi, kj: (b * hg + h, 0, kj[t])),
        ],
        out_specs=pl.BlockSpec((C_TQ, w), lambda b, h, t, qi, kj: (b * nq + qi[t], h)),
        scratch_shapes=[
            pltpu.VMEM((ATTN_HPS, C_TQ, 256), BF16),
            pltpu.VMEM((ATTN_HPS, C_TQ, 128), F32),
            pltpu.VMEM((ATTN_HPS, C_TQ, 256), F32),
        ],
    )
    return pl.pallas_call(
        _fox_attn_kernel,
        grid_spec=grid_spec,
        out_shape=jax.ShapeDtypeStruct((batch * seq, BRANCH_W), BF16),
        compiler_params=_cparams(("parallel", "parallel", "arbitrary")),
        name="fox_attn",
    )(qi, kj, z, z, z, cum)


def _swa_kernel(sink_ref, q_ref, kc_ref, kp_ref, vc_ref, vp_ref, bias_ref, o_ref):
    blk = SWA_BLOCK
    first = pl.program_id(1) == 0
    q_all = q_ref[...] * jnp.asarray(D_HD ** -0.5, BF16)
    kk_all = jnp.concatenate([kp_ref[...], kc_ref[...]], axis=0)
    vv_all = jnp.concatenate([vp_ref[...], vc_ref[...]], axis=0)
    for qb in range(SWA_QB):
        q = q_all[qb * blk:(qb + 1) * blk]
        kk = kk_all[qb * blk:(qb + 2) * blk]
        vv = vv_all[qb * blk:(qb + 2) * blk]
        outs = []
        for kv in range(D_KV):
            k_h = kk[:, kv * D_HD:(kv + 1) * D_HD]
            v_h = vv[:, kv * D_HD:(kv + 1) * D_HD]
            for g in range(D_GROUP):
                hh = kv * D_GROUP + g
                q_h = q[:, hh * D_HD:(hh + 1) * D_HD]
                s = lax.dot_general(q_h, k_h, (((1,), (1,)), ((), ())), preferred_element_type=F32)
                bias = jnp.where(first, bias_ref[0, hh], bias_ref[1, hh]) if qb == 0 else bias_ref[1, hh]
                lg = s + bias
                sink = sink_ref[hh]
                m = jnp.maximum(jnp.max(lg, axis=-1, keepdims=True), sink)
                e = jnp.exp(lg - m)
                p = e / (jnp.sum(e, axis=-1, keepdims=True) + jnp.exp(sink - m))
                outs.append(jnp.dot(p.astype(BF16), v_h, preferred_element_type=F32))
        o_ref[qb * blk:(qb + 1) * blk, :] = jnp.concatenate(outs, axis=1).astype(BF16)


def _swa_bias():
    slopes = np.exp2(-8.0 * np.arange(1, D_HEADS + 1, dtype=np.float32) / D_HEADS).astype(np.float32)
    qi = np.arange(SWA_BLOCK)
    kj = np.arange(2 * SWA_BLOCK) - SWA_BLOCK
    dist = (qi[:, None] - kj[None, :]).astype(np.float32)
    valid = (dist >= 0) & (dist < WINDOW)
    general = np.where(valid[None], -(slopes[:, None, None] * dist[None]), np.float32(NEG_INF))
    first = np.where((kj >= 0)[None, None, :], general, np.float32(NEG_INF))
    return jnp.asarray(np.stack([first, general]).astype(np.float32))


def _swa(z, sinks, batch, seq):
    rows = SWA_QB * SWA_BLOCK
    ns = seq // rows
    qc, kc, vc = DQ0 // BRANCH_W, DK0 // 128, DV0 // 128
    cur = lambda b, n: b * ns + n
    prev = lambda b, n: jnp.maximum((b * ns + n) * SWA_QB - 1, 0)
    return pl.pallas_call(
        _swa_kernel,
        grid=(batch, ns),
        in_specs=[
            pl.BlockSpec(memory_space=pltpu.SMEM),
            pl.BlockSpec((rows, BRANCH_W), lambda b, n: (cur(b, n), qc)),
            pl.BlockSpec((rows, 128), lambda b, n: (cur(b, n), kc)),
            pl.BlockSpec((SWA_BLOCK, 128), lambda b, n: (prev(b, n), kc)),
            pl.BlockSpec((rows, 128), lambda b, n: (cur(b, n), vc)),
            pl.BlockSpec((SWA_BLOCK, 128), lambda b, n: (prev(b, n), vc)),
            pl.BlockSpec((2, D_HEADS, SWA_BLOCK, 2 * SWA_BLOCK), lambda b, n: (0, 0, 0, 0)),
        ],
        out_specs=pl.BlockSpec((rows, BRANCH_W), lambda b, n: (cur(b, n), 0)),
        out_shape=jax.ShapeDtypeStruct((batch * seq, BRANCH_W), BF16),
        compiler_params=_cparams(("parallel", "arbitrary")),
        name="swa",
    )(sinks, z, z, z, z, z, _swa_bias())


def _lru_kernel(x_ref, g_ref, cw_ref, cb_ref, wa_ref, ba_ref, wx_ref, bx_ref, lam_ref, o_ref,
                xpad_sc, apad_sc, upad_sc, hc_sc):
    ts, pad = LRU_TS, LRU_PAD

    @pl.when(pl.program_id(1) == 0)
    def _():
        xpad_sc[0:8, :] = jnp.zeros((8, BRANCH_W), F32)
        hc_sc[...] = jnp.zeros(hc_sc.shape, F32)
        apad_sc[0:pad, :] = jnp.ones((pad, BRANCH_W), F32)
        upad_sc[0:pad, :] = jnp.zeros((pad, BRANCH_W), F32)

    x = x_ref[...].astype(F32)
    xpad_sc[8:8 + ts, :] = x
    cw = cw_ref[...]
    xc = (cw[3:4] * x + cw[2:3] * xpad_sc[7:7 + ts, :] + cw[1:2] * xpad_sc[6:6 + ts, :]
          + cw[0:1] * xpad_sc[5:5 + ts, :] + cb_ref[...])
    xpad_sc[0:8, :] = x[ts - 8:ts]

    xcb = xc.astype(BF16)
    r = jax.nn.sigmoid(jnp.dot(xcb, wa_ref[...], preferred_element_type=F32) + ba_ref[...])
    gi = jax.nn.sigmoid(jnp.dot(xcb, wx_ref[...], preferred_element_type=F32) + bx_ref[...])
    log_a = (-B_C * r) * _softplus(-lam_ref[...])
    a = jnp.exp(log_a)
    u = jnp.sqrt(1.0 - jnp.exp(2.0 * log_a)) * (gi * xc)

    sh = 1
    while sh < ts:
        apad_sc[pad:pad + ts, :] = a
        upad_sc[pad:pad + ts, :] = u
        u = a * upad_sc[pad - sh:pad - sh + ts, :] + u
        a = a * apad_sc[pad - sh:pad - sh + ts, :]
        sh *= 2
    h = a * hc_sc[...] + u
    hc_sc[...] = h[ts - 1:ts]
    o_ref[...] = (jax.nn.gelu(g_ref[...].astype(F32)) * h).astype(BF16)


def _lru(z, cw, cb, wa, ba, wx, bx, lam, l, batch, seq):
    ns = seq // LRU_TS
    xc_, gc_ = BX0 // BRANCH_W, BG0 // BRANCH_W
    const = lambda shape: pl.BlockSpec(shape, lambda b, s: (0, 0))
    return pl.pallas_call(
        _lru_kernel,
        grid=(batch, ns),
        in_specs=[
            pl.BlockSpec((LRU_TS, BRANCH_W), lambda b, s: (b * ns + s, xc_)),
            pl.BlockSpec((LRU_TS, BRANCH_W), lambda b, s: (b * ns + s, gc_)),
            const((B_CONV, BRANCH_W)), const((1, BRANCH_W)),
            pl.BlockSpec((None, BRANCH_W, BRANCH_W), lambda b, s: (l, 0, 0)), const((1, BRANCH_W)),
            pl.BlockSpec((None, BRANCH_W, BRANCH_W), lambda b, s: (l, 0, 0)), const((1, BRANCH_W)),
            const((1, BRANCH_W)),
        ],
        out_specs=pl.BlockSpec((LRU_TS, BRANCH_W), lambda b, s: (b * ns + s, 0)),
        out_shape=jax.ShapeDtypeStruct((batch * seq, BRANCH_W), BF16),
        scratch_shapes=[
            pltpu.VMEM((LRU_TS + 8, BRANCH_W), F32),
            pltpu.VMEM((LRU_TS + LRU_PAD, BRANCH_W), F32),
            pltpu.VMEM((LRU_TS + LRU_PAD, BRANCH_W), F32),
            pltpu.VMEM((1, BRANCH_W), F32),
        ],
        compiler_params=_cparams(("parallel", "arbitrary")),
        name="lru",
    )(z, z, cw, cb, wa, ba, wx, bx, lam)


def _merge_kernel(h_ref, ya_ref, yb_ref, yc_ref, yd_ref, g0_ref, g1_ref, g2_ref, g3_ref, bg_ref, wb_ref, wo_ref,
                  o_ref):
    ys = (ya_ref, yb_ref, yc_ref, yd_ref)
    gs = (g0_ref, g1_ref, g2_ref, g3_ref)
    mixed = None
    for n in range(N_BRANCH):
        proj = jnp.dot(ys[n][...], wb_ref[n], preferred_element_type=F32)
        term = jax.nn.sigmoid(gs[n][...].astype(F32) + bg_ref[n:n + 1, :]) * proj
        mixed = term if mixed is None else mixed + term
    o_ref[...] = h_ref[...] + jnp.dot(mixed.astype(BF16), wo_ref[...], preferred_element_type=F32)


def _merge(h, ya, yb, yc, yd, z, bg, wb, wo, l):
    t = h.shape[0]
    row = lambda shape: pl.BlockSpec(shape, lambda i: (i, 0))
    gate = lambda n: pl.BlockSpec((MERGE_TM, D_MODEL), lambda i: (i, GZ0 // D_MODEL + n))
    return pl.pallas_call(
        _merge_kernel,
        grid=(t // MERGE_TM,),
        in_specs=[
            row((MERGE_TM, D_MODEL)),
            row((MERGE_TM, BRANCH_W)), row((MERGE_TM, BRANCH_W)), row((MERGE_TM, BRANCH_W)),
            row((MERGE_TM, BRANCH_W)),
            gate(0), gate(1), gate(2), gate(3),
            pl.BlockSpec((N_BRANCH, D_MODEL), lambda i: (0, 0)),
            pl.BlockSpec((None, N_BRANCH, BRANCH_W, D_MODEL), lambda i: (l, 0, 0, 0), pipeline_mode=pl.Buffered(1)),
            pl.BlockSpec((None, D_MODEL, D_MODEL), lambda i: (l, 0, 0), pipeline_mode=pl.Buffered(1)),
        ],
        out_specs=row((MERGE_TM, D_MODEL)),
        out_shape=jax.ShapeDtypeStruct((t, D_MODEL), F32),
        compiler_params=_cparams(("parallel",)),
        name="merge",
    )(h, ya, yb, yc, yd, z, z, z, z, bg, wb, wo)


def _ffn_kernel(h_ref, g_ref, wg_ref, wu_ref, wd_ref, gf_ref, o_ref, v_sc, *, final_norm):
    j = pl.program_id(1)

    @pl.when(j == 0)
    def _():
        x = h_ref[...]
        v_sc[...] = _rms(x, g_ref[...]).astype(BF16)
        o_ref[...] = x

    v = v_sc[...]
    a = jnp.dot(v, wg_ref[...], preferred_element_type=F32)
    b = jnp.dot(v, wu_ref[...], preferred_element_type=F32)
    t = (a * jax.nn.sigmoid(a)) * b
    o_ref[...] += jnp.dot(t.astype(BF16), wd_ref[...], preferred_element_type=F32)

    if final_norm:
        @pl.when(j == pl.num_programs(1) - 1)
        def _():
            o_ref[...] = _rms(o_ref[...], gf_ref[...])


def _ffn(h, g, wg, wu, wd, gf, l, final_norm):
    t = h.shape[0]
    return pl.pallas_call(
        functools.partial(_ffn_kernel, final_norm=final_norm),
        grid=(t // FFN_TM, D_FF // FFN_TF),
        in_specs=[
            pl.BlockSpec((FFN_TM, D_MODEL), lambda i, j: (i, 0)),
            pl.BlockSpec((1, D_MODEL), lambda i, j: (0, 0)),
            pl.BlockSpec((None, D_MODEL, FFN_TF), lambda i, j: (l, 0, j)),
            pl.BlockSpec((None, D_MODEL, FFN_TF), lambda i, j: (l, 0, j)),
            pl.BlockSpec((None, FFN_TF, D_MODEL), lambda i, j: (l, j, 0)),
            pl.BlockSpec((1, D_MODEL), lambda i, j: (0, 0)),
        ],
        out_specs=pl.BlockSpec((FFN_TM, D_MODEL), lambda i, j: (i, 0)),
        out_shape=jax.ShapeDtypeStruct((t, D_MODEL), F32),
        scratch_shapes=[pltpu.VMEM((FFN_TM, D_MODEL), BF16)],
        compiler_params=_cparams(("parallel", "arbitrary")),
        name="ffn_final" if final_norm else "ffn",
    )(h, g, wg, wu, wd, gf)


def _block_diag(w):
    depth, nb, c, d = w.shape
    eye = jnp.eye(nb, dtype=w.dtype)
    return jnp.einsum('lncd,nm->lncmd', w, eye).reshape(depth, nb * c, nb * d)


def kernel(x, norm_mix, w_in, b_gate, diff_lq1, diff_lk1, diff_lq2, diff_lk2, diff_subln, lru_conv_w, lru_conv_b,
           lru_wa, lru_ba, lru_wx, lru_bx, lru_lambda, fox_b_f, swa_sinks, w_branch, w_out, norm_ffn, w_ffn_gate,
           w_ffn_up, w_ffn_down, norm_final):
    batch, seq, d = x.shape
    depth = w_in.shape[0]
    assert d == D_MODEL and seq % max(C_TQ, LRU_TS, SWA_QB * SWA_BLOCK) == 0 and (batch * seq) % IN_TM == 0
    t = batch * seq

    w_bf = w_in.astype(BF16)
    w_main = jnp.concatenate([w_bf[:, :, _R_GZ0:], w_bf[:, :, :_R_CF0], w_bf[:, :, _R_DQ0:_R_GZ0]], axis=2)
    w_cf = jnp.pad(w_bf[:, :, _R_CF0:_R_DQ0], ((0, 0), (0, 0), (0, CF_PAD - C_HEADS)))
    wb = w_branch.astype(BF16)
    wo = w_out.astype(BF16)
    wg = w_ffn_gate.astype(BF16)
    wu = w_ffn_up.astype(BF16)
    wd = w_ffn_down.astype(BF16)
    wa = _block_diag(lru_wa).astype(BF16)
    wx = _block_diag(lru_wx).astype(BF16)
    lvec = jnp.stack([diff_lq1, diff_lk1, diff_lq2, diff_lk2], axis=1).astype(F32)
    bg = b_gate.reshape(depth, N_BRANCH, D_MODEL)
    gf = norm_final.reshape(1, D_MODEL)

    h = x.reshape(t, D_MODEL)
    for l in range(depth):
        lam_init = 0.8 - 0.6 * math.exp(-0.3 * l)
        z, cf = _in_proj(h, norm_mix[l].reshape(1, D_MODEL), w_main, w_cf, l)

        ya = _diff_attn(z, lvec[l], diff_subln[l].reshape(1, 2 * A_HD),
                        jnp.asarray([lam_init, 1.0 - lam_init], F32), batch, seq)
        yb = _lru(z, lru_conv_w[l], lru_conv_b[l].reshape(1, -1), wa, lru_ba[l].reshape(1, -1), wx,
                  lru_bx[l].reshape(1, -1), lru_lambda[l].reshape(1, -1), l, batch, seq)
        cf_t = cf[:, :C_HEADS].reshape(batch, seq, C_HEADS).transpose(0, 2, 1).reshape(batch * C_HEADS, seq)
        bf_t = jnp.tile(fox_b_f[l], batch).reshape(batch * C_HEADS, 1)
        cum = _fox_cum(cf_t, bf_t).reshape(batch * C_HEADS, 1, seq)
        yc = _fox_attn(z, cum, batch, seq)
        yd = _swa(z, swa_sinks[l], batch, seq)

        h = _merge(h, ya, yb, yc, yd, z, bg[l], wb, wo, l)
        h = _ffn(h, norm_ffn[l].reshape(1, D_MODEL), wg, wu, wd, gf, l, final_norm=(l == depth - 1))
    return h.reshape(batch, seq, D_MODEL)
```

```python
import functools
import math

import jax
import jax.numpy as jnp
import numpy as np
from jax import lax
from jax.experimental import pallas as pl
from jax.experimental.pallas import tpu as pltpu

F32 = jnp.float32
BF16 = jnp.bfloat16

D_MODEL = 2048
N_BRANCH = 4
BRANCH_W = D_MODEL // 4
A_HEADS = 4
A_HD = BRANCH_W // (2 * A_HEADS)
B_BLOCKS = 8
B_CONV = 4
B_C = 8.0
C_HEADS = 4
C_HD = BRANCH_W // C_HEADS
D_HEADS = 8
D_KV = 2
D_GROUP = D_HEADS // D_KV
D_HD = BRANCH_W // D_HEADS
WINDOW = 128
SWA_BLOCK = 128
SWA_QB = 4
D_FF = -(-8 * D_MODEL // (3 * 256)) * 256
RMS_EPS = 1e-6
NEG_INF = -1e30

AQ0 = 0
AK0 = AQ0 + BRANCH_W
AV0 = AK0 + BRANCH_W
BX0 = AV0 + BRANCH_W
BG0 = BX0 + BRANCH_W
CQ0 = BG0 + BRANCH_W
CK0 = CQ0 + BRANCH_W
CV0 = CK0 + BRANCH_W
ZM_W = CV0 + BRANCH_W
DQ0 = 0
DK0 = DQ0 + BRANCH_W
DV0 = DK0 + D_KV * D_HD
ZD_W = DV0 + D_KV * D_HD
ZG_W = N_BRANCH * D_MODEL
CF_PAD = 128

_R_CF0 = ZM_W
_R_DQ0 = _R_CF0 + C_HEADS
_R_GZ0 = _R_DQ0 + ZD_W

V7X_VMEM_LIMIT = 56 * 1024 * 1024

NORM_TM = 1024
IN_TM, IN_TN = 2048, 1024
A_TQ = 512
FLASH_ROWS = 256
ATTN_HPS = 2
C_TQ = 1024
LRU_TS = 512
LRU_PAD = LRU_TS // 2
MERGE_TM = 256
FFN_TM, FFN_TF = 512, 512


def _cparams(sem):
    return pltpu.CompilerParams(dimension_semantics=sem, vmem_limit_bytes=V7X_VMEM_LIMIT)


def _rms(x, g):
    return x * lax.rsqrt(jnp.mean(x * x, axis=-1, keepdims=True) + RMS_EPS) * g


def _softplus(y):
    return jnp.maximum(y, 0.0) + jnp.log1p(jnp.exp(-jnp.abs(y)))


def _norm_kernel(x_ref, g_ref, o_ref):
    o_ref[...] = _rms(x_ref[...], g_ref[...]).astype(BF16)


def _norm(x, g):
    t = x.shape[0]
    return pl.pallas_call(
        _norm_kernel,
        grid=(t // NORM_TM,),
        in_specs=[pl.BlockSpec((NORM_TM, D_MODEL), lambda i: (i, 0)), pl.BlockSpec((1, D_MODEL), lambda i: (0, 0))],
        out_specs=pl.BlockSpec((NORM_TM, D_MODEL), lambda i: (i, 0)),
        out_shape=jax.ShapeDtypeStruct((t, D_MODEL), BF16),
        compiler_params=_cparams(("parallel",)),
        name="norm0",
    )(x, g)


def _proj_kernel(u_ref, w_ref, z_ref):
    z_ref[...] = jnp.dot(u_ref[...], w_ref[...], preferred_element_type=F32).astype(BF16)


def _proj(u, w, l, n_cols, name):
    t = u.shape[0]
    return pl.pallas_call(
        _proj_kernel,
        grid=(t // IN_TM, n_cols // IN_TN),
        in_specs=[
            pl.BlockSpec((IN_TM, D_MODEL), lambda i, j: (i, 0)),
            pl.BlockSpec((None, D_MODEL, IN_TN), lambda i, j: (l, 0, j)),
        ],
        out_specs=pl.BlockSpec((IN_TM, IN_TN), lambda i, j: (i, j)),
        out_shape=jax.ShapeDtypeStruct((t, n_cols), BF16),
        compiler_params=_cparams(("parallel", "arbitrary")),
        name=name,
    )(u, w)


def _proj_d_kernel(u_ref, w_ref, wcf_ref, z_ref, cf_ref):
    u = u_ref[...]
    z_ref[...] = jnp.dot(u, w_ref[...], preferred_element_type=F32).astype(BF16)
    cf_ref[...] = jnp.dot(u, wcf_ref[...], preferred_element_type=F32)


def _proj_d(u, w, wcf, l):
    t = u.shape[0]
    return pl.pallas_call(
        _proj_d_kernel,
        grid=(t // IN_TM,),
        in_specs=[
            pl.BlockSpec((IN_TM, D_MODEL), lambda i: (i, 0)),
            pl.BlockSpec((None, D_MODEL, ZD_W), lambda i: (l, 0, 0)),
            pl.BlockSpec((None, D_MODEL, CF_PAD), lambda i: (l, 0, 0)),
        ],
        out_specs=[
            pl.BlockSpec((IN_TM, ZD_W), lambda i: (i, 0)),
            pl.BlockSpec((IN_TM, CF_PAD), lambda i: (i, 0)),
        ],
        out_shape=[jax.ShapeDtypeStruct((t, ZD_W), BF16), jax.ShapeDtypeStruct((t, CF_PAD), F32)],
        compiler_params=_cparams(("parallel",)),
        name="proj_d",
    )(u, w, wcf)


def _tri_tables(n):
    qi = [i for i in range(n) for _ in range(i + 1)]
    kj = [j for i in range(n) for j in range(i + 1)]
    return jnp.asarray(qi, jnp.int32), jnp.asarray(kj, jnp.int32)


def _flash_rows(q_ref, k_ref, v2_sc, bias, m_sc, acc_sc, r0, diag_q0, scale):
    rows = FLASH_ROWS
    nk = k_ref.shape[0] if diag_q0 is None else diag_q0 + rows
    s = lax.dot_general(q_ref[r0:r0 + rows, :], k_ref[0:nk, :], (((1,), (1,)), ((), ())),
                        preferred_element_type=F32)
    if scale is not None:
        s = s * scale
    s = s + bias[:, 0:nk]
    if diag_q0 is not None:
        qpos = lax.broadcasted_iota(jnp.int32, s.shape, 0) + diag_q0
        kpos = lax.broadcasted_iota(jnp.int32, s.shape, 1)
        s = jnp.where(kpos <= qpos, s, NEG_INF)
    m_prev = m_sc[r0:r0 + rows, :]
    m_new = jnp.maximum(m_prev, jnp.max(s, axis=-1, keepdims=True))
    alpha = jnp.exp(m_prev - m_new)
    p = jnp.exp(s - jnp.concatenate([m_new] * (nk // 128), axis=1))
    pv = jnp.dot(p.astype(BF16), v2_sc[0:nk, :], preferred_element_type=F32)
    acc_sc[r0:r0 + rows, :] = jnp.concatenate([alpha, alpha], axis=1) * acc_sc[r0:r0 + rows, :] + pv
    m_sc[r0:r0 + rows, :] = m_new


def _flash_block(q_ref, k_ref, v2_sc, bias, m_sc, acc_sc, n_rows, tq, diag, scale):
    for r0 in range(0, n_rows, FLASH_ROWS):
        _flash_rows(q_ref, k_ref, v2_sc, bias, m_sc, acc_sc, r0, (r0 % tq) if diag else None, scale)


def _flash_init(v2_sc, m_sc, acc_sc):
    v2_sc[:, 128:256] = jnp.ones((v2_sc.shape[0], 128), BF16)
    m_sc[...] = jnp.full(m_sc.shape, -jnp.inf, F32)
    acc_sc[...] = jnp.zeros(acc_sc.shape, F32)


def _diff_attn_kernel(qi_ref, kj_ref, slope_ref, lami_ref, q_ref, k_ref, v_ref, lv_ref, sg_ref, o_ref,
                      q2_sc, v2_sc, m_sc, acc_sc):
    tq = A_TQ
    t = pl.program_id(2)
    i = qi_ref[t]
    j = kj_ref[t]
    heads = [(hh, slice(hh * 128, (hh + 1) * 128)) for hh in range(ATTN_HPS)]

    @pl.when(j == 0)
    def _():
        for hh, cols in heads:
            q = q_ref[:, cols]
            lane = lax.broadcasted_iota(jnp.int32, q.shape, 1)
            qs = q * jnp.asarray(A_HD ** -0.5, BF16)
            zero = jnp.zeros_like(qs)
            q2_sc[hh, 0:tq, :] = jnp.where(lane < A_HD, qs, zero)
            q2_sc[hh, tq:2 * tq, :] = jnp.where(lane >= A_HD, qs, zero)
            _flash_init(v2_sc.at[hh], m_sc.at[hh], acc_sc.at[hh])

    kpos = (lax.broadcasted_iota(jnp.int32, (1, tq), 1) + (j - i) * tq).astype(F32)

    def block(diag):
        for hh, cols in heads:
            v2_sc[hh, :, 0:128] = v_ref[:, cols]
            bias = kpos * slope_ref[pl.program_id(1) * ATTN_HPS + hh]
            _flash_block(q2_sc.at[hh], k_ref.at[:, cols], v2_sc.at[hh], bias, m_sc.at[hh], acc_sc.at[hh],
                         2 * tq, tq, diag, None)

    @pl.when(j < i)
    def _():
        block(False)

    @pl.when(j == i)
    def _():
        block(True)
        lv = lv_ref[...]
        lam = (jnp.exp(jnp.sum(lv[0:1] * lv[1:2], axis=-1, keepdims=True))
               - jnp.exp(jnp.sum(lv[2:3] * lv[3:4], axis=-1, keepdims=True)) + lami_ref[0])
        for hh, cols in heads:
            acc = acc_sc[hh]
            o = acc[0:tq, 0:128] / acc[0:tq, 128:256] - lam * (acc[tq:2 * tq, 0:128] / acc[tq:2 * tq, 128:256])
            o_ref[:, cols] = (_rms(o, sg_ref[...]) * lami_ref[1]).astype(BF16)


def _diff_attn(z, lvec, subln, lam_init, batch, seq):
    nq = seq // A_TQ
    qi, kj = _tri_tables(nq)
    slopes = jnp.asarray(np.exp2(-8.0 * np.arange(1, A_HEADS + 1, dtype=np.float32) / A_HEADS), F32)
    w = 128 * ATTN_HPS
    qc, kc, vc = AQ0 // w, AK0 // w, AV0 // w
    smem = pl.BlockSpec(memory_space=pltpu.SMEM)
    grid_spec = pltpu.PrefetchScalarGridSpec(
        num_scalar_prefetch=2,
        grid=(batch, A_HEADS // ATTN_HPS, int(qi.shape[0])),
        in_specs=[
            smem, smem,
            pl.BlockSpec((A_TQ, w), lambda b, h, t, qi, kj: (b * nq + qi[t], qc + h)),
            pl.BlockSpec((A_TQ, w), lambda b, h, t, qi, kj: (b * nq + kj[t], kc + h)),
            pl.BlockSpec((A_TQ, w), lambda b, h, t, qi, kj: (b * nq + kj[t], vc + h)),
            pl.BlockSpec((4, A_HD), lambda b, h, t, qi, kj: (0, 0)),
            pl.BlockSpec((1, 2 * A_HD), lambda b, h, t, qi, kj: (0, 0)),
        ],
        out_specs=pl.BlockSpec((A_TQ, w), lambda b, h, t, qi, kj: (b * nq + qi[t], h)),
        scratch_shapes=[
            pltpu.VMEM((ATTN_HPS, 2 * A_TQ, 128), BF16),
            pltpu.VMEM((ATTN_HPS, A_TQ, 256), BF16),
            pltpu.VMEM((ATTN_HPS, 2 * A_TQ, 128), F32),
            pltpu.VMEM((ATTN_HPS, 2 * A_TQ, 256), F32),
        ],
    )
    return pl.pallas_call(
        _diff_attn_kernel,
        grid_spec=grid_spec,
        out_shape=jax.ShapeDtypeStruct((batch * seq, BRANCH_W), BF16),
        compiler_params=_cparams(("parallel", "parallel", "arbitrary")),
        name="diff_attn",
    )(qi, kj, slopes, lam_init, z, z, z, lvec, subln)


def _fox_cum_kernel(cf_ref, bf_ref, o_ref):
    x = cf_ref[...] + bf_ref[...]
    logf = jnp.minimum(x, 0.0) - jnp.log1p(jnp.exp(-jnp.abs(x)))
    rows, seq = logf.shape
    lane = lax.broadcasted_iota(jnp.int32, (rows, 128), 1)
    carry = jnp.zeros((rows, 1), F32)
    for c in range(seq // 128):
        y = logf[:, c * 128:(c + 1) * 128]
        sh = 1
        while sh < 128:
            y = y + jnp.where(lane >= sh, pltpu.roll(y, sh, axis=1), 0.0)
            sh *= 2
        y = y + carry
        o_ref[:, c * 128:(c + 1) * 128] = y
        carry = y[:, 127:128]


def _fox_cum(cf_t, bf_t):
    rows, seq = cf_t.shape
    return pl.pallas_call(
        _fox_cum_kernel,
        grid=(1,),
        in_specs=[pl.BlockSpec((rows, seq), lambda i: (0, 0)), pl.BlockSpec((rows, 1), lambda i: (0, 0))],
        out_specs=pl.BlockSpec((rows, seq), lambda i: (0, 0)),
        out_shape=jax.ShapeDtypeStruct((rows, seq), F32),
        compiler_params=_cparams(("arbitrary",)),
        name="fox_cum",
    )(cf_t, bf_t)


def _fox_attn_kernel(qi_ref, kj_ref, q_ref, k_ref, v_ref, cum_ref, o_ref, v2_sc, m_sc, acc_sc):
    tq = C_TQ
    t = pl.program_id(2)
    i = qi_ref[t]
    j = kj_ref[t]
    heads = [(hh, slice(hh * 128, (hh + 1) * 128)) for hh in range(ATTN_HPS)]

    @pl.when(j == 0)
    def _():
        for hh, _ in heads:
            _flash_init(v2_sc.at[hh], m_sc.at[hh], acc_sc.at[hh])

    def block(diag):
        for hh, cols in heads:
            v2_sc[hh, :, 0:128] = v_ref[:, cols]
            _flash_block(q_ref.at[:, cols], k_ref.at[:, cols], v2_sc.at[hh], -cum_ref[hh], m_sc.at[hh],
                         acc_sc.at[hh], tq, tq, diag, C_HD ** -0.5)

    @pl.when(j < i)
    def _():
        block(False)

    @pl.when(j == i)
    def _():
        block(True)
        for hh, cols in heads:
            acc = acc_sc[hh]
            o_ref[:, cols] = (acc[:, 0:128] / acc[:, 128:256]).astype(BF16)


def _fox_attn(z, cum, batch, seq):
    nq = seq // C_TQ
    qi, kj = _tri_tables(nq)
    w = 128 * ATTN_HPS
    qc, kc, vc = CQ0 // w, CK0 // w, CV0 // w
    hg = C_HEADS // ATTN_HPS
    grid_spec = pltpu.PrefetchScalarGridSpec(
        num_scalar_prefetch=2,
        grid=(batch, hg, int(qi.shape[0])),
        in_specs=[
            pl.BlockSpec((C_TQ, w), lambda b, h, t, qi, kj: (b * nq + qi[t], qc + h)),
            pl.BlockSpec((C_TQ, w), lambda b, h, t, qi, kj: (b * nq + kj[t], kc + h)),
            pl.BlockSpec((C_TQ, w), lambda b, h, t, qi, kj: (b * nq + kj[t], vc + h)),
            pl.BlockSpec((ATTN_HPS, 1, C_TQ), lambda b, h, t, qi, kj: (b * hg + h, 0, kj[t])),
        ],
        out_specs=pl.BlockSpec((C_TQ, w), lambda b, h, t, qi, kj: (b * nq + qi[t], h)),
        scratch_shapes=[
            pltpu.VMEM((ATTN_HPS, C_TQ, 256), BF16),
            pltpu.VMEM((ATTN_HPS, C_TQ, 128), F32),
            pltpu.VMEM((ATTN_HPS, C_TQ, 256), F32),
        ],
    )
    return pl.pallas_call(
        _fox_attn_kernel,
        grid_spec=grid_spec,
        out_shape=jax.ShapeDtypeStruct((batch * seq, BRANCH_W), BF16),
        compiler_params=_cparams(("parallel", "parallel", "arbitrary")),
        name="fox_attn",
    )(qi, kj, z, z, z, cum)


def _swa_kernel(sink_ref, q_ref, kc_ref, kp_ref, vc_ref, vp_ref, bias_ref, o_ref):
    blk = SWA_BLOCK
    first = pl.program_id(1) == 0
    q_all = q_ref[...] * jnp.asarray(D_HD ** -0.5, BF16)
    kk_all = jnp.concatenate([kp_ref[...], kc_ref[...]], axis=0)
    vv_all = jnp.concatenate([vp_ref[...], vc_ref[...]], axis=0)
    for qb in range(SWA_QB):
        q = q_all[qb * blk:(qb + 1) * blk]
        kk = kk_all[qb * blk:(qb + 2) * blk]
        vv = vv_all[qb * blk:(qb + 2) * blk]
        outs = []
        for kv in range(D_KV):
            k_h = kk[:, kv * D_HD:(kv + 1) * D_HD]
            v_h = vv[:, kv * D_HD:(kv + 1) * D_HD]
            for g in range(D_GROUP):
                hh = kv * D_GROUP + g
                q_h = q[:, hh * D_HD:(hh + 1) * D_HD]
                s = lax.dot_general(q_h, k_h, (((1,), (1,)), ((), ())), preferred_element_type=F32)
                bias = jnp.where(first, bias_ref[0, hh], bias_ref[1, hh]) if qb == 0 else bias_ref[1, hh]
                lg = s + bias
                sink = sink_ref[hh]
                m = jnp.maximum(jnp.max(lg, axis=-1, keepdims=True), sink)
                e = jnp.exp(lg - m)
                p = e / (jnp.sum(e, axis=-1, keepdims=True) + jnp.exp(sink - m))
                outs.append(jnp.dot(p.astype(BF16), v_h, preferred_element_type=F32))
        o_ref[qb * blk:(qb + 1) * blk, :] = jnp.concatenate(outs, axis=1).astype(BF16)


def _swa_bias():
    slopes = np.exp2(-8.0 * np.arange(1, D_HEADS + 1, dtype=np.float32) / D_HEADS).astype(np.float32)
    qi = np.arange(SWA_BLOCK)
    kj = np.arange(2 * SWA_BLOCK) - SWA_BLOCK
    dist = (qi[:, None] - kj[None, :]).astype(np.float32)
    valid = (dist >= 0) & (dist < WINDOW)
    general = np.where(valid[None], -(slopes[:, None, None] * dist[None]), np.float32(NEG_INF))
    first = np.where((kj >= 0)[None, None, :], general, np.float32(NEG_INF))
    return jnp.asarray(np.stack([first, general]).astype(np.float32))


def _swa(z, sinks, batch, seq):
    rows = SWA_QB * SWA_BLOCK
    ns = seq // rows
    qc, kc, vc = DQ0 // BRANCH_W, DK0 // 128, DV0 // 128
    cur = lambda b, n: b * ns + n
    prev = lambda b, n: jnp.maximum((b * ns + n) * SWA_QB - 1, 0)
    return pl.pallas_call(
        _swa_kernel,
        grid=(batch, ns),
        in_specs=[
            pl.BlockSpec(memory_space=pltpu.SMEM),
            pl.BlockSpec((rows, BRANCH_W), lambda b, n: (cur(b, n), qc)),
            pl.BlockSpec((rows, 128), lambda b, n: (cur(b, n), kc)),
            pl.BlockSpec((SWA_BLOCK, 128), lambda b, n: (prev(b, n), kc)),
            pl.BlockSpec((rows, 128), lambda b, n: (cur(b, n), vc)),
            pl.BlockSpec((SWA_BLOCK, 128), lambda b, n: (prev(b, n), vc)),
            pl.BlockSpec((2, D_HEADS, SWA_BLOCK, 2 * SWA_BLOCK), lambda b, n: (0, 0, 0, 0)),
        ],
        out_specs=pl.BlockSpec((rows, BRANCH_W), lambda b, n: (cur(b, n), 0)),
        out_shape=jax.ShapeDtypeStruct((batch * seq, BRANCH_W), BF16),
        compiler_params=_cparams(("parallel", "arbitrary")),
        name="swa",
    )(sinks, z, z, z, z, z, _swa_bias())


def _lru_kernel(x_ref, g_ref, cw_ref, cb_ref, wa_ref, ba_ref, wx_ref, bx_ref, lam_ref, o_ref,
                xpad_sc, apad_sc, upad_sc, hc_sc):
    ts, pad = LRU_TS, LRU_PAD

    @pl.when(pl.program_id(1) == 0)
    def _():
        xpad_sc[0:8, :] = jnp.zeros((8, BRANCH_W), F32)
        hc_sc[...] = jnp.zeros(hc_sc.shape, F32)
        apad_sc[0:pad, :] = jnp.ones((pad, BRANCH_W), F32)
        upad_sc[0:pad, :] = jnp.zeros((pad, BRANCH_W), F32)

    x = x_ref[...].astype(F32)
    xpad_sc[8:8 + ts, :] = x
    cw = cw_ref[...]
    xc = (cw[3:4] * x + cw[2:3] * xpad_sc[7:7 + ts, :] + cw[1:2] * xpad_sc[6:6 + ts, :]
          + cw[0:1] * xpad_sc[5:5 + ts, :] + cb_ref[...])
    xpad_sc[0:8, :] = x[ts - 8:ts]

    xcb = xc.astype(BF16)
    r = jax.nn.sigmoid(jnp.dot(xcb, wa_ref[...], preferred_element_type=F32) + ba_ref[...])
    gi = jax.nn.sigmoid(jnp.dot(xcb, wx_ref[...], preferred_element_type=F32) + bx_ref[...])
    log_a = (-B_C * r) * _softplus(-lam_ref[...])
    a = jnp.exp(log_a)
    u = jnp.sqrt(1.0 - jnp.exp(2.0 * log_a)) * (gi * xc)

    sh = 1
    while sh < ts:
        apad_sc[pad:pad + ts, :] = a
        upad_sc[pad:pad + ts, :] = u
        u = a * upad_sc[pad - sh:pad - sh + ts, :] + u
        a = a * apad_sc[pad - sh:pad - sh + ts, :]
        sh *= 2
    h = a * hc_sc[...] + u
    hc_sc[...] = h[ts - 1:ts]
    o_ref[...] = (jax.nn.gelu(g_ref[...].astype(F32)) * h).astype(BF16)


def _lru(z, cw, cb, wa, ba, wx, bx, lam, l, batch, seq):
    ns = seq // LRU_TS
    xc_, gc_ = BX0 // BRANCH_W, BG0 // BRANCH_W
    const = lambda shape: pl.BlockSpec(shape, lambda b, s: (0, 0))
    return pl.pallas_call(
        _lru_kernel,
        grid=(batch, ns),
        in_specs=[
            pl.BlockSpec((LRU_TS, BRANCH_W), lambda b, s: (b * ns + s, xc_)),
            pl.BlockSpec((LRU_TS, BRANCH_W), lambda b, s: (b * ns + s, gc_)),
            const((B_CONV, BRANCH_W)), const((1, BRANCH_W)),
            pl.BlockSpec((None, BRANCH_W, BRANCH_W), lambda b, s: (l, 0, 0)), const((1, BRANCH_W)),
            pl.BlockSpec((None, BRANCH_W, BRANCH_W), lambda b, s: (l, 0, 0)), const((1, BRANCH_W)),
            const((1, BRANCH_W)),
        ],
        out_specs=pl.BlockSpec((LRU_TS, BRANCH_W), lambda b, s: (b * ns + s, 0)),
        out_shape=jax.ShapeDtypeStruct((batch * seq, BRANCH_W), BF16),
        scratch_shapes=[
            pltpu.VMEM((LRU_TS + 8, BRANCH_W), F32),
            pltpu.VMEM((LRU_TS + LRU_PAD, BRANCH_W), F32),
            pltpu.VMEM((LRU_TS + LRU_PAD, BRANCH_W), F32),
            pltpu.VMEM((1, BRANCH_W), F32),
        ],
        compiler_params=_cparams(("parallel", "arbitrary")),
        name="lru",
    )(z, z, cw, cb, wa, ba, wx, bx, lam)


def _merge_kernel(h_ref, ya_ref, yb_ref, yc_ref, yd_ref, g0_ref, g1_ref, g2_ref, g3_ref, bg_ref, wb_ref, wo_ref,
                  gn_ref, o_ref, v_ref):
    ys = (ya_ref, yb_ref, yc_ref, yd_ref)
    gs = (g0_ref, g1_ref, g2_ref, g3_ref)
    mixed = None
    for n in range(N_BRANCH):
        proj = jnp.dot(ys[n][...], wb_ref[n], preferred_element_type=F32)
        term = jax.nn.sigmoid(gs[n][...].astype(F32) + bg_ref[n:n + 1, :]) * proj
        mixed = term if mixed is None else mixed + term
    o = h_ref[...] + jnp.dot(mixed.astype(BF16), wo_ref[...], preferred_element_type=F32)
    o_ref[...] = o
    v_ref[...] = _rms(o, gn_ref[...]).astype(BF16)


def _merge(h, ya, yb, yc, yd, zg, bg, wb, wo, gn, l):
    t = h.shape[0]
    row = lambda shape: pl.BlockSpec(shape, lambda i: (i, 0))
    gate = lambda n: pl.BlockSpec((MERGE_TM, D_MODEL), lambda i: (i, n))
    return pl.pallas_call(
        _merge_kernel,
        grid=(t // MERGE_TM,),
        in_specs=[
            row((MERGE_TM, D_MODEL)),
            row((MERGE_TM, BRANCH_W)), row((MERGE_TM, BRANCH_W)), row((MERGE_TM, BRANCH_W)),
            row((MERGE_TM, BRANCH_W)),
            gate(0), gate(1), gate(2), gate(3),
            pl.BlockSpec((N_BRANCH, D_MODEL), lambda i: (0, 0)),
            pl.BlockSpec((None, N_BRANCH, BRANCH_W, D_MODEL), lambda i: (l, 0, 0, 0), pipeline_mode=pl.Buffered(1)),
            pl.BlockSpec((None, D_MODEL, D_MODEL), lambda i: (l, 0, 0), pipeline_mode=pl.Buffered(1)),
            pl.BlockSpec((1, D_MODEL), lambda i: (0, 0)),
        ],
        out_specs=[row((MERGE_TM, D_MODEL)), row((MERGE_TM, D_MODEL))],
        out_shape=[jax.ShapeDtypeStruct((t, D_MODEL), F32), jax.ShapeDtypeStruct((t, D_MODEL), BF16)],
        compiler_params=_cparams(("parallel",)),
        name="merge",
    )(h, ya, yb, yc, yd, zg, zg, zg, zg, bg, wb, wo, gn)


def _ffn_kernel(v_ref, h_ref, wg_ref, wu_ref, wd_ref, gn_ref, o_ref, *u_ref, final_norm):
    j = pl.program_id(1)

    @pl.when(j == 0)
    def _():
        o_ref[...] = h_ref[...]

    v = v_ref[...]
    a = jnp.dot(v, wg_ref[...], preferred_element_type=F32)
    b = jnp.dot(v, wu_ref[...], preferred_element_type=F32)
    t = (a * jax.nn.sigmoid(a)) * b
    o_ref[...] += jnp.dot(t.astype(BF16), wd_ref[...], preferred_element_type=F32)

    @pl.when(j == pl.num_programs(1) - 1)
    def _():
        y = _rms(o_ref[...], gn_ref[...])
        if final_norm:
            o_ref[...] = y
        else:
            u_ref[0][...] = y.astype(BF16)


def _ffn(v, h, wg, wu, wd, gn, l, final_norm):
    t = h.shape[0]
    row = lambda: pl.BlockSpec((FFN_TM, D_MODEL), lambda i, j: (i, 0))
    out_specs = [row()] if final_norm else [row(), row()]
    out_shape = [jax.ShapeDtypeStruct((t, D_MODEL), F32)]
    if not final_norm:
        out_shape.append(jax.ShapeDtypeStruct((t, D_MODEL), BF16))
    return pl.pallas_call(
        functools.partial(_ffn_kernel, final_norm=final_norm),
        grid=(t // FFN_TM, D_FF // FFN_TF),
        in_specs=[
            row(), row(),
            pl.BlockSpec((None, D_MODEL, FFN_TF), lambda i, j: (l, 0, j)),
            pl.BlockSpec((None, D_MODEL, FFN_TF), lambda i, j: (l, 0, j)),
            pl.BlockSpec((None, FFN_TF, D_MODEL), lambda i, j: (l, j, 0)),
            pl.BlockSpec((1, D_MODEL), lambda i, j: (0, 0)),
        ],
        out_specs=out_specs,
        out_shape=out_shape,
        compiler_params=_cparams(("parallel", "arbitrary")),
        name="ffn_final" if final_norm else "ffn",
    )(v, h, wg, wu, wd, gn)


def _block_diag(w):
    depth, nb, c, d = w.shape
    eye = jnp.eye(nb, dtype=w.dtype)
    return jnp.einsum('lncd,nm->lncmd', w, eye).reshape(depth, nb * c, nb * d)


def kernel(x, norm_mix, w_in, b_gate, diff_lq1, diff_lk1, diff_lq2, diff_lk2, diff_subln, lru_conv_w, lru_conv_b,
           lru_wa, lru_ba, lru_wx, lru_bx, lru_lambda, fox_b_f, swa_sinks, w_branch, w_out, norm_ffn, w_ffn_gate,
           w_ffn_up, w_ffn_down, norm_final):
    batch, seq, d = x.shape
    depth = w_in.shape[0]
    assert d == D_MODEL and seq % max(C_TQ, LRU_TS, SWA_QB * SWA_BLOCK) == 0 and (batch * seq) % IN_TM == 0
    t = batch * seq

    w_bf = w_in.astype(BF16)
    w_gz = w_bf[:, :, _R_GZ0:]
    w_d = w_bf[:, :, _R_DQ0:_R_GZ0]
    w_cf = jnp.pad(w_bf[:, :, _R_CF0:_R_DQ0], ((0, 0), (0, 0), (0, CF_PAD - C_HEADS)))
    wb = w_branch.astype(BF16)
    wo = w_out.astype(BF16)
    wg = w_ffn_gate.astype(BF16)
    wu = w_ffn_up.astype(BF16)
    wd = w_ffn_down.astype(BF16)
    wa = _block_diag(lru_wa).astype(BF16)
    wx = _block_diag(lru_wx).astype(BF16)
    lvec = jnp.stack([diff_lq1, diff_lk1, diff_lq2, diff_lk2], axis=1).astype(F32)
    bg = b_gate.reshape(depth, N_BRANCH, D_MODEL)
    gf = norm_final.reshape(1, D_MODEL)

    h = x.reshape(t, D_MODEL)
    u = _norm(h, norm_mix[0].reshape(1, D_MODEL))
    for l in range(depth):
        lam_init = 0.8 - 0.6 * math.exp(-0.3 * l)
        zg = _proj(u, w_gz, l, ZG_W, "proj_gate")
        zm = _proj(u, w_bf, l, ZM_W, "proj_mix")
        zd, cf = _proj_d(u, w_d, w_cf, l)

        ya = _diff_attn(zm, lvec[l], diff_subln[l].reshape(1, 2 * A_HD),
                        jnp.asarray([lam_init, 1.0 - lam_init], F32), batch, seq)
        yb = _lru(zm, lru_conv_w[l], lru_conv_b[l].reshape(1, -1), wa, lru_ba[l].reshape(1, -1), wx,
                  lru_bx[l].reshape(1, -1), lru_lambda[l].reshape(1, -1), l, batch, seq)
        cf_t = cf[:, :C_HEADS].reshape(batch, seq, C_HEADS).transpose(0, 2, 1).reshape(batch * C_HEADS, seq)
        bf_t = jnp.tile(fox_b_f[l], batch).reshape(batch * C_HEADS, 1)
        cum = _fox_cum(cf_t, bf_t).reshape(batch * C_HEADS, 1, seq)
        yc = _fox_attn(zm, cum, batch, seq)
        yd = _swa(zd, swa_sinks[l], batch, seq)

        h, v = _merge(h, ya, yb, yc, yd, zg, bg[l], wb, wo, norm_ffn[l].reshape(1, D_MODEL), l)
        if l == depth - 1:
            (h,) = _ffn(v, h, wg, wu, wd, gf, l, final_norm=True)
        else:
            h, u = _ffn(v, h, wg, wu, wd, norm_mix[l + 1].reshape(1, D_MODEL), l, final_norm=False)
    return h.reshape(batch, seq, D_MODEL)
```

```python
import functools
import math

import jax
import jax.numpy as jnp
import numpy as np
from jax import lax
from jax.experimental import pallas as pl
from jax.experimental.pallas import tpu as pltpu

F32 = jnp.float32
BF16 = jnp.bfloat16

D_MODEL = 2048
N_BRANCH = 4
BRANCH_W = D_MODEL // 4
A_HEADS = 4
A_HD = BRANCH_W // (2 * A_HEADS)
B_BLOCKS = 8
B_CONV = 4
B_C = 8.0
C_HEADS = 4
C_HD = BRANCH_W // C_HEADS
D_HEADS = 8
D_KV = 2
D_GROUP = D_HEADS // D_KV
D_HD = BRANCH_W // D_HEADS
WINDOW = 128
SWA_BLOCK = 128
SWA_QB = 4
D_FF = -(-8 * D_MODEL // (3 * 256)) * 256
RMS_EPS = 1e-6
NEG_INF = -1e30

AQ0 = 0
AK0 = AQ0 + BRANCH_W
AV0 = AK0 + BRANCH_W
BX0 = AV0 + BRANCH_W
BG0 = BX0 + BRANCH_W
CQ0 = BG0 + BRANCH_W
CK0 = CQ0 + BRANCH_W
CV0 = CK0 + BRANCH_W
ZM_W = CV0 + BRANCH_W
DQ0 = 0
DK0 = DQ0 + BRANCH_W
DV0 = DK0 + D_KV * D_HD
ZD_W = DV0 + D_KV * D_HD
ZG_W = N_BRANCH * D_MODEL
CF_PAD = 128

_R_CF0 = ZM_W
_R_DQ0 = _R_CF0 + C_HEADS
_R_GZ0 = _R_DQ0 + ZD_W

V7X_VMEM_LIMIT = 56 * 1024 * 1024

NORM_TM = 1024
IN_TM, IN_TN = 2048, 1024
FLASH_ROWS = 256
A_TQ, A_HPS = 1024, 1
C_TQ, C_HPS = 1024, 2
LOG2E = math.log2(math.e)
LRU_TS = 512
LRU_PAD = LRU_TS // 2
MERGE_TM = 256
FFN_TM, FFN_TF = 512, 512


def _cparams(sem):
    return pltpu.CompilerParams(dimension_semantics=sem, vmem_limit_bytes=V7X_VMEM_LIMIT)


def _rms(x, g):
    return x * lax.rsqrt(jnp.mean(x * x, axis=-1, keepdims=True) + RMS_EPS) * g


def _softplus(y):
    return jnp.maximum(y, 0.0) + jnp.log1p(jnp.exp(-jnp.abs(y)))


def _norm_kernel(x_ref, g_ref, o_ref):
    o_ref[...] = _rms(x_ref[...], g_ref[...]).astype(BF16)


def _norm(x, g):
    t = x.shape[0]
    return pl.pallas_call(
        _norm_kernel,
        grid=(t // NORM_TM,),
        in_specs=[pl.BlockSpec((NORM_TM, D_MODEL), lambda i: (i, 0)), pl.BlockSpec((1, D_MODEL), lambda i: (0, 0))],
        out_specs=pl.BlockSpec((NORM_TM, D_MODEL), lambda i: (i, 0)),
        out_shape=jax.ShapeDtypeStruct((t, D_MODEL), BF16),
        compiler_params=_cparams(("parallel",)),
        name="norm0",
    )(x, g)


def _proj_kernel(u_ref, w_ref, z_ref):
    z_ref[...] = jnp.dot(u_ref[...], w_ref[...], preferred_element_type=F32).astype(BF16)


def _proj(u, w, l, n_cols, name):
    t = u.shape[0]
    return pl.pallas_call(
        _proj_kernel,
        grid=(t // IN_TM, n_cols // IN_TN),
        in_specs=[
            pl.BlockSpec((IN_TM, D_MODEL), lambda i, j: (i, 0)),
            pl.BlockSpec((None, D_MODEL, IN_TN), lambda i, j: (l, 0, j)),
        ],
        out_specs=pl.BlockSpec((IN_TM, IN_TN), lambda i, j: (i, j)),
        out_shape=jax.ShapeDtypeStruct((t, n_cols), BF16),
        compiler_params=_cparams(("parallel", "arbitrary")),
        name=name,
    )(u, w)


def _proj_d_kernel(u_ref, w_ref, wcf_ref, z_ref, cf_ref):
    u = u_ref[...]
    z_ref[...] = jnp.dot(u, w_ref[...], preferred_element_type=F32).astype(BF16)
    cf_ref[...] = jnp.dot(u, wcf_ref[...], preferred_element_type=F32)


def _proj_d(u, w, wcf, l):
    t = u.shape[0]
    return pl.pallas_call(
        _proj_d_kernel,
        grid=(t // IN_TM,),
        in_specs=[
            pl.BlockSpec((IN_TM, D_MODEL), lambda i: (i, 0)),
            pl.BlockSpec((None, D_MODEL, ZD_W), lambda i: (l, 0, 0)),
            pl.BlockSpec((None, D_MODEL, CF_PAD), lambda i: (l, 0, 0)),
        ],
        out_specs=[
            pl.BlockSpec((IN_TM, ZD_W), lambda i: (i, 0)),
            pl.BlockSpec((IN_TM, CF_PAD), lambda i: (i, 0)),
        ],
        out_shape=[jax.ShapeDtypeStruct((t, ZD_W), BF16), jax.ShapeDtypeStruct((t, CF_PAD), F32)],
        compiler_params=_cparams(("parallel",)),
        name="proj_d",
    )(u, w, wcf)


def _tri_tables(n):
    qi = [i for i in range(n) for _ in range(i + 1)]
    kj = [j for i in range(n) for j in range(i + 1)]
    return jnp.asarray(qi, jnp.int32), jnp.asarray(kj, jnp.int32)


def _flash_rows(q_ref, k_ref, v2_sc, bias, m_sc, acc_sc, r0, diag_q0, scale):
    rows = FLASH_ROWS
    nk = k_ref.shape[0] if diag_q0 is None else diag_q0 + rows
    s = lax.dot_general(q_ref[r0:r0 + rows, :], k_ref[0:nk, :], (((1,), (1,)), ((), ())),
                        preferred_element_type=F32)
    s = s * scale + bias[:, 0:nk]
    if diag_q0 is not None:
        qpos = lax.broadcasted_iota(jnp.int32, s.shape, 0) + diag_q0
        kpos = lax.broadcasted_iota(jnp.int32, s.shape, 1)
        s = jnp.where(kpos <= qpos, s, NEG_INF)
    m_prev = m_sc[r0:r0 + rows, :]
    m_new = jnp.maximum(m_prev, jnp.max(s, axis=-1, keepdims=True))
    alpha = jnp.exp2(m_prev - m_new)
    p = jnp.exp2(s - jnp.concatenate([m_new] * (nk // 128), axis=1))
    pv = jnp.dot(p.astype(BF16), v2_sc[0:nk, :], preferred_element_type=F32)
    acc_sc[r0:r0 + rows, :] = jnp.concatenate([alpha, alpha], axis=1) * acc_sc[r0:r0 + rows, :] + pv
    m_sc[r0:r0 + rows, :] = m_new


def _flash_block(q_ref, k_ref, v2_sc, bias, m_sc, acc_sc, n_rows, tq, diag, scale):
    for r0 in range(0, n_rows, FLASH_ROWS):
        _flash_rows(q_ref, k_ref, v2_sc, bias, m_sc, acc_sc, r0, (r0 % tq) if diag else None, scale)


def _flash_init(v2_sc, m_sc, acc_sc):
    v2_sc[:, 128:256] = jnp.ones((v2_sc.shape[0], 128), BF16)
    m_sc[...] = jnp.full(m_sc.shape, -jnp.inf, F32)
    acc_sc[...] = jnp.zeros(acc_sc.shape, F32)


def _diff_attn_kernel(qi_ref, kj_ref, slope_ref, lami_ref, q_ref, k_ref, v_ref, lv_ref, sg_ref, o_ref,
                      q2_sc, v2_sc, m_sc, acc_sc):
    tq = A_TQ
    t = pl.program_id(2)
    i = qi_ref[t]
    j = kj_ref[t]
    heads = [(hh, slice(hh * 128, (hh + 1) * 128)) for hh in range(A_HPS)]

    @pl.when(j == 0)
    def _():
        for hh, cols in heads:
            q = q_ref[:, cols]
            lane = lax.broadcasted_iota(jnp.int32, q.shape, 1)
            qs = q * jnp.asarray(A_HD ** -0.5, BF16)
            zero = jnp.zeros_like(qs)
            q2_sc[hh, 0:tq, :] = jnp.where(lane < A_HD, qs, zero)
            q2_sc[hh, tq:2 * tq, :] = jnp.where(lane >= A_HD, qs, zero)
            _flash_init(v2_sc.at[hh], m_sc.at[hh], acc_sc.at[hh])

    kpos = (lax.broadcasted_iota(jnp.int32, (1, tq), 1) + (j - i) * tq).astype(F32)

    def block(diag):
        for hh, cols in heads:
            v2_sc[hh, :, 0:128] = v_ref[:, cols]
            bias = kpos * (slope_ref[pl.program_id(1) * A_HPS + hh] * LOG2E)
            _flash_block(q2_sc.at[hh], k_ref.at[:, cols], v2_sc.at[hh], bias, m_sc.at[hh], acc_sc.at[hh],
                         2 * tq, tq, diag, LOG2E)

    @pl.when(j < i)
    def _():
        block(False)

    @pl.when(j == i)
    def _():
        block(True)
        lv = lv_ref[...]
        lam = (jnp.exp(jnp.sum(lv[0:1] * lv[1:2], axis=-1, keepdims=True))
               - jnp.exp(jnp.sum(lv[2:3] * lv[3:4], axis=-1, keepdims=True)) + lami_ref[0])
        for hh, cols in heads:
            acc = acc_sc[hh]
            o = acc[0:tq, 0:128] / acc[0:tq, 128:256] - lam * (acc[tq:2 * tq, 0:128] / acc[tq:2 * tq, 128:256])
            o_ref[:, cols] = (_rms(o, sg_ref[...]) * lami_ref[1]).astype(BF16)


def _diff_attn(z, lvec, subln, lam_init, batch, seq):
    nq = seq // A_TQ
    qi, kj = _tri_tables(nq)
    slopes = jnp.asarray(np.exp2(-8.0 * np.arange(1, A_HEADS + 1, dtype=np.float32) / A_HEADS), F32)
    w = 128 * A_HPS
    qc, kc, vc = AQ0 // w, AK0 // w, AV0 // w
    smem = pl.BlockSpec(memory_space=pltpu.SMEM)
    grid_spec = pltpu.PrefetchScalarGridSpec(
        num_scalar_prefetch=2,
        grid=(batch, A_HEADS // A_HPS, int(qi.shape[0])),
        in_specs=[
            smem, smem,
            pl.BlockSpec((A_TQ, w), lambda b, h, t, qi, kj: (b * nq + qi[t], qc + h)),
            pl.BlockSpec((A_TQ, w), lambda b, h, t, qi, kj: (b * nq + kj[t], kc + h)),
            pl.BlockSpec((A_TQ, w), lambda b, h, t, qi, kj: (b * nq + kj[t], vc + h)),
            pl.BlockSpec((4, A_HD), lambda b, h, t, qi, kj: (0, 0)),
            pl.BlockSpec((1, 2 * A_HD), lambda b, h, t, qi, kj: (0, 0)),
        ],
        out_specs=pl.BlockSpec((A_TQ, w), lambda b, h, t, qi, kj: (b * nq + qi[t], h)),
        scratch_shapes=[
            pltpu.VMEM((A_HPS, 2 * A_TQ, 128), BF16),
            pltpu.VMEM((A_HPS, A_TQ, 256), BF16),
            pltpu.VMEM((A_HPS, 2 * A_TQ, 128), F32),
            pltpu.VMEM((A_HPS, 2 * A_TQ, 256), F32),
        ],
    )
    return pl.pallas_call(
        _diff_attn_kernel,
        grid_spec=grid_spec,
        out_shape=jax.ShapeDtypeStruct((batch * seq, BRANCH_W), BF16),
        compiler_params=_cparams(("parallel", "parallel", "arbitrary")),
        name="diff_attn",
    )(qi, kj, slopes, lam_init, z, z, z, lvec, subln)


def _fox_cum_kernel(cf_ref, bf_ref, o_ref):
    x = cf_ref[...] + bf_ref[...]
    logf = jnp.minimum(x, 0.0) - jnp.log1p(jnp.exp(-jnp.abs(x)))
    rows, seq = logf.shape
    lane = lax.broadcasted_iota(jnp.int32, (rows, 128), 1)
    carry = jnp.zeros((rows, 1), F32)
    for c in range(seq // 128):
        y = logf[:, c * 128:(c + 1) * 128]
        sh = 1
        while sh < 128:
            y = y + jnp.where(lane >= sh, pltpu.roll(y, sh, axis=1), 0.0)
            sh *= 2
        y = y + carry
        o_ref[:, c * 128:(c + 1) * 128] = y
        carry = y[:, 127:128]


def _fox_cum(cf_t, bf_t):
    rows, seq = cf_t.shape
    return pl.pallas_call(
        _fox_cum_kernel,
        grid=(1,),
        in_specs=[pl.BlockSpec((rows, seq), lambda i: (0, 0)), pl.BlockSpec((rows, 1), lambda i: (0, 0))],
        out_specs=pl.BlockSpec((rows, seq), lambda i: (0, 0)),
        out_shape=jax.ShapeDtypeStruct((rows, seq), F32),
        compiler_params=_cparams(("arbitrary",)),
        name="fox_cum",
    )(cf_t, bf_t)


def _fox_attn_kernel(qi_ref, kj_ref, q_ref, k_ref, v_ref, cum_ref, o_ref, v2_sc, m_sc, acc_sc):
    tq = C_TQ
    t = pl.program_id(2)
    i = qi_ref[t]
    j = kj_ref[t]
    heads = [(hh, slice(hh * 128, (hh + 1) * 128)) for hh in range(C_HPS)]

    @pl.when(j == 0)
    def _():
        for hh, _ in heads:
            _flash_init(v2_sc.at[hh], m_sc.at[hh], acc_sc.at[hh])

    def block(diag):
        for hh, cols in heads:
            v2_sc[hh, :, 0:128] = v_ref[:, cols]
            _flash_block(q_ref.at[:, cols], k_ref.at[:, cols], v2_sc.at[hh], cum_ref[hh] * (-LOG2E), m_sc.at[hh],
                         acc_sc.at[hh], tq, tq, diag, C_HD ** -0.5 * LOG2E)

    @pl.when(j < i)
    def _():
        block(False)

    @pl.when(j == i)
    def _():
        block(True)
        for hh, cols in heads:
            acc = acc_sc[hh]
            o_ref[:, cols] = (acc[:, 0:128] / acc[:, 128:256]).astype(BF16)


def _fox_attn(z, cum, batch, seq):
    nq = seq // C_TQ
    qi, kj = _tri_tables(nq)
    w = 128 * C_HPS
    qc, kc, vc = CQ0 // w, CK0 // w, CV0 // w
    hg = C_HEADS // C_HPS
    grid_spec = pltpu.PrefetchScalarGridSpec(
        num_scalar_prefetch=2,
        grid=(batch, hg, int(qi.shape[0])),
        in_specs=[
            pl.BlockSpec((C_TQ, w), lambda b, h, t, qi, kj: (b * nq + qi[t], qc + h)),
            pl.BlockSpec((C_TQ, w), lambda b, h, t, qi, kj: (b * nq + kj[t], kc + h)),
            pl.BlockSpec((C_TQ, w), lambda b, h, t, qi, kj: (b * nq + kj[t], vc + h)),
            pl.BlockSpec((C_HPS, 1, C_TQ), lambda b, h, t, qi, kj: (b * hg + h, 0, kj[t])),
        ],
        out_specs=pl.BlockSpec((C_TQ, w), lambda b, h, t, qi, kj: (b * nq + qi[t], h)),
        scratch_shapes=[
            pltpu.VMEM((C_HPS, C_TQ, 256), BF16),
            pltpu.VMEM((C_HPS, C_TQ, 128), F32),
            pltpu.VMEM((C_HPS, C_TQ, 256), F32),
        ],
    )
    return pl.pallas_call(
        _fox_attn_kernel,
        grid_spec=grid_spec,
        out_shape=jax.ShapeDtypeStruct((batch * seq, BRANCH_W), BF16),
        compiler_params=_cparams(("parallel", "parallel", "arbitrary")),
        name="fox_attn",
    )(qi, kj, z, z, z, cum)


def _swa_kernel(sink_ref, q_ref, kc_ref, kp_ref, vc_ref, vp_ref, bias_ref, o_ref):
    blk = SWA_BLOCK
    first = pl.program_id(1) == 0
    q_all = q_ref[...] * jnp.asarray(D_HD ** -0.5, BF16)
    kk_all = jnp.concatenate([kp_ref[...], kc_ref[...]], axis=0)
    vv_all = jnp.concatenate([vp_ref[...], vc_ref[...]], axis=0)
    for qb in range(SWA_QB):
        q = q_all[qb * blk:(qb + 1) * blk]
        kk = kk_all[qb * blk:(qb + 2) * blk]
        vv = vv_all[qb * blk:(qb + 2) * blk]
        outs = []
        for kv in range(D_KV):
            k_h = kk[:, kv * D_HD:(kv + 1) * D_HD]
            v_h = vv[:, kv * D_HD:(kv + 1) * D_HD]
            for g in range(D_GROUP):
                hh = kv * D_GROUP + g
                q_h = q[:, hh * D_HD:(hh + 1) * D_HD]
                s = lax.dot_general(q_h, k_h, (((1,), (1,)), ((), ())), preferred_element_type=F32)
                bias = jnp.where(first, bias_ref[0, hh], bias_ref[1, hh]) if qb == 0 else bias_ref[1, hh]
                lg = s + bias
                sink = sink_ref[hh]
                m = jnp.maximum(jnp.max(lg, axis=-1, keepdims=True), sink)
                e = jnp.exp(lg - m)
                p = e / (jnp.sum(e, axis=-1, keepdims=True) + jnp.exp(sink - m))
                outs.append(jnp.dot(p.astype(BF16), v_h, preferred_element_type=F32))
        o_ref[qb * blk:(qb + 1) * blk, :] = jnp.concatenate(outs, axis=1).astype(BF16)


def _swa_bias():
    slopes = np.exp2(-8.0 * np.arange(1, D_HEADS + 1, dtype=np.float32) / D_HEADS).astype(np.float32)
    qi = np.arange(SWA_BLOCK)
    kj = np.arange(2 * SWA_BLOCK) - SWA_BLOCK
    dist = (qi[:, None] - kj[None, :]).astype(np.float32)
    valid = (dist >= 0) & (dist < WINDOW)
    general = np.where(valid[None], -(slopes[:, None, None] * dist[None]), np.float32(NEG_INF))
    first = np.where((kj >= 0)[None, None, :], general, np.float32(NEG_INF))
    return jnp.asarray(np.stack([first, general]).astype(np.float32))


def _swa(z, sinks, batch, seq):
    rows = SWA_QB * SWA_BLOCK
    ns = seq // rows
    qc, kc, vc = DQ0 // BRANCH_W, DK0 // 128, DV0 // 128
    cur = lambda b, n: b * ns + n
    prev = lambda b, n: jnp.maximum((b * ns + n) * SWA_QB - 1, 0)
    return pl.pallas_call(
        _swa_kernel,
        grid=(batch, ns),
        in_specs=[
            pl.BlockSpec(memory_space=pltpu.SMEM),
            pl.BlockSpec((rows, BRANCH_W), lambda b, n: (cur(b, n), qc)),
            pl.BlockSpec((rows, 128), lambda b, n: (cur(b, n), kc)),
            pl.BlockSpec((SWA_BLOCK, 128), lambda b, n: (prev(b, n), kc)),
            pl.BlockSpec((rows, 128), lambda b, n: (cur(b, n), vc)),
            pl.BlockSpec((SWA_BLOCK, 128), lambda b, n: (prev(b, n), vc)),
            pl.BlockSpec((2, D_HEADS, SWA_BLOCK, 2 * SWA_BLOCK), lambda b, n: (0, 0, 0, 0)),
        ],
        out_specs=pl.BlockSpec((rows, BRANCH_W), lambda b, n: (cur(b, n), 0)),
        out_shape=jax.ShapeDtypeStruct((batch * seq, BRANCH_W), BF16),
        compiler_params=_cparams(("parallel", "arbitrary")),
        name="swa",
    )(sinks, z, z, z, z, z, _swa_bias())


def _lru_kernel(x_ref, g_ref, cw_ref, cb_ref, wa_ref, ba_ref, wx_ref, bx_ref, lam_ref, o_ref,
                xpad_sc, apad_sc, upad_sc, hc_sc):
    ts, pad = LRU_TS, LRU_PAD

    @pl.when(pl.program_id(1) == 0)
    def _():
        xpad_sc[0:8, :] = jnp.zeros((8, BRANCH_W), F32)
        hc_sc[...] = jnp.zeros(hc_sc.shape, F32)
        apad_sc[0:pad, :] = jnp.ones((pad, BRANCH_W), F32)
        upad_sc[0:pad, :] = jnp.zeros((pad, BRANCH_W), F32)

    x = x_ref[...].astype(F32)
    xpad_sc[8:8 + ts, :] = x
    cw = cw_ref[...]
    xc = (cw[3:4] * x + cw[2:3] * xpad_sc[7:7 + ts, :] + cw[1:2] * xpad_sc[6:6 + ts, :]
          + cw[0:1] * xpad_sc[5:5 + ts, :] + cb_ref[...])
    xpad_sc[0:8, :] = x[ts - 8:ts]

    xcb = xc.astype(BF16)
    r = jax.nn.sigmoid(jnp.dot(xcb, wa_ref[...], preferred_element_type=F32) + ba_ref[...])
    gi = jax.nn.sigmoid(jnp.dot(xcb, wx_ref[...], preferred_element_type=F32) + bx_ref[...])
    log_a = (-B_C * r) * _softplus(-lam_ref[...])
    a = jnp.exp(log_a)
    u = jnp.sqrt(1.0 - jnp.exp(2.0 * log_a)) * (gi * xc)

    sh = 1
    while sh < ts:
        apad_sc[pad:pad + ts, :] = a
        upad_sc[pad:pad + ts, :] = u
        u = a * upad_sc[pad - sh:pad - sh + ts, :] + u
        a = a * apad_sc[pad - sh:pad - sh + ts, :]
        sh *= 2
    h = a * hc_sc[...] + u
    hc_sc[...] = h[ts - 1:ts]
    o_ref[...] = (jax.nn.gelu(g_ref[...].astype(F32)) * h).astype(BF16)


def _lru(z, cw, cb, wa, ba, wx, bx, lam, l, batch, seq):
    ns = seq // LRU_TS
    xc_, gc_ = BX0 // BRANCH_W, BG0 // BRANCH_W
    const = lambda shape: pl.BlockSpec(shape, lambda b, s: (0, 0))
    return pl.pallas_call(
        _lru_kernel,
        grid=(batch, ns),
        in_specs=[
            pl.BlockSpec((LRU_TS, BRANCH_W), lambda b, s: (b * ns + s, xc_)),
            pl.BlockSpec((LRU_TS, BRANCH_W), lambda b, s: (b * ns + s, gc_)),
            const((B_CONV, BRANCH_W)), const((1, BRANCH_W)),
            pl.BlockSpec((None, BRANCH_W, BRANCH_W), lambda b, s: (l, 0, 0)), const((1, BRANCH_W)),
            pl.BlockSpec((None, BRANCH_W, BRANCH_W), lambda b, s: (l, 0, 0)), const((1, BRANCH_W)),
            const((1, BRANCH_W)),
        ],
        out_specs=pl.BlockSpec((LRU_TS, BRANCH_W), lambda b, s: (b * ns + s, 0)),
        out_shape=jax.ShapeDtypeStruct((batch * seq, BRANCH_W), BF16),
        scratch_shapes=[
            pltpu.VMEM((LRU_TS + 8, BRANCH_W), F32),
            pltpu.VMEM((LRU_TS + LRU_PAD, BRANCH_W), F32),
            pltpu.VMEM((LRU_TS + LRU_PAD, BRANCH_W), F32),
            pltpu.VMEM((1, BRANCH_W), F32),
        ],
        compiler_params=_cparams(("parallel", "arbitrary")),
        name="lru",
    )(z, z, cw, cb, wa, ba, wx, bx, lam)


def _merge_kernel(h_ref, ya_ref, yb_ref, yc_ref, yd_ref, g0_ref, g1_ref, g2_ref, g3_ref, bg_ref, wb_ref, wo_ref,
                  gn_ref, o_ref, v_ref):
    ys = (ya_ref, yb_ref, yc_ref, yd_ref)
    gs = (g0_ref, g1_ref, g2_ref, g3_ref)
    mixed = None
    for n in range(N_BRANCH):
        proj = jnp.dot(ys[n][...], wb_ref[n], preferred_element_type=F32)
        term = jax.nn.sigmoid(gs[n][...].astype(F32) + bg_ref[n:n + 1, :]) * proj
        mixed = term if mixed is None else mixed + term
    o = h_ref[...] + jnp.dot(mixed.astype(BF16), wo_ref[...], preferred_element_type=F32)
    o_ref[...] = o
    v_ref[...] = _rms(o, gn_ref[...]).astype(BF16)


def _merge(h, ya, yb, yc, yd, zg, bg, wb, wo, gn, l):
    t = h.shape[0]
    row = lambda shape: pl.BlockSpec(shape, lambda i: (i, 0))
    gate = lambda n: pl.BlockSpec((MERGE_TM, D_MODEL), lambda i: (i, n))
    return pl.pallas_call(
        _merge_kernel,
        grid=(t // MERGE_TM,),
        in_specs=[
            row((MERGE_TM, D_MODEL)),
            row((MERGE_TM, BRANCH_W)), row((MERGE_TM, BRANCH_W)), row((MERGE_TM, BRANCH_W)),
            row((MERGE_TM, BRANCH_W)),
            gate(0), gate(1), gate(2), gate(3),
            pl.BlockSpec((N_BRANCH, D_MODEL), lambda i: (0, 0)),
            pl.BlockSpec((None, N_BRANCH, BRANCH_W, D_MODEL), lambda i: (l, 0, 0, 0), pipeline_mode=pl.Buffered(1)),
            pl.BlockSpec((None, D_MODEL, D_MODEL), lambda i: (l, 0, 0), pipeline_mode=pl.Buffered(1)),
            pl.BlockSpec((1, D_MODEL), lambda i: (0, 0)),
        ],
        out_specs=[row((MERGE_TM, D_MODEL)), row((MERGE_TM, D_MODEL))],
        out_shape=[jax.ShapeDtypeStruct((t, D_MODEL), F32), jax.ShapeDtypeStruct((t, D_MODEL), BF16)],
        compiler_params=_cparams(("parallel",)),
        name="merge",
    )(h, ya, yb, yc, yd, zg, zg, zg, zg, bg, wb, wo, gn)


def _ffn_kernel(v_ref, h_ref, wgu_ref, wd_ref, gn_ref, o_ref, *u_ref, final_norm):
    j = pl.program_id(1)

    @pl.when(j == 0)
    def _():
        o_ref[...] = h_ref[...]

    ab = jnp.dot(v_ref[...], wgu_ref[...], preferred_element_type=F32)
    a = ab[:, 0:FFN_TF]
    b = ab[:, FFN_TF:2 * FFN_TF]
    t = (a * jax.nn.sigmoid(a)) * b
    o_ref[...] += jnp.dot(t.astype(BF16), wd_ref[...], preferred_element_type=F32)

    @pl.when(j == pl.num_programs(1) - 1)
    def _():
        y = _rms(o_ref[...], gn_ref[...])
        if final_norm:
            o_ref[...] = y
        else:
            u_ref[0][...] = y.astype(BF16)


def _ffn(v, h, wgu, wd, gn, l, final_norm):
    t = h.shape[0]
    row = lambda: pl.BlockSpec((FFN_TM, D_MODEL), lambda i, j: (i, 0))
    out_specs = [row()] if final_norm else [row(), row()]
    out_shape = [jax.ShapeDtypeStruct((t, D_MODEL), F32)]
    if not final_norm:
        out_shape.append(jax.ShapeDtypeStruct((t, D_MODEL), BF16))
    return pl.pallas_call(
        functools.partial(_ffn_kernel, final_norm=final_norm),
        grid=(t // FFN_TM, D_FF // FFN_TF),
        in_specs=[
            row(), row(),
            pl.BlockSpec((None, D_MODEL, 2 * FFN_TF), lambda i, j: (l, 0, j)),
            pl.BlockSpec((None, FFN_TF, D_MODEL), lambda i, j: (l, j, 0)),
            pl.BlockSpec((1, D_MODEL), lambda i, j: (0, 0)),
        ],
        out_specs=out_specs,
        out_shape=out_shape,
        compiler_params=_cparams(("parallel", "arbitrary")),
        name="ffn_final" if final_norm else "ffn",
    )(v, h, wgu, wd, gn)


def _block_diag(w):
    depth, nb, c, d = w.shape
    eye = jnp.eye(nb, dtype=w.dtype)
    return jnp.einsum('lncd,nm->lncmd', w, eye).reshape(depth, nb * c, nb * d)


def kernel(x, norm_mix, w_in, b_gate, diff_lq1, diff_lk1, diff_lq2, diff_lk2, diff_subln, lru_conv_w, lru_conv_b,
           lru_wa, lru_ba, lru_wx, lru_bx, lru_lambda, fox_b_f, swa_sinks, w_branch, w_out, norm_ffn, w_ffn_gate,
           w_ffn_up, w_ffn_down, norm_final):
    batch, seq, d = x.shape
    depth = w_in.shape[0]
    assert d == D_MODEL and seq % max(A_TQ, C_TQ, LRU_TS, SWA_QB * SWA_BLOCK) == 0 and (batch * seq) % IN_TM == 0
    t = batch * seq

    w_zm = w_in[:, :, :ZM_W].astype(BF16)
    w_gz = w_in[:, :, _R_GZ0:].astype(BF16)
    w_d = w_in[:, :, _R_DQ0:_R_GZ0].astype(BF16)
    w_cf = jnp.pad(w_in[:, :, _R_CF0:_R_DQ0], ((0, 0), (0, 0), (0, CF_PAD - C_HEADS))).astype(BF16)
    wb = w_branch.astype(BF16)
    wo = w_out.astype(BF16)
    nf = D_FF // FFN_TF
    wgu = jnp.concatenate([w_ffn_gate.reshape(depth, D_MODEL, nf, FFN_TF),
                           w_ffn_up.reshape(depth, D_MODEL, nf, FFN_TF)], axis=3)
    wgu = wgu.reshape(depth, D_MODEL, 2 * D_FF).astype(BF16)
    wd = w_ffn_down.astype(BF16)
    wa = _block_diag(lru_wa).astype(BF16)
    wx = _block_diag(lru_wx).astype(BF16)
    lvec = jnp.stack([diff_lq1, diff_lk1, diff_lq2, diff_lk2], axis=1).astype(F32)
    bg = b_gate.reshape(depth, N_BRANCH, D_MODEL)
    gf = norm_final.reshape(1, D_MODEL)

    h = x.reshape(t, D_MODEL)
    u = _norm(h, norm_mix[0].reshape(1, D_MODEL))
    for l in range(depth):
        lam_init = 0.8 - 0.6 * math.exp(-0.3 * l)
        zg = _proj(u, w_gz, l, ZG_W, "proj_gate")
        zm = _proj(u, w_zm, l, ZM_W, "proj_mix")
        zd, cf = _proj_d(u, w_d, w_cf, l)

        ya = _diff_attn(zm, lvec[l], diff_subln[l].reshape(1, 2 * A_HD),
                        jnp.asarray([lam_init, 1.0 - lam_init], F32), batch, seq)
        yb = _lru(zm, lru_conv_w[l], lru_conv_b[l].reshape(1, -1), wa, lru_ba[l].reshape(1, -1), wx,
                  lru_bx[l].reshape(1, -1), lru_lambda[l].reshape(1, -1), l, batch, seq)
        cf_t = cf[:, :C_HEADS].reshape(batch, seq, C_HEADS).transpose(0, 2, 1).reshape(batch * C_HEADS, seq)
        bf_t = jnp.tile(fox_b_f[l], batch).reshape(batch * C_HEADS, 1)
        cum = _fox_cum(cf_t, bf_t).reshape(batch * C_HEADS, 1, seq)
        yc = _fox_attn(zm, cum, batch, seq)
        yd = _swa(zd, swa_sinks[l], batch, seq)

        h, v = _merge(h, ya, yb, yc, yd, zg, bg[l], wb, wo, norm_ffn[l].reshape(1, D_MODEL), l)
        if l == depth - 1:
            (h,) = _ffn(v, h, wgu, wd, gf, l, final_norm=True)
        else:
            h, u = _ffn(v, h, wgu, wd, norm_mix[l + 1].reshape(1, D_MODEL), l, final_norm=False)
    return h.reshape(batch, seq, D_MODEL)
```

```python
import functools
import math

import jax
import jax.numpy as jnp
import numpy as np
from jax import lax
from jax.experimental import pallas as pl
from jax.experimental.pallas import tpu as pltpu

F32 = jnp.float32
BF16 = jnp.bfloat16

D_MODEL = 2048
N_BRANCH = 4
BRANCH_W = D_MODEL // 4
A_HEADS = 4
A_HD = BRANCH_W // (2 * A_HEADS)
B_BLOCKS = 8
B_CONV = 4
B_C = 8.0
C_HEADS = 4
C_HD = BRANCH_W // C_HEADS
D_HEADS = 8
D_KV = 2
D_GROUP = D_HEADS // D_KV
D_HD = BRANCH_W // D_HEADS
WINDOW = 128
SWA_BLOCK = 128
SWA_QB = 4
D_FF = -(-8 * D_MODEL // (3 * 256)) * 256
RMS_EPS = 1e-6
NEG_INF = -1e30

AQ0 = 0
AK0 = AQ0 + BRANCH_W
AV0 = AK0 + BRANCH_W
BX0 = AV0 + BRANCH_W
BG0 = BX0 + BRANCH_W
CQ0 = BG0 + BRANCH_W
CK0 = CQ0 + BRANCH_W
CV0 = CK0 + BRANCH_W
ZM_W = CV0 + BRANCH_W
DQ0 = 0
DK0 = DQ0 + BRANCH_W
DV0 = DK0 + D_KV * D_HD
ZD_W = DV0 + D_KV * D_HD
ZG_W = N_BRANCH * D_MODEL
CF_PAD = 128

_R_CF0 = ZM_W
_R_DQ0 = _R_CF0 + C_HEADS
_R_GZ0 = _R_DQ0 + ZD_W

V7X_VMEM_LIMIT = 56 * 1024 * 1024

NORM_TM = 1024
IN_TM, IN_TN = 2048, 1024
FLASH_ROWS = 256
A_TQ, A_HPS = 1024, 1
C_TQ, C_HPS = 1024, 2
LOG2E = math.log2(math.e)
LRU_TS = 512
LRU_PAD = LRU_TS // 2
MERGE_TM = 256
FFN_TM, FFN_TF = 512, 512


def _cparams(sem):
    return pltpu.CompilerParams(dimension_semantics=sem, vmem_limit_bytes=V7X_VMEM_LIMIT)


def _rms(x, g):
    return x * lax.rsqrt(jnp.mean(x * x, axis=-1, keepdims=True) + RMS_EPS) * g


def _softplus(y):
    return jnp.maximum(y, 0.0) + jnp.log1p(jnp.exp(-jnp.abs(y)))


def _norm_kernel(x_ref, g_ref, o_ref):
    o_ref[...] = _rms(x_ref[...], g_ref[...]).astype(BF16)


def _norm(x, g):
    t = x.shape[0]
    return pl.pallas_call(
        _norm_kernel,
        grid=(t // NORM_TM,),
        in_specs=[pl.BlockSpec((NORM_TM, D_MODEL), lambda i: (i, 0)), pl.BlockSpec((1, D_MODEL), lambda i: (0, 0))],
        out_specs=pl.BlockSpec((NORM_TM, D_MODEL), lambda i: (i, 0)),
        out_shape=jax.ShapeDtypeStruct((t, D_MODEL), BF16),
        compiler_params=_cparams(("parallel",)),
        name="norm0",
    )(x, g)


def _proj_kernel(u_ref, w_ref, z_ref):
    z_ref[...] = jnp.dot(u_ref[...], w_ref[...], preferred_element_type=F32).astype(BF16)


def _proj(u, w, l, n_cols, name):
    t = u.shape[0]
    return pl.pallas_call(
        _proj_kernel,
        grid=(t // IN_TM, n_cols // IN_TN),
        in_specs=[
            pl.BlockSpec((IN_TM, D_MODEL), lambda i, j: (i, 0)),
            pl.BlockSpec((None, D_MODEL, IN_TN), lambda i, j: (l, 0, j)),
        ],
        out_specs=pl.BlockSpec((IN_TM, IN_TN), lambda i, j: (i, j)),
        out_shape=jax.ShapeDtypeStruct((t, n_cols), BF16),
        compiler_params=_cparams(("parallel", "arbitrary")),
        name=name,
    )(u, w)


def _proj_d_kernel(u_ref, w_ref, wcf_ref, z_ref, cf_ref):
    u = u_ref[...]
    z_ref[...] = jnp.dot(u, w_ref[...], preferred_element_type=F32).astype(BF16)
    cf_ref[...] = jnp.dot(u, wcf_ref[...], preferred_element_type=F32)


def _proj_d(u, w, wcf, l):
    t = u.shape[0]
    return pl.pallas_call(
        _proj_d_kernel,
        grid=(t // IN_TM,),
        in_specs=[
            pl.BlockSpec((IN_TM, D_MODEL), lambda i: (i, 0)),
            pl.BlockSpec((None, D_MODEL, ZD_W), lambda i: (l, 0, 0)),
            pl.BlockSpec((None, D_MODEL, CF_PAD), lambda i: (l, 0, 0)),
        ],
        out_specs=[
            pl.BlockSpec((IN_TM, ZD_W), lambda i: (i, 0)),
            pl.BlockSpec((IN_TM, CF_PAD), lambda i: (i, 0)),
        ],
        out_shape=[jax.ShapeDtypeStruct((t, ZD_W), BF16), jax.ShapeDtypeStruct((t, CF_PAD), F32)],
        compiler_params=_cparams(("parallel",)),
        name="proj_d",
    )(u, w, wcf)


def _tri_tables(n):
    qi = [i for i in range(n) for _ in range(i + 1)]
    kj = [j for i in range(n) for j in range(i + 1)]
    return jnp.asarray(qi, jnp.int32), jnp.asarray(kj, jnp.int32)


def _flash_rows(q_ref, k_ref, v2_sc, bias, m_sc, acc_sc, r0, diag_q0, scale):
    rows = FLASH_ROWS
    nk = k_ref.shape[0] if diag_q0 is None else diag_q0 + rows
    s = lax.dot_general(q_ref[r0:r0 + rows, :], k_ref[0:nk, :], (((1,), (1,)), ((), ())),
                        preferred_element_type=F32)
    s = s * scale + bias[:, 0:nk]
    if diag_q0 is not None:
        qpos = lax.broadcasted_iota(jnp.int32, s.shape, 0) + diag_q0
        kpos = lax.broadcasted_iota(jnp.int32, s.shape, 1)
        s = jnp.where(kpos <= qpos, s, NEG_INF)
    m_prev = m_sc[r0:r0 + rows, :]
    m_new = jnp.maximum(m_prev, jnp.max(s, axis=-1, keepdims=True))
    alpha = jnp.exp2(m_prev - m_new)
    p = jnp.exp2(s - jnp.concatenate([m_new] * (nk // 128), axis=1))
    pv = jnp.dot(p.astype(BF16), v2_sc[0:nk, :], preferred_element_type=F32)
    acc_sc[r0:r0 + rows, :] = jnp.concatenate([alpha, alpha], axis=1) * acc_sc[r0:r0 + rows, :] + pv
    m_sc[r0:r0 + rows, :] = m_new


def _flash_block(q_ref, k_ref, v2_sc, bias, m_sc, acc_sc, n_rows, tq, diag, scale):
    for r0 in range(0, n_rows, FLASH_ROWS):
        _flash_rows(q_ref, k_ref, v2_sc, bias, m_sc, acc_sc, r0, (r0 % tq) if diag else None, scale)


def _flash_init(v2_sc, m_sc, acc_sc):
    v2_sc[:, 128:256] = jnp.ones((v2_sc.shape[0], 128), BF16)
    m_sc[...] = jnp.full(m_sc.shape, -jnp.inf, F32)
    acc_sc[...] = jnp.zeros(acc_sc.shape, F32)


def _diff_attn_kernel(qi_ref, kj_ref, slope_ref, lami_ref, q_ref, k_ref, v_ref, lv_ref, sg_ref, o_ref,
                      q2_sc, v2_sc, m_sc, acc_sc):
    tq = A_TQ
    t = pl.program_id(2)
    i = qi_ref[t]
    j = kj_ref[t]
    heads = [(hh, slice(hh * 128, (hh + 1) * 128)) for hh in range(A_HPS)]

    @pl.when(j == 0)
    def _():
        for hh, cols in heads:
            q = q_ref[:, cols]
            lane = lax.broadcasted_iota(jnp.int32, q.shape, 1)
            qs = q * jnp.asarray(A_HD ** -0.5, BF16)
            zero = jnp.zeros_like(qs)
            q2_sc[hh, 0:tq, :] = jnp.where(lane < A_HD, qs, zero)
            q2_sc[hh, tq:2 * tq, :] = jnp.where(lane >= A_HD, qs, zero)
            _flash_init(v2_sc.at[hh], m_sc.at[hh], acc_sc.at[hh])

    kpos = (lax.broadcasted_iota(jnp.int32, (1, tq), 1) + (j - i) * tq).astype(F32)

    def block(diag):
        for hh, cols in heads:
            v2_sc[hh, :, 0:128] = v_ref[:, cols]
            bias = kpos * (slope_ref[pl.program_id(1) * A_HPS + hh] * LOG2E)
            _flash_block(q2_sc.at[hh], k_ref.at[:, cols], v2_sc.at[hh], bias, m_sc.at[hh], acc_sc.at[hh],
                         2 * tq, tq, diag, LOG2E)

    @pl.when(j < i)
    def _():
        block(False)

    @pl.when(j == i)
    def _():
        block(True)
        lv = lv_ref[...]
        lam = (jnp.exp(jnp.sum(lv[0:1] * lv[1:2], axis=-1, keepdims=True))
               - jnp.exp(jnp.sum(lv[2:3] * lv[3:4], axis=-1, keepdims=True)) + lami_ref[0])
        for hh, cols in heads:
            acc = acc_sc[hh]
            o = acc[0:tq, 0:128] / acc[0:tq, 128:256] - lam * (acc[tq:2 * tq, 0:128] / acc[tq:2 * tq, 128:256])
            o_ref[:, cols] = (_rms(o, sg_ref[...]) * lami_ref[1]).astype(BF16)


def _diff_attn(z, lvec, subln, lam_init, batch, seq):
    nq = seq // A_TQ
    qi, kj = _tri_tables(nq)
    slopes = jnp.asarray(np.exp2(-8.0 * np.arange(1, A_HEADS + 1, dtype=np.float32) / A_HEADS), F32)
    w = 128 * A_HPS
    qc, kc, vc = AQ0 // w, AK0 // w, AV0 // w
    smem = pl.BlockSpec(memory_space=pltpu.SMEM)
    grid_spec = pltpu.PrefetchScalarGridSpec(
        num_scalar_prefetch=2,
        grid=(batch, A_HEADS // A_HPS, int(qi.shape[0])),
        in_specs=[
            smem, smem,
            pl.BlockSpec((A_TQ, w), lambda b, h, t, qi, kj: (b * nq + qi[t], qc + h)),
            pl.BlockSpec((A_TQ, w), lambda b, h, t, qi, kj: (b * nq + kj[t], kc + h)),
            pl.BlockSpec((A_TQ, w), lambda b, h, t, qi, kj: (b * nq + kj[t], vc + h)),
            pl.BlockSpec((4, A_HD), lambda b, h, t, qi, kj: (0, 0)),
            pl.BlockSpec((1, 2 * A_HD), lambda b, h, t, qi, kj: (0, 0)),
        ],
        out_specs=pl.BlockSpec((A_TQ, w), lambda b, h, t, qi, kj: (b * nq + qi[t], h)),
        scratch_shapes=[
            pltpu.VMEM((A_HPS, 2 * A_TQ, 128), BF16),
            pltpu.VMEM((A_HPS, A_TQ, 256), BF16),
            pltpu.VMEM((A_HPS, 2 * A_TQ, 128), F32),
            pltpu.VMEM((A_HPS, 2 * A_TQ, 256), F32),
        ],
    )
    return pl.pallas_call(
        _diff_attn_kernel,
        grid_spec=grid_spec,
        out_shape=jax.ShapeDtypeStruct((batch * seq, BRANCH_W), BF16),
        compiler_params=_cparams(("parallel", "parallel", "arbitrary")),
        name="diff_attn",
    )(qi, kj, slopes, lam_init, z, z, z, lvec, subln)


def _fox_cum_kernel(cf_ref, bf_ref, o_ref):
    x = cf_ref[...] + bf_ref[...]
    logf = jnp.minimum(x, 0.0) - jnp.log1p(jnp.exp(-jnp.abs(x)))
    rows, seq = logf.shape
    lane = lax.broadcasted_iota(jnp.int32, (rows, 128), 1)
    carry = jnp.zeros((rows, 1), F32)
    for c in range(seq // 128):
        y = logf[:, c * 128:(c + 1) * 128]
        sh = 1
        while sh < 128:
            y = y + jnp.where(lane >= sh, pltpu.roll(y, sh, axis=1), 0.0)
            sh *= 2
        y = y + carry
        o_ref[:, c * 128:(c + 1) * 128] = y
        carry = y[:, 127:128]


def _fox_cum(cf_t, bf_t):
    rows, seq = cf_t.shape
    return pl.pallas_call(
        _fox_cum_kernel,
        grid=(1,),
        in_specs=[pl.BlockSpec((rows, seq), lambda i: (0, 0)), pl.BlockSpec((rows, 1), lambda i: (0, 0))],
        out_specs=pl.BlockSpec((rows, seq), lambda i: (0, 0)),
        out_shape=jax.ShapeDtypeStruct((rows, seq), F32),
        compiler_params=_cparams(("arbitrary",)),
        name="fox_cum",
    )(cf_t, bf_t)


def _fox_attn_kernel(qi_ref, kj_ref, q_ref, k_ref, v_ref, cum_ref, o_ref, v2_sc, m_sc, acc_sc):
    tq = C_TQ
    t = pl.program_id(2)
    i = qi_ref[t]
    j = kj_ref[t]
    heads = [(hh, slice(hh * 128, (hh + 1) * 128)) for hh in range(C_HPS)]

    @pl.when(j == 0)
    def _():
        for hh, _ in heads:
            _flash_init(v2_sc.at[hh], m_sc.at[hh], acc_sc.at[hh])

    def block(diag):
        for hh, cols in heads:
            v2_sc[hh, :, 0:128] = v_ref[:, cols]
            _flash_block(q_ref.at[:, cols], k_ref.at[:, cols], v2_sc.at[hh], cum_ref[hh] * (-LOG2E), m_sc.at[hh],
                         acc_sc.at[hh], tq, tq, diag, C_HD ** -0.5 * LOG2E)

    @pl.when(j < i)
    def _():
        block(False)

    @pl.when(j == i)
    def _():
        block(True)
        for hh, cols in heads:
            acc = acc_sc[hh]
            o_ref[:, cols] = (acc[:, 0:128] / acc[:, 128:256]).astype(BF16)


def _fox_attn(z, cum, batch, seq):
    nq = seq // C_TQ
    qi, kj = _tri_tables(nq)
    w = 128 * C_HPS
    qc, kc, vc = CQ0 // w, CK0 // w, CV0 // w
    hg = C_HEADS // C_HPS
    grid_spec = pltpu.PrefetchScalarGridSpec(
        num_scalar_prefetch=2,
        grid=(batch, hg, int(qi.shape[0])),
        in_specs=[
            pl.BlockSpec((C_TQ, w), lambda b, h, t, qi, kj: (b * nq + qi[t], qc + h)),
            pl.BlockSpec((C_TQ, w), lambda b, h, t, qi, kj: (b * nq + kj[t], kc + h)),
            pl.BlockSpec((C_TQ, w), lambda b, h, t, qi, kj: (b * nq + kj[t], vc + h)),
            pl.BlockSpec((C_HPS, 1, C_TQ), lambda b, h, t, qi, kj: (b * hg + h, 0, kj[t])),
        ],
        out_specs=pl.BlockSpec((C_TQ, w), lambda b, h, t, qi, kj: (b * nq + qi[t], h)),
        scratch_shapes=[
            pltpu.VMEM((C_HPS, C_TQ, 256), BF16),
            pltpu.VMEM((C_HPS, C_TQ, 128), F32),
            pltpu.VMEM((C_HPS, C_TQ, 256), F32),
        ],
    )
    return pl.pallas_call(
        _fox_attn_kernel,
        grid_spec=grid_spec,
        out_shape=jax.ShapeDtypeStruct((batch * seq, BRANCH_W), BF16),
        compiler_params=_cparams(("parallel", "parallel", "arbitrary")),
        name="fox_attn",
    )(qi, kj, z, z, z, cum)


def _swa_kernel(sink_ref, q_ref, kc_ref, kp_ref, vc_ref, vp_ref, bias_ref, o_ref):
    blk = SWA_BLOCK
    first = pl.program_id(1) == 0
    swap = lambda x: jnp.concatenate([x[:, D_HD:2 * D_HD], x[:, 0:D_HD]], axis=1)
    ones = jnp.ones((blk * (SWA_QB + 1), 128), BF16)
    kk_all = jnp.concatenate([kp_ref[...], kc_ref[...]], axis=0)
    vv_all = jnp.concatenate([vp_ref[...], vc_ref[...]], axis=0)
    kk_var = (kk_all, swap(kk_all))
    vv_var = (jnp.concatenate([vv_all, ones], axis=1), jnp.concatenate([swap(vv_all), ones], axis=1))
    q_all = q_ref[...] * jnp.asarray(D_HD ** -0.5, BF16)
    q_lane = lax.broadcasted_iota(jnp.int32, (2 * blk, 128), 1)
    o_lane = lax.broadcasted_iota(jnp.int32, (blk, 128), 1)
    for qb in range(SWA_QB):
        keys = slice(qb * blk, (qb + 2) * blk)
        for kv in range(D_KV):
            res = []
            for half in range(2):
                q2 = jnp.concatenate([q_all[qb * blk:(qb + 1) * blk, (2 * kv + tt) * 128:(2 * kv + tt + 1) * 128]
                                      for tt in range(2)], axis=0)
                q2 = jnp.where((q_lane < D_HD) if half == 0 else (q_lane >= D_HD), q2, jnp.zeros_like(q2))
                var = half ^ kv
                s = lax.dot_general(q2, kk_var[var][keys], (((1,), (1,)), ((), ())), preferred_element_type=F32)
                bias = bias_ref[1, kv, half]
                if qb == 0:
                    bias = jnp.where(first, bias_ref[0, kv, half], bias)
                lg = s + bias
                es, sinks = [], []
                for tt in range(2):
                    sink = sink_ref[4 * kv + half + 2 * tt]
                    lg_h = lg[tt * blk:(tt + 1) * blk]
                    m = jnp.maximum(jnp.max(lg_h, axis=-1, keepdims=True), sink)
                    es.append(jnp.exp(lg_h - m))
                    sinks.append(jnp.exp(sink - m))
                pv = jnp.dot(jnp.concatenate(es, axis=0).astype(BF16), vv_var[var][keys],
                             preferred_element_type=F32)
                res.append([pv[tt * blk:(tt + 1) * blk, 0:128] / (pv[tt * blk:(tt + 1) * blk, 128:256] + sinks[tt])
                            for tt in range(2)])
            for tt in range(2):
                tile = jnp.where(o_lane < D_HD, res[0][tt], res[1][tt])
                o_ref[qb * blk:(qb + 1) * blk, (2 * kv + tt) * 128:(2 * kv + tt + 1) * 128] = tile.astype(BF16)


def _swa_bias():
    slopes = np.exp2(-8.0 * np.arange(1, D_HEADS + 1, dtype=np.float32) / D_HEADS).astype(np.float32)
    qi = np.arange(SWA_BLOCK)
    kj = np.arange(2 * SWA_BLOCK) - SWA_BLOCK
    dist = (qi[:, None] - kj[None, :]).astype(np.float32)
    valid = (dist >= 0) & (dist < WINDOW)
    general = np.where(valid[None], -(slopes[:, None, None] * dist[None]), np.float32(NEG_INF))
    first = np.where((kj >= 0)[None, None, :], general, np.float32(NEG_INF))
    per_head = np.stack([first, general]).astype(np.float32)
    table = np.stack([np.stack([np.concatenate([per_head[:, 4 * kv + half], per_head[:, 4 * kv + half + 2]], axis=1)
                                for half in range(2)], axis=1) for kv in range(D_KV)], axis=1)
    return jnp.asarray(table)


def _swa(z, sinks, batch, seq):
    rows = SWA_QB * SWA_BLOCK
    ns = seq // rows
    qc, kc, vc = DQ0 // BRANCH_W, DK0 // 128, DV0 // 128
    cur = lambda b, n: b * ns + n
    prev = lambda b, n: jnp.maximum((b * ns + n) * SWA_QB - 1, 0)
    return pl.pallas_call(
        _swa_kernel,
        grid=(batch, ns),
        in_specs=[
            pl.BlockSpec(memory_space=pltpu.SMEM),
            pl.BlockSpec((rows, BRANCH_W), lambda b, n: (cur(b, n), qc)),
            pl.BlockSpec((rows, 128), lambda b, n: (cur(b, n), kc)),
            pl.BlockSpec((SWA_BLOCK, 128), lambda b, n: (prev(b, n), kc)),
            pl.BlockSpec((rows, 128), lambda b, n: (cur(b, n), vc)),
            pl.BlockSpec((SWA_BLOCK, 128), lambda b, n: (prev(b, n), vc)),
            pl.BlockSpec((2, D_KV, 2, 2 * SWA_BLOCK, 2 * SWA_BLOCK), lambda b, n: (0, 0, 0, 0, 0)),
        ],
        out_specs=pl.BlockSpec((rows, BRANCH_W), lambda b, n: (cur(b, n), 0)),
        out_shape=jax.ShapeDtypeStruct((batch * seq, BRANCH_W), BF16),
        compiler_params=_cparams(("parallel", "arbitrary")),
        name="swa",
    )(sinks, z, z, z, z, z, _swa_bias())


def _lru_kernel(x_ref, g_ref, cw_ref, cb_ref, wa_ref, ba_ref, wx_ref, bx_ref, lam_ref, o_ref,
                xpad_sc, apad_sc, upad_sc, hc_sc):
    ts, pad = LRU_TS, LRU_PAD

    @pl.when(pl.program_id(1) == 0)
    def _():
        xpad_sc[0:8, :] = jnp.zeros((8, BRANCH_W), F32)
        hc_sc[...] = jnp.zeros(hc_sc.shape, F32)
        apad_sc[0:pad, :] = jnp.ones((pad, BRANCH_W), F32)
        upad_sc[0:pad, :] = jnp.zeros((pad, BRANCH_W), F32)

    x = x_ref[...].astype(F32)
    xpad_sc[8:8 + ts, :] = x
    cw = cw_ref[...]
    xc = (cw[3:4] * x + cw[2:3] * xpad_sc[7:7 + ts, :] + cw[1:2] * xpad_sc[6:6 + ts, :]
          + cw[0:1] * xpad_sc[5:5 + ts, :] + cb_ref[...])
    xpad_sc[0:8, :] = x[ts - 8:ts]

    xcb = xc.astype(BF16)
    r = jax.nn.sigmoid(jnp.dot(xcb, wa_ref[...], preferred_element_type=F32) + ba_ref[...])
    gi = jax.nn.sigmoid(jnp.dot(xcb, wx_ref[...], preferred_element_type=F32) + bx_ref[...])
    log_a = (-B_C * r) * _softplus(-lam_ref[...])
    a = jnp.exp(log_a)
    u = jnp.sqrt(1.0 - jnp.exp(2.0 * log_a)) * (gi * xc)

    sh = 1
    while sh < ts:
        apad_sc[pad:pad + ts, :] = a
        upad_sc[pad:pad + ts, :] = u
        u = a * upad_sc[pad - sh:pad - sh + ts, :] + u
        a = a * apad_sc[pad - sh:pad - sh + ts, :]
        sh *= 2
    h = a * hc_sc[...] + u
    hc_sc[...] = h[ts - 1:ts]
    o_ref[...] = (jax.nn.gelu(g_ref[...].astype(F32)) * h).astype(BF16)


def _lru(z, cw, cb, wa, ba, wx, bx, lam, l, batch, seq):
    ns = seq // LRU_TS
    xc_, gc_ = BX0 // BRANCH_W, BG0 // BRANCH_W
    const = lambda shape: pl.BlockSpec(shape, lambda b, s: (0, 0))
    return pl.pallas_call(
        _lru_kernel,
        grid=(batch, ns),
        in_specs=[
            pl.BlockSpec((LRU_TS, BRANCH_W), lambda b, s: (b * ns + s, xc_)),
            pl.BlockSpec((LRU_TS, BRANCH_W), lambda b, s: (b * ns + s, gc_)),
            const((B_CONV, BRANCH_W)), const((1, BRANCH_W)),
            pl.BlockSpec((None, BRANCH_W, BRANCH_W), lambda b, s: (l, 0, 0)), const((1, BRANCH_W)),
            pl.BlockSpec((None, BRANCH_W, BRANCH_W), lambda b, s: (l, 0, 0)), const((1, BRANCH_W)),
            const((1, BRANCH_W)),
        ],
        out_specs=pl.BlockSpec((LRU_TS, BRANCH_W), lambda b, s: (b * ns + s, 0)),
        out_shape=jax.ShapeDtypeStruct((batch * seq, BRANCH_W), BF16),
        scratch_shapes=[
            pltpu.VMEM((LRU_TS + 8, BRANCH_W), F32),
            pltpu.VMEM((LRU_TS + LRU_PAD, BRANCH_W), F32),
            pltpu.VMEM((LRU_TS + LRU_PAD, BRANCH_W), F32),
            pltpu.VMEM((1, BRANCH_W), F32),
        ],
        compiler_params=_cparams(("parallel", "arbitrary")),
        name="lru",
    )(z, z, cw, cb, wa, ba, wx, bx, lam)


def _merge_kernel(h_ref, ya_ref, yb_ref, yc_ref, yd_ref, zg_ref, bg_ref, wb_ref, wo_ref, gn_ref, o_ref, v_ref):
    ys = (ya_ref, yb_ref, yc_ref, yd_ref)
    mixed = None
    for n in range(N_BRANCH):
        proj = jnp.dot(ys[n][...], wb_ref[n], preferred_element_type=F32)
        gate = zg_ref[:, n * D_MODEL:(n + 1) * D_MODEL].astype(F32) + bg_ref[n:n + 1, :]
        term = jax.nn.sigmoid(gate) * proj
        mixed = term if mixed is None else mixed + term
    o = h_ref[...] + jnp.dot(mixed.astype(BF16), wo_ref[...], preferred_element_type=F32)
    o_ref[...] = o
    v_ref[...] = _rms(o, gn_ref[...]).astype(BF16)


def _merge(h, ya, yb, yc, yd, zg, bg, wb, wo, gn, l):
    t = h.shape[0]
    row = lambda shape: pl.BlockSpec(shape, lambda i: (i, 0))
    return pl.pallas_call(
        _merge_kernel,
        grid=(t // MERGE_TM,),
        in_specs=[
            row((MERGE_TM, D_MODEL)),
            row((MERGE_TM, BRANCH_W)), row((MERGE_TM, BRANCH_W)), row((MERGE_TM, BRANCH_W)),
            row((MERGE_TM, BRANCH_W)),
            row((MERGE_TM, ZG_W)),
            pl.BlockSpec((N_BRANCH, D_MODEL), lambda i: (0, 0)),
            pl.BlockSpec((None, N_BRANCH, BRANCH_W, D_MODEL), lambda i: (l, 0, 0, 0), pipeline_mode=pl.Buffered(1)),
            pl.BlockSpec((None, D_MODEL, D_MODEL), lambda i: (l, 0, 0), pipeline_mode=pl.Buffered(1)),
            pl.BlockSpec((1, D_MODEL), lambda i: (0, 0)),
        ],
        out_specs=[row((MERGE_TM, D_MODEL)), row((MERGE_TM, D_MODEL))],
        out_shape=[jax.ShapeDtypeStruct((t, D_MODEL), F32), jax.ShapeDtypeStruct((t, D_MODEL), BF16)],
        compiler_params=_cparams(("parallel",)),
        name="merge",
    )(h, ya, yb, yc, yd, zg, bg, wb, wo, gn)


def _ffn_kernel(v_ref, h_ref, wg_ref, wu_ref, wd_ref, gn_ref, o_ref, *u_ref, final_norm):
    j = pl.program_id(1)

    @pl.when(j == 0)
    def _():
        o_ref[...] = h_ref[...]

    v = v_ref[...]
    a = jnp.dot(v, wg_ref[...], preferred_element_type=F32)
    b = jnp.dot(v, wu_ref[...], preferred_element_type=F32)
    t = (a * jax.nn.sigmoid(a)) * b
    o_ref[...] += jnp.dot(t.astype(BF16), wd_ref[...], preferred_element_type=F32)

    @pl.when(j == pl.num_programs(1) - 1)
    def _():
        y = _rms(o_ref[...], gn_ref[...])
        if final_norm:
            o_ref[...] = y
        else:
            u_ref[0][...] = y.astype(BF16)


def _ffn(v, h, wg, wu, wd, gn, l, final_norm):
    t = h.shape[0]
    row = lambda: pl.BlockSpec((FFN_TM, D_MODEL), lambda i, j: (i, 0))
    out_specs = [row()] if final_norm else [row(), row()]
    out_shape = [jax.ShapeDtypeStruct((t, D_MODEL), F32)]
    if not final_norm:
        out_shape.append(jax.ShapeDtypeStruct((t, D_MODEL), BF16))
    return pl.pallas_call(
        functools.partial(_ffn_kernel, final_norm=final_norm),
        grid=(t // FFN_TM, D_FF // FFN_TF),
        in_specs=[
            row(), row(),
            pl.BlockSpec((None, D_MODEL, FFN_TF), lambda i, j: (l, 0, j)),
            pl.BlockSpec((None, D_MODEL, FFN_TF), lambda i, j: (l, 0, j)),
            pl.BlockSpec((None, FFN_TF, D_MODEL), lambda i, j: (l, j, 0)),
            pl.BlockSpec((1, D_MODEL), lambda i, j: (0, 0)),
        ],
        out_specs=out_specs,
        out_shape=out_shape,
        compiler_params=_cparams(("parallel", "arbitrary")),
        name="ffn_final" if final_norm else "ffn",
    )(v, h, wg, wu, wd, gn)


def _block_diag(w):
    depth, nb, c, d = w.shape
    eye = jnp.eye(nb, dtype=w.dtype)
    return jnp.einsum('lncd,nm->lncmd', w, eye).reshape(depth, nb * c, nb * d)


def kernel(x, norm_mix, w_in, b_gate, diff_lq1, diff_lk1, diff_lq2, diff_lk2, diff_subln, lru_conv_w, lru_conv_b,
           lru_wa, lru_ba, lru_wx, lru_bx, lru_lambda, fox_b_f, swa_sinks, w_branch, w_out, norm_ffn, w_ffn_gate,
           w_ffn_up, w_ffn_down, norm_final):
    batch, seq, d = x.shape
    depth = w_in.shape[0]
    assert d == D_MODEL and seq % max(A_TQ, C_TQ, LRU_TS, SWA_QB * SWA_BLOCK) == 0 and (batch * seq) % IN_TM == 0
    t = batch * seq

    w_bf = w_in.astype(BF16)
    w_gz = w_bf[:, :, _R_GZ0:]
    w_d = w_bf[:, :, _R_DQ0:_R_GZ0]
    w_cf = jnp.pad(w_bf[:, :, _R_CF0:_R_DQ0], ((0, 0), (0, 0), (0, CF_PAD - C_HEADS)))
    wb = w_branch.astype(BF16)
    wo = w_out.astype(BF16)
    wg = w_ffn_gate.astype(BF16)
    wu = w_ffn_up.astype(BF16)
    wd = w_ffn_down.astype(BF16)
    wa = _block_diag(lru_wa).astype(BF16)
    wx = _block_diag(lru_wx).astype(BF16)
    lvec = jnp.stack([diff_lq1, diff_lk1, diff_lq2, diff_lk2], axis=1).astype(F32)
    bg = b_gate.reshape(depth, N_BRANCH, D_MODEL)
    gf = norm_final.reshape(1, D_MODEL)

    h = x.reshape(t, D_MODEL)
    u = _norm(h, norm_mix[0].reshape(1, D_MODEL))
    for l in range(depth):
        lam_init = 0.8 - 0.6 * math.exp(-0.3 * l)
        zg = _proj(u, w_gz, l, ZG_W, "proj_gate")
        zm = _proj(u, w_bf, l, ZM_W, "proj_mix")
        zd, cf = _proj_d(u, w_d, w_cf, l)

        ya = _diff_attn(zm, lvec[l], diff_subln[l].reshape(1, 2 * A_HD),
                        jnp.asarray([lam_init, 1.0 - lam_init], F32), batch, seq)
        yb = _lru(zm, lru_conv_w[l], lru_conv_b[l].reshape(1, -1), wa, lru_ba[l].reshape(1, -1), wx,
                  lru_bx[l].reshape(1, -1), lru_lambda[l].reshape(1, -1), l, batch, seq)
        cf_t = cf[:, :C_HEADS].reshape(batch, seq, C_HEADS).transpose(0, 2, 1).reshape(batch * C_HEADS, seq)
        bf_t = jnp.tile(fox_b_f[l], batch).reshape(batch * C_HEADS, 1)
        cum = _fox_cum(cf_t, bf_t).reshape(batch * C_HEADS, 1, seq)
        yc = _fox_attn(zm, cum, batch, seq)
        yd = _swa(zd, swa_sinks[l], batch, seq)

        h, v = _merge(h, ya, yb, yc, yd, zg, bg[l], wb, wo, norm_ffn[l].reshape(1, D_MODEL), l)
        if l == depth - 1:
            (h,) = _ffn(v, h, wg, wu, wd, gf, l, final_norm=True)
        else:
            h, u = _ffn(v, h, wg, wu, wd, norm_mix[l + 1].reshape(1, D_MODEL), l, final_norm=False)
    return h.reshape(batch, seq, D_MODEL)
```

```python
import functools
import math

import jax
import jax.numpy as jnp
import numpy as np
from jax import lax
from jax.experimental import pallas as pl
from jax.experimental.pallas import tpu as pltpu

F32 = jnp.float32
BF16 = jnp.bfloat16

D_MODEL = 2048
N_BRANCH = 4
BRANCH_W = D_MODEL // 4
A_HEADS = 4
A_HD = BRANCH_W // (2 * A_HEADS)
B_BLOCKS = 8
B_CONV = 4
B_C = 8.0
C_HEADS = 4
C_HD = BRANCH_W // C_HEADS
D_HEADS = 8
D_KV = 2
D_GROUP = D_HEADS // D_KV
D_HD = BRANCH_W // D_HEADS
WINDOW = 128
SWA_BLOCK = 128
SWA_QB = 4
D_FF = -(-8 * D_MODEL // (3 * 256)) * 256
RMS_EPS = 1e-6
NEG_INF = -1e30

AQ0 = 0
AK0 = AQ0 + BRANCH_W
AV0 = AK0 + BRANCH_W
BX0 = AV0 + BRANCH_W
BG0 = BX0 + BRANCH_W
CQ0 = BG0 + BRANCH_W
CK0 = CQ0 + BRANCH_W
CV0 = CK0 + BRANCH_W
ZM_W = CV0 + BRANCH_W
DQ0 = 0
DK0 = DQ0 + BRANCH_W
DV0 = DK0 + D_KV * D_HD
ZD_W = DV0 + D_KV * D_HD
ZG_W = N_BRANCH * D_MODEL
CF_PAD = 128

_R_CF0 = ZM_W
_R_DQ0 = _R_CF0 + C_HEADS
_R_GZ0 = _R_DQ0 + ZD_W

V7X_VMEM_LIMIT = 56 * 1024 * 1024

CAST_ROWS = 256
NORM_TM = 1024
IN_TM, IN_TN = 2048, 1024
FLASH_ROWS = 256
A_TQ, A_HPS = 1024, 1
C_TQ, C_HPS = 1024, 2
LOG2E = math.log2(math.e)
LRU_TS = 512
LRU_PAD = LRU_TS // 2
MERGE_TM = 256
FFN_TM, FFN_TF = 512, 512


def _cparams(sem):
    return pltpu.CompilerParams(dimension_semantics=sem, vmem_limit_bytes=V7X_VMEM_LIMIT)


def _rms(x, g):
    return x * lax.rsqrt(jnp.mean(x * x, axis=-1, keepdims=True) + RMS_EPS) * g


def _softplus(y):
    return jnp.maximum(y, 0.0) + jnp.log1p(jnp.exp(-jnp.abs(y)))


def _cast_kernel(w_ref, o_ref):
    o_ref[...] = w_ref[...].astype(BF16)


def _cast_bf16(w, rows):
    depth, r, c = w.shape
    spec = pl.BlockSpec((None, rows, c), lambda l, i: (l, i, 0))
    return pl.pallas_call(
        _cast_kernel,
        grid=(depth, r // rows),
        in_specs=[spec],
        out_specs=spec,
        out_shape=jax.ShapeDtypeStruct(w.shape, BF16),
        compiler_params=_cparams(("parallel", "parallel")),
        name="cast_bf16",
    )(w)


def _norm_kernel(x_ref, g_ref, o_ref):
    o_ref[...] = _rms(x_ref[...], g_ref[...]).astype(BF16)


def _norm(x, g):
    t = x.shape[0]
    return pl.pallas_call(
        _norm_kernel,
        grid=(t // NORM_TM,),
        in_specs=[pl.BlockSpec((NORM_TM, D_MODEL), lambda i: (i, 0)), pl.BlockSpec((1, D_MODEL), lambda i: (0, 0))],
        out_specs=pl.BlockSpec((NORM_TM, D_MODEL), lambda i: (i, 0)),
        out_shape=jax.ShapeDtypeStruct((t, D_MODEL), BF16),
        compiler_params=_cparams(("parallel",)),
        name="norm0",
    )(x, g)


def _proj_kernel(u_ref, w_ref, z_ref):
    z_ref[...] = jnp.dot(u_ref[...], w_ref[...], preferred_element_type=F32).astype(BF16)


def _proj(u, w, l, n_cols, name):
    t = u.shape[0]
    return pl.pallas_call(
        _proj_kernel,
        grid=(t // IN_TM, n_cols // IN_TN),
        in_specs=[
            pl.BlockSpec((IN_TM, D_MODEL), lambda i, j: (i, 0)),
            pl.BlockSpec((None, D_MODEL, IN_TN), lambda i, j: (l, 0, j)),
        ],
        out_specs=pl.BlockSpec((IN_TM, IN_TN), lambda i, j: (i, j)),
        out_shape=jax.ShapeDtypeStruct((t, n_cols), BF16),
        compiler_params=_cparams(("parallel", "arbitrary")),
        name=name,
    )(u, w)


def _proj_d_kernel(u_ref, w_ref, wcf_ref, z_ref, cf_ref):
    u = u_ref[...]
    z_ref[...] = jnp.dot(u, w_ref[...], preferred_element_type=F32).astype(BF16)
    cf_ref[...] = jnp.dot(u, wcf_ref[...], preferred_element_type=F32)


def _proj_d(u, w, wcf, l):
    t = u.shape[0]
    return pl.pallas_call(
        _proj_d_kernel,
        grid=(t // IN_TM,),
        in_specs=[
            pl.BlockSpec((IN_TM, D_MODEL), lambda i: (i, 0)),
            pl.BlockSpec((None, D_MODEL, ZD_W), lambda i: (l, 0, 0)),
            pl.BlockSpec((None, D_MODEL, CF_PAD), lambda i: (l, 0, 0)),
        ],
        out_specs=[
            pl.BlockSpec((IN_TM, ZD_W), lambda i: (i, 0)),
            pl.BlockSpec((IN_TM, CF_PAD), lambda i: (i, 0)),
        ],
        out_shape=[jax.ShapeDtypeStruct((t, ZD_W), BF16), jax.ShapeDtypeStruct((t, CF_PAD), F32)],
        compiler_params=_cparams(("parallel",)),
        name="proj_d",
    )(u, w, wcf)


def _tri_tables(n):
    qi = [i for i in range(n) for _ in range(i + 1)]
    kj = [j for i in range(n) for j in range(i + 1)]
    return jnp.asarray(qi, jnp.int32), jnp.asarray(kj, jnp.int32)


def _flash_rows(q_ref, k_ref, v2_sc, bias, m_sc, acc_sc, r0, diag_q0, scale):
    rows = FLASH_ROWS
    nk = k_ref.shape[0] if diag_q0 is None else diag_q0 + rows
    s = lax.dot_general(q_ref[r0:r0 + rows, :], k_ref[0:nk, :], (((1,), (1,)), ((), ())),
                        preferred_element_type=F32)
    s = s * scale + bias[:, 0:nk]
    if diag_q0 is not None:
        qpos = lax.broadcasted_iota(jnp.int32, s.shape, 0) + diag_q0
        kpos = lax.broadcasted_iota(jnp.int32, s.shape, 1)
        s = jnp.where(kpos <= qpos, s, NEG_INF)
    m_prev = m_sc[r0:r0 + rows, :]
    m_new = jnp.maximum(m_prev, jnp.max(s, axis=-1, keepdims=True))
    alpha = jnp.exp2(m_prev - m_new)
    p = jnp.exp2(s - jnp.concatenate([m_new] * (nk // 128), axis=1))
    pv = jnp.dot(p.astype(BF16), v2_sc[0:nk, :], preferred_element_type=F32)
    acc_sc[r0:r0 + rows, :] = jnp.concatenate([alpha, alpha], axis=1) * acc_sc[r0:r0 + rows, :] + pv
    m_sc[r0:r0 + rows, :] = m_new


def _flash_block(q_ref, k_ref, v2_sc, bias, m_sc, acc_sc, n_rows, tq, diag, scale):
    for r0 in range(0, n_rows, FLASH_ROWS):
        _flash_rows(q_ref, k_ref, v2_sc, bias, m_sc, acc_sc, r0, (r0 % tq) if diag else None, scale)


def _flash_init(v2_sc, m_sc, acc_sc):
    v2_sc[:, 128:256] = jnp.ones((v2_sc.shape[0], 128), BF16)
    m_sc[...] = jnp.full(m_sc.shape, -jnp.inf, F32)
    acc_sc[...] = jnp.zeros(acc_sc.shape, F32)


def _diff_attn_kernel(qi_ref, kj_ref, slope_ref, lami_ref, q_ref, k_ref, v_ref, lv_ref, sg_ref, o_ref,
                      q2_sc, v2_sc, m_sc, acc_sc):
    tq = A_TQ
    t = pl.program_id(2)
    i = qi_ref[t]
    j = kj_ref[t]
    heads = [(hh, slice(hh * 128, (hh + 1) * 128)) for hh in range(A_HPS)]

    @pl.when(j == 0)
    def _():
        for hh, cols in heads:
            q = q_ref[:, cols]
            lane = lax.broadcasted_iota(jnp.int32, q.shape, 1)
            qs = q * jnp.asarray(A_HD ** -0.5, BF16)
            zero = jnp.zeros_like(qs)
            q2_sc[hh, 0:tq, :] = jnp.where(lane < A_HD, qs, zero)
            q2_sc[hh, tq:2 * tq, :] = jnp.where(lane >= A_HD, qs, zero)
            _flash_init(v2_sc.at[hh], m_sc.at[hh], acc_sc.at[hh])

    kpos = (lax.broadcasted_iota(jnp.int32, (1, tq), 1) + (j - i) * tq).astype(F32)

    def block(diag):
        for hh, cols in heads:
            v2_sc[hh, :, 0:128] = v_ref[:, cols]
            bias = kpos * (slope_ref[pl.program_id(1) * A_HPS + hh] * LOG2E)
            _flash_block(q2_sc.at[hh], k_ref.at[:, cols], v2_sc.at[hh], bias, m_sc.at[hh], acc_sc.at[hh],
                         2 * tq, tq, diag, LOG2E)

    @pl.when(j < i)
    def _():
        block(False)

    @pl.when(j == i)
    def _():
        block(True)
        lv = lv_ref[...]
        lam = (jnp.exp(jnp.sum(lv[0:1] * lv[1:2], axis=-1, keepdims=True))
               - jnp.exp(jnp.sum(lv[2:3] * lv[3:4], axis=-1, keepdims=True)) + lami_ref[0])
        for hh, cols in heads:
            acc = acc_sc[hh]
            o = acc[0:tq, 0:128] / acc[0:tq, 128:256] - lam * (acc[tq:2 * tq, 0:128] / acc[tq:2 * tq, 128:256])
            o_ref[:, cols] = (_rms(o, sg_ref[...]) * lami_ref[1]).astype(BF16)


def _diff_attn(z, lvec, subln, lam_init, batch, seq):
    nq = seq // A_TQ
    qi, kj = _tri_tables(nq)
    slopes = jnp.asarray(np.exp2(-8.0 * np.arange(1, A_HEADS + 1, dtype=np.float32) / A_HEADS), F32)
    w = 128 * A_HPS
    qc, kc, vc = AQ0 // w, AK0 // w, AV0 // w
    smem = pl.BlockSpec(memory_space=pltpu.SMEM)
    grid_spec = pltpu.PrefetchScalarGridSpec(
        num_scalar_prefetch=2,
        grid=(batch, A_HEADS // A_HPS, int(qi.shape[0])),
        in_specs=[
            smem, smem,
            pl.BlockSpec((A_TQ, w), lambda b, h, t, qi, kj: (b * nq + qi[t], qc + h)),
            pl.BlockSpec((A_TQ, w), lambda b, h, t, qi, kj: (b * nq + kj[t], kc + h)),
            pl.BlockSpec((A_TQ, w), lambda b, h, t, qi, kj: (b * nq + kj[t], vc + h)),
            pl.BlockSpec((4, A_HD), lambda b, h, t, qi, kj: (0, 0)),
            pl.BlockSpec((1, 2 * A_HD), lambda b, h, t, qi, kj: (0, 0)),
        ],
        out_specs=pl.BlockSpec((A_TQ, w), lambda b, h, t, qi, kj: (b * nq + qi[t], h)),
        scratch_shapes=[
            pltpu.VMEM((A_HPS, 2 * A_TQ, 128), BF16),
            pltpu.VMEM((A_HPS, A_TQ, 256), BF16),
            pltpu.VMEM((A_HPS, 2 * A_TQ, 128), F32),
            pltpu.VMEM((A_HPS, 2 * A_TQ, 256), F32),
        ],
    )
    return pl.pallas_call(
        _diff_attn_kernel,
        grid_spec=grid_spec,
        out_shape=jax.ShapeDtypeStruct((batch * seq, BRANCH_W), BF16),
        compiler_params=_cparams(("parallel", "parallel", "arbitrary")),
        name="diff_attn",
    )(qi, kj, slopes, lam_init, z, z, z, lvec, subln)


def _fox_cum_kernel(cf_ref, bf_ref, o_ref):
    x = cf_ref[...] + bf_ref[...]
    logf = jnp.minimum(x, 0.0) - jnp.log1p(jnp.exp(-jnp.abs(x)))
    rows, seq = logf.shape
    lane = lax.broadcasted_iota(jnp.int32, (rows, 128), 1)
    carry = jnp.zeros((rows, 1), F32)
    for c in range(seq // 128):
        y = logf[:, c * 128:(c + 1) * 128]
        sh = 1
        while sh < 128:
            y = y + jnp.where(lane >= sh, pltpu.roll(y, sh, axis=1), 0.0)
            sh *= 2
        y = y + carry
        o_ref[:, c * 128:(c + 1) * 128] = y
        carry = y[:, 127:128]


def _fox_cum(cf_t, bf_t):
    rows, seq = cf_t.shape
    return pl.pallas_call(
        _fox_cum_kernel,
        grid=(1,),
        in_specs=[pl.BlockSpec((rows, seq), lambda i: (0, 0)), pl.BlockSpec((rows, 1), lambda i: (0, 0))],
        out_specs=pl.BlockSpec((rows, seq), lambda i: (0, 0)),
        out_shape=jax.ShapeDtypeStruct((rows, seq), F32),
        compiler_params=_cparams(("arbitrary",)),
        name="fox_cum",
    )(cf_t, bf_t)


def _fox_attn_kernel(qi_ref, kj_ref, q_ref, k_ref, v_ref, cum_ref, o_ref, v2_sc, m_sc, acc_sc):
    tq = C_TQ
    t = pl.program_id(2)
    i = qi_ref[t]
    j = kj_ref[t]
    heads = [(hh, slice(hh * 128, (hh + 1) * 128)) for hh in range(C_HPS)]

    @pl.when(j == 0)
    def _():
        for hh, _ in heads:
            _flash_init(v2_sc.at[hh], m_sc.at[hh], acc_sc.at[hh])

    def block(diag):
        for hh, cols in heads:
            v2_sc[hh, :, 0:128] = v_ref[:, cols]
            _flash_block(q_ref.at[:, cols], k_ref.at[:, cols], v2_sc.at[hh], cum_ref[hh] * (-LOG2E), m_sc.at[hh],
                         acc_sc.at[hh], tq, tq, diag, C_HD ** -0.5 * LOG2E)

    @pl.when(j < i)
    def _():
        block(False)

    @pl.when(j == i)
    def _():
        block(True)
        for hh, cols in heads:
            acc = acc_sc[hh]
            o_ref[:, cols] = (acc[:, 0:128] / acc[:, 128:256]).astype(BF16)


def _fox_attn(z, cum, batch, seq):
    nq = seq // C_TQ
    qi, kj = _tri_tables(nq)
    w = 128 * C_HPS
    qc, kc, vc = CQ0 // w, CK0 // w, CV0 // w
    hg = C_HEADS // C_HPS
    grid_spec = pltpu.PrefetchScalarGridSpec(
        num_scalar_prefetch=2,
        grid=(batch, hg, int(qi.shape[0])),
        in_specs=[
            pl.BlockSpec((C_TQ, w), lambda b, h, t, qi, kj: (b * nq + qi[t], qc + h)),
            pl.BlockSpec((C_TQ, w), lambda b, h, t, qi, kj: (b * nq + kj[t], kc + h)),
            pl.BlockSpec((C_TQ, w), lambda b, h, t, qi, kj: (b * nq + kj[t], vc + h)),
            pl.BlockSpec((C_HPS, 1, C_TQ), lambda b, h, t, qi, kj: (b * hg + h, 0, kj[t])),
        ],
        out_specs=pl.BlockSpec((C_TQ, w), lambda b, h, t, qi, kj: (b * nq + qi[t], h)),
        scratch_shapes=[
            pltpu.VMEM((C_HPS, C_TQ, 256), BF16),
            pltpu.VMEM((C_HPS, C_TQ, 128), F32),
            pltpu.VMEM((C_HPS, C_TQ, 256), F32),
        ],
    )
    return pl.pallas_call(
        _fox_attn_kernel,
        grid_spec=grid_spec,
        out_shape=jax.ShapeDtypeStruct((batch * seq, BRANCH_W), BF16),
        compiler_params=_cparams(("parallel", "parallel", "arbitrary")),
        name="fox_attn",
    )(qi, kj, z, z, z, cum)


def _swa_kernel(sink_ref, q_ref, kc_ref, kp_ref, vc_ref, vp_ref, bias_ref, o_ref):
    blk = SWA_BLOCK
    first = pl.program_id(1) == 0
    swap = lambda x: jnp.concatenate([x[:, D_HD:2 * D_HD], x[:, 0:D_HD]], axis=1)
    ones = jnp.ones((blk * (SWA_QB + 1), 128), BF16)
    kk_all = jnp.concatenate([kp_ref[...], kc_ref[...]], axis=0)
    vv_all = jnp.concatenate([vp_ref[...], vc_ref[...]], axis=0)
    kk_var = (kk_all, swap(kk_all))
    vv_var = (jnp.concatenate([vv_all, ones], axis=1), jnp.concatenate([swap(vv_all), ones], axis=1))
    q_all = q_ref[...] * jnp.asarray(D_HD ** -0.5, BF16)
    q_lane = lax.broadcasted_iota(jnp.int32, (2 * blk, 128), 1)
    o_lane = lax.broadcasted_iota(jnp.int32, (blk, 128), 1)
    for qb in range(SWA_QB):
        keys = slice(qb * blk, (qb + 2) * blk)
        for kv in range(D_KV):
            res = []
            for half in range(2):
                q2 = jnp.concatenate([q_all[qb * blk:(qb + 1) * blk, (2 * kv + tt) * 128:(2 * kv + tt + 1) * 128]
                                      for tt in range(2)], axis=0)
                q2 = jnp.where((q_lane < D_HD) if half == 0 else (q_lane >= D_HD), q2, jnp.zeros_like(q2))
                var = half ^ kv
                s = lax.dot_general(q2, kk_var[var][keys], (((1,), (1,)), ((), ())), preferred_element_type=F32)
                bias = bias_ref[1, kv, half]
                if qb == 0:
                    bias = jnp.where(first, bias_ref[0, kv, half], bias)
                lg = s + bias
                es, sinks = [], []
                for tt in range(2):
                    sink = sink_ref[4 * kv + half + 2 * tt]
                    lg_h = lg[tt * blk:(tt + 1) * blk]
                    m = jnp.maximum(jnp.max(lg_h, axis=-1, keepdims=True), sink)
                    es.append(jnp.exp(lg_h - m))
                    sinks.append(jnp.exp(sink - m))
                pv = jnp.dot(jnp.concatenate(es, axis=0).astype(BF16), vv_var[var][keys],
                             preferred_element_type=F32)
                res.append([pv[tt * blk:(tt + 1) * blk, 0:128] / (pv[tt * blk:(tt + 1) * blk, 128:256] + sinks[tt])
                            for tt in range(2)])
            for tt in range(2):
                tile = jnp.where(o_lane < D_HD, res[0][tt], res[1][tt])
                o_ref[qb * blk:(qb + 1) * blk, (2 * kv + tt) * 128:(2 * kv + tt + 1) * 128] = tile.astype(BF16)


def _swa_bias():
    slopes = np.exp2(-8.0 * np.arange(1, D_HEADS + 1, dtype=np.float32) / D_HEADS).astype(np.float32)
    qi = np.arange(SWA_BLOCK)
    kj = np.arange(2 * SWA_BLOCK) - SWA_BLOCK
    dist = (qi[:, None] - kj[None, :]).astype(np.float32)
    valid = (dist >= 0) & (dist < WINDOW)
    general = np.where(valid[None], -(slopes[:, None, None] * dist[None]), np.float32(NEG_INF))
    first = np.where((kj >= 0)[None, None, :], general, np.float32(NEG_INF))
    per_head = np.stack([first, general]).astype(np.float32)
    table = np.stack([np.stack([np.concatenate([per_head[:, 4 * kv + half], per_head[:, 4 * kv + half + 2]], axis=1)
                                for half in range(2)], axis=1) for kv in range(D_KV)], axis=1)
    return jnp.asarray(table)


def _swa(z, sinks, batch, seq):
    rows = SWA_QB * SWA_BLOCK
    ns = seq // rows
    qc, kc, vc = DQ0 // BRANCH_W, DK0 // 128, DV0 // 128
    cur = lambda b, n: b * ns + n
    prev = lambda b, n: jnp.maximum((b * ns + n) * SWA_QB - 1, 0)
    return pl.pallas_call(
        _swa_kernel,
        grid=(batch, ns),
        in_specs=[
            pl.BlockSpec(memory_space=pltpu.SMEM),
            pl.BlockSpec((rows, BRANCH_W), lambda b, n: (cur(b, n), qc)),
            pl.BlockSpec((rows, 128), lambda b, n: (cur(b, n), kc)),
            pl.BlockSpec((SWA_BLOCK, 128), lambda b, n: (prev(b, n), kc)),
            pl.BlockSpec((rows, 128), lambda b, n: (cur(b, n), vc)),
            pl.BlockSpec((SWA_BLOCK, 128), lambda b, n: (prev(b, n), vc)),
            pl.BlockSpec((2, D_KV, 2, 2 * SWA_BLOCK, 2 * SWA_BLOCK), lambda b, n: (0, 0, 0, 0, 0)),
        ],
        out_specs=pl.BlockSpec((rows, BRANCH_W), lambda b, n: (cur(b, n), 0)),
        out_shape=jax.ShapeDtypeStruct((batch * seq, BRANCH_W), BF16),
        compiler_params=_cparams(("parallel", "arbitrary")),
        name="swa",
    )(sinks, z, z, z, z, z, _swa_bias())


def _lru_kernel(x_ref, g_ref, cw_ref, cb_ref, wa_ref, ba_ref, wx_ref, bx_ref, lam_ref, o_ref,
                xpad_sc, apad_sc, upad_sc, hc_sc):
    ts, pad = LRU_TS, LRU_PAD

    @pl.when(pl.program_id(1) == 0)
    def _():
        xpad_sc[0:8, :] = jnp.zeros((8, BRANCH_W), F32)
        hc_sc[...] = jnp.zeros(hc_sc.shape, F32)
        apad_sc[0:pad, :] = jnp.ones((pad, BRANCH_W), F32)
        upad_sc[0:pad, :] = jnp.zeros((pad, BRANCH_W), F32)

    x = x_ref[...].astype(F32)
    xpad_sc[8:8 + ts, :] = x
    cw = cw_ref[...]
    xc = (cw[3:4] * x + cw[2:3] * xpad_sc[7:7 + ts, :] + cw[1:2] * xpad_sc[6:6 + ts, :]
          + cw[0:1] * xpad_sc[5:5 + ts, :] + cb_ref[...])
    xpad_sc[0:8, :] = x[ts - 8:ts]

    xcb = xc.astype(BF16)
    r = jax.nn.sigmoid(jnp.dot(xcb, wa_ref[...], preferred_element_type=F32) + ba_ref[...])
    gi = jax.nn.sigmoid(jnp.dot(xcb, wx_ref[...], preferred_element_type=F32) + bx_ref[...])
    log_a = (-B_C * r) * _softplus(-lam_ref[...])
    a = jnp.exp(log_a)
    u = jnp.sqrt(1.0 - a * a) * (gi * xc)

    sh = 1
    while sh < ts:
        apad_sc[pad:pad + ts, :] = a
        upad_sc[pad:pad + ts, :] = u
        u = a * upad_sc[pad - sh:pad - sh + ts, :] + u
        a = a * apad_sc[pad - sh:pad - sh + ts, :]
        sh *= 2
    h = a * hc_sc[...] + u
    hc_sc[...] = h[ts - 1:ts]
    g = g_ref[...].astype(F32)
    gelu = g * jax.nn.sigmoid((2.0 * math.sqrt(2.0 / math.pi)) * (g + 0.044715 * (g * g * g)))
    o_ref[...] = (gelu * h).astype(BF16)


def _lru(z, cw, cb, wa, ba, wx, bx, lam, l, batch, seq):
    ns = seq // LRU_TS
    xc_, gc_ = BX0 // BRANCH_W, BG0 // BRANCH_W
    const = lambda shape: pl.BlockSpec(shape, lambda b, s: (0, 0))
    return pl.pallas_call(
        _lru_kernel,
        grid=(batch, ns),
        in_specs=[
            pl.BlockSpec((LRU_TS, BRANCH_W), lambda b, s: (b * ns + s, xc_)),
            pl.BlockSpec((LRU_TS, BRANCH_W), lambda b, s: (b * ns + s, gc_)),
            const((B_CONV, BRANCH_W)), const((1, BRANCH_W)),
            pl.BlockSpec((None, BRANCH_W, BRANCH_W), lambda b, s: (l, 0, 0)), const((1, BRANCH_W)),
            pl.BlockSpec((None, BRANCH_W, BRANCH_W), lambda b, s: (l, 0, 0)), const((1, BRANCH_W)),
            const((1, BRANCH_W)),
        ],
        out_specs=pl.BlockSpec((LRU_TS, BRANCH_W), lambda b, s: (b * ns + s, 0)),
        out_shape=jax.ShapeDtypeStruct((batch * seq, BRANCH_W), BF16),
        scratch_shapes=[
            pltpu.VMEM((LRU_TS + 8, BRANCH_W), F32),
            pltpu.VMEM((LRU_TS + LRU_PAD, BRANCH_W), F32),
            pltpu.VMEM((LRU_TS + LRU_PAD, BRANCH_W), F32),
            pltpu.VMEM((1, BRANCH_W), F32),
        ],
        compiler_params=_cparams(("parallel", "arbitrary")),
        name="lru",
    )(z, z, cw, cb, wa, ba, wx, bx, lam)


def _merge_kernel(h_ref, ya_ref, yb_ref, yc_ref, yd_ref, zg_ref, bg_ref, wb_ref, wo_ref, gn_ref, o_ref, v_ref):
    ys = (ya_ref, yb_ref, yc_ref, yd_ref)
    mixed = None
    for n in range(N_BRANCH):
        proj = jnp.dot(ys[n][...], wb_ref[n], preferred_element_type=F32)
        gate = zg_ref[:, n * D_MODEL:(n + 1) * D_MODEL].astype(F32) + bg_ref[n:n + 1, :]
        term = jax.nn.sigmoid(gate) * proj
        mixed = term if mixed is None else mixed + term
    o = h_ref[...] + jnp.dot(mixed.astype(BF16), wo_ref[...], preferred_element_type=F32)
    o_ref[...] = o
    v_ref[...] = _rms(o, gn_ref[...]).astype(BF16)


def _merge(h, ya, yb, yc, yd, zg, bg, wb, wo, gn, l):
    t = h.shape[0]
    row = lambda shape: pl.BlockSpec(shape, lambda i: (i, 0))
    return pl.pallas_call(
        _merge_kernel,
        grid=(t // MERGE_TM,),
        in_specs=[
            row((MERGE_TM, D_MODEL)),
            row((MERGE_TM, BRANCH_W)), row((MERGE_TM, BRANCH_W)), row((MERGE_TM, BRANCH_W)),
            row((MERGE_TM, BRANCH_W)),
            row((MERGE_TM, ZG_W)),
            pl.BlockSpec((N_BRANCH, D_MODEL), lambda i: (0, 0)),
            pl.BlockSpec((None, N_BRANCH, BRANCH_W, D_MODEL), lambda i: (l, 0, 0, 0), pipeline_mode=pl.Buffered(1)),
            pl.BlockSpec((None, D_MODEL, D_MODEL), lambda i: (l, 0, 0), pipeline_mode=pl.Buffered(1)),
            pl.BlockSpec((1, D_MODEL), lambda i: (0, 0)),
        ],
        out_specs=[row((MERGE_TM, D_MODEL)), row((MERGE_TM, D_MODEL))],
        out_shape=[jax.ShapeDtypeStruct((t, D_MODEL), F32), jax.ShapeDtypeStruct((t, D_MODEL), BF16)],
        compiler_params=_cparams(("parallel",)),
        name="merge",
    )(h, ya, yb, yc, yd, zg, bg, wb, wo, gn)


def _ffn_kernel(v_ref, h_ref, wg_ref, wu_ref, wd_ref, gn_ref, o_ref, *u_ref, final_norm):
    j = pl.program_id(1)

    @pl.when(j == 0)
    def _():
        o_ref[...] = h_ref[...]

    v = v_ref[...]
    a = jnp.dot(v, wg_ref[...], preferred_element_type=F32)
    b = jnp.dot(v, wu_ref[...], preferred_element_type=F32)
    t = (a * jax.nn.sigmoid(a)) * b
    o_ref[...] += jnp.dot(t.astype(BF16), wd_ref[...], preferred_element_type=F32)

    @pl.when(j == pl.num_programs(1) - 1)
    def _():
        y = _rms(o_ref[...], gn_ref[...])
        if final_norm:
            o_ref[...] = y
        else:
            u_ref[0][...] = y.astype(BF16)


def _ffn(v, h, wg, wu, wd, gn, l, final_norm):
    t = h.shape[0]
    row = lambda: pl.BlockSpec((FFN_TM, D_MODEL), lambda i, j: (i, 0))
    out_specs = [row()] if final_norm else [row(), row()]
    out_shape = [jax.ShapeDtypeStruct((t, D_MODEL), F32)]
    if not final_norm:
        out_shape.append(jax.ShapeDtypeStruct((t, D_MODEL), BF16))
    return pl.pallas_call(
        functools.partial(_ffn_kernel, final_norm=final_norm),
        grid=(t // FFN_TM, D_FF // FFN_TF),
        in_specs=[
            row(), row(),
            pl.BlockSpec((None, D_MODEL, FFN_TF), lambda i, j: (l, 0, j)),
            pl.BlockSpec((None, D_MODEL, FFN_TF), lambda i, j: (l, 0, j)),
            pl.BlockSpec((None, FFN_TF, D_MODEL), lambda i, j: (l, j, 0)),
            pl.BlockSpec((1, D_MODEL), lambda i, j: (0, 0)),
        ],
        out_specs=out_specs,
        out_shape=out_shape,
        compiler_params=_cparams(("parallel", "arbitrary")),
        name="ffn_final" if final_norm else "ffn",
    )(v, h, wg, wu, wd, gn)


def _block_diag(w):
    depth, nb, c, d = w.shape
    eye = jnp.eye(nb, dtype=w.dtype)
    return jnp.einsum('lncd,nm->lncmd', w, eye).reshape(depth, nb * c, nb * d)


def kernel(x, norm_mix, w_in, b_gate, diff_lq1, diff_lk1, diff_lq2, diff_lk2, diff_subln, lru_conv_w, lru_conv_b,
           lru_wa, lru_ba, lru_wx, lru_bx, lru_lambda, fox_b_f, swa_sinks, w_branch, w_out, norm_ffn, w_ffn_gate,
           w_ffn_up, w_ffn_down, norm_final):
    batch, seq, d = x.shape
    depth = w_in.shape[0]
    assert d == D_MODEL and seq % max(A_TQ, C_TQ, LRU_TS, SWA_QB * SWA_BLOCK) == 0 and (batch * seq) % IN_TM == 0
    t = batch * seq

    w_bf = _cast_bf16(w_in, CAST_ROWS)
    w_gz = w_bf[:, :, _R_GZ0:]
    w_d = w_bf[:, :, _R_DQ0:_R_GZ0]
    w_cf = jnp.pad(w_bf[:, :, _R_CF0:_R_DQ0], ((0, 0), (0, 0), (0, CF_PAD - C_HEADS)))
    wb = w_branch.astype(BF16)
    wo = w_out.astype(BF16)
    wg = w_ffn_gate.astype(BF16)
    wu = w_ffn_up.astype(BF16)
    wd = w_ffn_down.astype(BF16)
    wa = _block_diag(lru_wa).astype(BF16)
    wx = _block_diag(lru_wx).astype(BF16)
    lvec = jnp.stack([diff_lq1, diff_lk1, diff_lq2, diff_lk2], axis=1).astype(F32)
    bg = b_gate.reshape(depth, N_BRANCH, D_MODEL)
    gf = norm_final.reshape(1, D_MODEL)

    h = x.reshape(t, D_MODEL)
    u = _norm(h, norm_mix[0].reshape(1, D_MODEL))
    for l in range(depth):
        lam_init = 0.8 - 0.6 * math.exp(-0.3 * l)
        zg = _proj(u, w_gz, l, ZG_W, "proj_gate")
        zm = _proj(u, w_bf, l, ZM_W, "proj_mix")
        zd, cf = _proj_d(u, w_d, w_cf, l)

        ya = _diff_attn(zm, lvec[l], diff_subln[l].reshape(1, 2 * A_HD),
                        jnp.asarray([lam_init, 1.0 - lam_init], F32), batch, seq)
        yb = _lru(zm, lru_conv_w[l], lru_conv_b[l].reshape(1, -1), wa, lru_ba[l].reshape(1, -1), wx,
                  lru_bx[l].reshape(1, -1), lru_lambda[l].reshape(1, -1), l, batch, seq)
        cf_t = cf[:, :C_HEADS].reshape(batch, seq, C_HEADS).transpose(0, 2, 1).reshape(batch * C_HEADS, seq)
        bf_t = jnp.tile(fox_b_f[l], batch).reshape(batch * C_HEADS, 1)
        cum = _fox_cum(cf_t, bf_t).reshape(batch * C_HEADS, 1, seq)
        yc = _fox_attn(zm, cum, batch, seq)
        yd = _swa(zd, swa_sinks[l], batch, seq)

        h, v = _merge(h, ya, yb, yc, yd, zg, bg[l], wb, wo, norm_ffn[l].reshape(1, D_MODEL), l)
        if l == depth - 1:
            (h,) = _ffn(v, h, wg, wu, wd, gf, l, final_norm=True)
        else:
            h, u = _ffn(v, h, wg, wu, wd, norm_mix[l + 1].reshape(1, D_MODEL), l, final_norm=False)
    return h.reshape(batch, seq, D_MODEL)
```

```python
import functools
import math

import jax
import jax.numpy as jnp
import numpy as np
from jax import lax
from jax.experimental import pallas as pl
from jax.experimental.pallas import tpu as pltpu

F32 = jnp.float32
BF16 = jnp.bfloat16

D_MODEL = 2048
N_BRANCH = 4
BRANCH_W = D_MODEL // 4
A_HEADS = 4
A_HD = BRANCH_W // (2 * A_HEADS)
B_BLOCKS = 8
B_CONV = 4
B_C = 8.0
C_HEADS = 4
C_HD = BRANCH_W // C_HEADS
D_HEADS = 8
D_KV = 2
D_GROUP = D_HEADS // D_KV
D_HD = BRANCH_W // D_HEADS
WINDOW = 128
SWA_BLOCK = 128
SWA_QB = 4
D_FF = -(-8 * D_MODEL // (3 * 256)) * 256
RMS_EPS = 1e-6
NEG_INF = -1e30

AQ0 = 0
AK0 = AQ0 + BRANCH_W
AV0 = AK0 + BRANCH_W
BX0 = AV0 + BRANCH_W
BG0 = BX0 + BRANCH_W
CQ0 = BG0 + BRANCH_W
CK0 = CQ0 + BRANCH_W
CV0 = CK0 + BRANCH_W
ZM_W = CV0 + BRANCH_W
DQ0 = 0
DK0 = DQ0 + BRANCH_W
DV0 = DK0 + D_KV * D_HD
ZD_W = DV0 + D_KV * D_HD
ZG_W = N_BRANCH * D_MODEL
CF_PAD = 128

_R_CF0 = ZM_W
_R_DQ0 = _R_CF0 + C_HEADS
_R_GZ0 = _R_DQ0 + ZD_W

V7X_VMEM_LIMIT = 56 * 1024 * 1024

SPLIT_ROWS = 128
NORM_TM = 1024
IN_TM, IN_TN = 2048, 1024
FLASH_ROWS = 256
A_TQ, A_HPS = 1024, 1
C_TQ, C_HPS = 1024, 2
LOG2E = math.log2(math.e)
LRU_TS = 512
LRU_PAD = LRU_TS // 2
MERGE_TM = 256
FFN_TM, FFN_TF = 512, 512


def _cparams(sem):
    return pltpu.CompilerParams(dimension_semantics=sem, vmem_limit_bytes=V7X_VMEM_LIMIT)


def _rms(x, g):
    return x * lax.rsqrt(jnp.mean(x * x, axis=-1, keepdims=True) + RMS_EPS) * g


def _softplus(y):
    return jnp.maximum(y, 0.0) + jnp.log1p(jnp.exp(-jnp.abs(y)))


def _split_w_kernel(w_ref, zm_ref, gz_ref, d_ref, cf_ref):
    rows = w_ref.shape[0]
    zm_ref[...] = w_ref[:, 0:ZM_W].astype(BF16)
    lane = lax.broadcasted_iota(jnp.int32, (rows, CF_PAD), 1)
    cf_ref[...] = jnp.where(lane < C_HEADS, w_ref[:, _R_CF0:_R_CF0 + CF_PAD], 0.0).astype(BF16)
    dwin = ZD_W + 128
    d = pltpu.roll(w_ref[:, _R_CF0:_R_CF0 + dwin], dwin - C_HEADS, axis=1)
    d_ref[...] = d[:, 0:ZD_W].astype(BF16)
    g0 = _R_GZ0 - C_HEADS
    g = pltpu.roll(w_ref[:, g0:g0 + ZG_W], ZG_W - C_HEADS, axis=1)
    gz_ref[:, 0:ZG_W - 128] = g[:, 0:ZG_W - 128].astype(BF16)
    last = jnp.concatenate([g[:, ZG_W - 128:ZG_W - C_HEADS], w_ref[:, g0 + ZG_W:g0 + ZG_W + C_HEADS]], axis=1)
    gz_ref[:, ZG_W - 128:ZG_W] = last.astype(BF16)


def _split_w(w_in):
    depth, r, c = w_in.shape
    assert c == _R_GZ0 + ZG_W and (_R_GZ0 - C_HEADS) % 128 == 0 and _R_CF0 % 128 == 0
    spec = lambda n: pl.BlockSpec((None, SPLIT_ROWS, n), lambda l, i: (l, i, 0))
    shape = lambda n: jax.ShapeDtypeStruct((depth, r, n), BF16)
    return pl.pallas_call(
        _split_w_kernel,
        grid=(depth, r // SPLIT_ROWS),
        in_specs=[spec(c)],
        out_specs=[spec(ZM_W), spec(ZG_W), spec(ZD_W), spec(CF_PAD)],
        out_shape=[shape(ZM_W), shape(ZG_W), shape(ZD_W), shape(CF_PAD)],
        compiler_params=_cparams(("parallel", "parallel")),
        name="split_w",
    )(w_in)


def _norm_kernel(x_ref, g_ref, o_ref):
    o_ref[...] = _rms(x_ref[...], g_ref[...]).astype(BF16)


def _norm(x, g):
    t = x.shape[0]
    return pl.pallas_call(
        _norm_kernel,
        grid=(t // NORM_TM,),
        in_specs=[pl.BlockSpec((NORM_TM, D_MODEL), lambda i: (i, 0)), pl.BlockSpec((1, D_MODEL), lambda i: (0, 0))],
        out_specs=pl.BlockSpec((NORM_TM, D_MODEL), lambda i: (i, 0)),
        out_shape=jax.ShapeDtypeStruct((t, D_MODEL), BF16),
        compiler_params=_cparams(("parallel",)),
        name="norm0",
    )(x, g)


def _proj_kernel(u_ref, w_ref, z_ref):
    z_ref[...] = jnp.dot(u_ref[...], w_ref[...], preferred_element_type=F32).astype(BF16)


def _proj(u, w, l, n_cols, name):
    t = u.shape[0]
    return pl.pallas_call(
        _proj_kernel,
        grid=(t // IN_TM, n_cols // IN_TN),
        in_specs=[
            pl.BlockSpec((IN_TM, D_MODEL), lambda i, j: (i, 0)),
            pl.BlockSpec((None, D_MODEL, IN_TN), lambda i, j: (l, 0, j)),
        ],
        out_specs=pl.BlockSpec((IN_TM, IN_TN), lambda i, j: (i, j)),
        out_shape=jax.ShapeDtypeStruct((t, n_cols), BF16),
        compiler_params=_cparams(("parallel", "arbitrary")),
        name=name,
    )(u, w)


def _proj_d_kernel(u_ref, w_ref, wcf_ref, z_ref, cf_ref):
    u = u_ref[...]
    z_ref[...] = jnp.dot(u, w_ref[...], preferred_element_type=F32).astype(BF16)
    cf_ref[...] = jnp.dot(u, wcf_ref[...], preferred_element_type=F32)


def _proj_d(u, w, wcf, l):
    t = u.shape[0]
    return pl.pallas_call(
        _proj_d_kernel,
        grid=(t // IN_TM,),
        in_specs=[
            pl.BlockSpec((IN_TM, D_MODEL), lambda i: (i, 0)),
            pl.BlockSpec((None, D_MODEL, ZD_W), lambda i: (l, 0, 0)),
            pl.BlockSpec((None, D_MODEL, CF_PAD), lambda i: (l, 0, 0)),
        ],
        out_specs=[
            pl.BlockSpec((IN_TM, ZD_W), lambda i: (i, 0)),
            pl.BlockSpec((IN_TM, CF_PAD), lambda i: (i, 0)),
        ],
        out_shape=[jax.ShapeDtypeStruct((t, ZD_W), BF16), jax.ShapeDtypeStruct((t, CF_PAD), F32)],
        compiler_params=_cparams(("parallel",)),
        name="proj_d",
    )(u, w, wcf)


def _tri_tables(n):
    qi = [i for i in range(n) for _ in range(i + 1)]
    kj = [j for i in range(n) for j in range(i + 1)]
    return jnp.asarray(qi, jnp.int32), jnp.asarray(kj, jnp.int32)


def _flash_rows(q_ref, k_ref, v2_sc, bias, m_sc, acc_sc, r0, diag_q0, scale):
    rows = FLASH_ROWS
    nk = k_ref.shape[0] if diag_q0 is None else diag_q0 + rows
    s = lax.dot_general(q_ref[r0:r0 + rows, :], k_ref[0:nk, :], (((1,), (1,)), ((), ())),
                        preferred_element_type=F32)
    s = s * scale + bias[:, 0:nk]
    if diag_q0 is not None:
        qpos = lax.broadcasted_iota(jnp.int32, s.shape, 0) + diag_q0
        kpos = lax.broadcasted_iota(jnp.int32, s.shape, 1)
        s = jnp.where(kpos <= qpos, s, NEG_INF)
    m_prev = m_sc[r0:r0 + rows, :]
    m_new = jnp.maximum(m_prev, jnp.max(s, axis=-1, keepdims=True))
    alpha = jnp.exp2(m_prev - m_new)
    p = jnp.exp2(s - jnp.concatenate([m_new] * (nk // 128), axis=1))
    pv = jnp.dot(p.astype(BF16), v2_sc[0:nk, :], preferred_element_type=F32)
    acc_sc[r0:r0 + rows, :] = jnp.concatenate([alpha, alpha], axis=1) * acc_sc[r0:r0 + rows, :] + pv
    m_sc[r0:r0 + rows, :] = m_new


def _flash_block(q_ref, k_ref, v2_sc, bias, m_sc, acc_sc, n_rows, tq, diag, scale):
    for r0 in range(0, n_rows, FLASH_ROWS):
        _flash_rows(q_ref, k_ref, v2_sc, bias, m_sc, acc_sc, r0, (r0 % tq) if diag else None, scale)


def _flash_init(v2_sc, m_sc, acc_sc):
    v2_sc[:, 128:256] = jnp.ones((v2_sc.shape[0], 128), BF16)
    m_sc[...] = jnp.full(m_sc.shape, -jnp.inf, F32)
    acc_sc[...] = jnp.zeros(acc_sc.shape, F32)


def _diff_attn_kernel(qi_ref, kj_ref, slope_ref, lami_ref, q_ref, k_ref, v_ref, lv_ref, sg_ref, o_ref,
                      q2_sc, v2_sc, m_sc, acc_sc):
    tq = A_TQ
    t = pl.program_id(2)
    i = qi_ref[t]
    j = kj_ref[t]
    heads = [(hh, slice(hh * 128, (hh + 1) * 128)) for hh in range(A_HPS)]

    @pl.when(j == 0)
    def _():
        for hh, cols in heads:
            q = q_ref[:, cols]
            lane = lax.broadcasted_iota(jnp.int32, q.shape, 1)
            qs = q * jnp.asarray(A_HD ** -0.5, BF16)
            zero = jnp.zeros_like(qs)
            q2_sc[hh, 0:tq, :] = jnp.where(lane < A_HD, qs, zero)
            q2_sc[hh, tq:2 * tq, :] = jnp.where(lane >= A_HD, qs, zero)
            _flash_init(v2_sc.at[hh], m_sc.at[hh], acc_sc.at[hh])

    kpos = (lax.broadcasted_iota(jnp.int32, (1, tq), 1) + (j - i) * tq).astype(F32)

    def block(diag):
        for hh, cols in heads:
            v2_sc[hh, :, 0:128] = v_ref[:, cols]
            bias = kpos * (slope_ref[pl.program_id(1) * A_HPS + hh] * LOG2E)
            _flash_block(q2_sc.at[hh], k_ref.at[:, cols], v2_sc.at[hh], bias, m_sc.at[hh], acc_sc.at[hh],
                         2 * tq, tq, diag, LOG2E)

    @pl.when(j < i)
    def _():
        block(False)

    @pl.when(j == i)
    def _():
        block(True)
        lv = lv_ref[...]
        lam = (jnp.exp(jnp.sum(lv[0:1] * lv[1:2], axis=-1, keepdims=True))
               - jnp.exp(jnp.sum(lv[2:3] * lv[3:4], axis=-1, keepdims=True)) + lami_ref[0])
        for hh, cols in heads:
            acc = acc_sc[hh]
            o = acc[0:tq, 0:128] / acc[0:tq, 128:256] - lam * (acc[tq:2 * tq, 0:128] / acc[tq:2 * tq, 128:256])
            o_ref[:, cols] = (_rms(o, sg_ref[...]) * lami_ref[1]).astype(BF16)


def _diff_attn(z, lvec, subln, lam_init, batch, seq):
    nq = seq // A_TQ
    qi, kj = _tri_tables(nq)
    slopes = jnp.asarray(np.exp2(-8.0 * np.arange(1, A_HEADS + 1, dtype=np.float32) / A_HEADS), F32)
    w = 128 * A_HPS
    qc, kc, vc = AQ0 // w, AK0 // w, AV0 // w
    smem = pl.BlockSpec(memory_space=pltpu.SMEM)
    grid_spec = pltpu.PrefetchScalarGridSpec(
        num_scalar_prefetch=2,
        grid=(batch, A_HEADS // A_HPS, int(qi.shape[0])),
        in_specs=[
            smem, smem,
            pl.BlockSpec((A_TQ, w), lambda b, h, t, qi, kj: (b * nq + qi[t], qc + h)),
            pl.BlockSpec((A_TQ, w), lambda b, h, t, qi, kj: (b * nq + kj[t], kc + h)),
            pl.BlockSpec((A_TQ, w), lambda b, h, t, qi, kj: (b * nq + kj[t], vc + h)),
            pl.BlockSpec((4, A_HD), lambda b, h, t, qi, kj: (0, 0)),
            pl.BlockSpec((1, 2 * A_HD), lambda b, h, t, qi, kj: (0, 0)),
        ],
        out_specs=pl.BlockSpec((A_TQ, w), lambda b, h, t, qi, kj: (b * nq + qi[t], h)),
        scratch_shapes=[
            pltpu.VMEM((A_HPS, 2 * A_TQ, 128), BF16),
            pltpu.VMEM((A_HPS, A_TQ, 256), BF16),
            pltpu.VMEM((A_HPS, 2 * A_TQ, 128), F32),
            pltpu.VMEM((A_HPS, 2 * A_TQ, 256), F32),
        ],
    )
    return pl.pallas_call(
        _diff_attn_kernel,
        grid_spec=grid_spec,
        out_shape=jax.ShapeDtypeStruct((batch * seq, BRANCH_W), BF16),
        compiler_params=_cparams(("parallel", "parallel", "arbitrary")),
        name="diff_attn",
    )(qi, kj, slopes, lam_init, z, z, z, lvec, subln)


def _fox_cum_kernel(cf_ref, bf_ref, o_ref):
    x = cf_ref[...] + bf_ref[...]
    logf = jnp.minimum(x, 0.0) - jnp.log1p(jnp.exp(-jnp.abs(x)))
    rows, seq = logf.shape
    lane = lax.broadcasted_iota(jnp.int32, (rows, 128), 1)
    carry = jnp.zeros((rows, 1), F32)
    for c in range(seq // 128):
        y = logf[:, c * 128:(c + 1) * 128]
        sh = 1
        while sh < 128:
            y = y + jnp.where(lane >= sh, pltpu.roll(y, sh, axis=1), 0.0)
            sh *= 2
        y = y + carry
        o_ref[:, c * 128:(c + 1) * 128] = y
        carry = y[:, 127:128]


def _fox_cum(cf_t, bf_t):
    rows, seq = cf_t.shape
    return pl.pallas_call(
        _fox_cum_kernel,
        grid=(1,),
        in_specs=[pl.BlockSpec((rows, seq), lambda i: (0, 0)), pl.BlockSpec((rows, 1), lambda i: (0, 0))],
        out_specs=pl.BlockSpec((rows, seq), lambda i: (0, 0)),
        out_shape=jax.ShapeDtypeStruct((rows, seq), F32),
        compiler_params=_cparams(("arbitrary",)),
        name="fox_cum",
    )(cf_t, bf_t)


def _fox_attn_kernel(qi_ref, kj_ref, q_ref, k_ref, v_ref, cum_ref, o_ref, v2_sc, m_sc, acc_sc):
    tq = C_TQ
    t = pl.program_id(2)
    i = qi_ref[t]
    j = kj_ref[t]
    heads = [(hh, slice(hh * 128, (hh + 1) * 128)) for hh in range(C_HPS)]

    @pl.when(j == 0)
    def _():
        for hh, _ in heads:
            _flash_init(v2_sc.at[hh], m_sc.at[hh], acc_sc.at[hh])

    def block(diag):
        for hh, cols in heads:
            v2_sc[hh, :, 0:128] = v_ref[:, cols]
            _flash_block(q_ref.at[:, cols], k_ref.at[:, cols], v2_sc.at[hh], cum_ref[hh] * (-LOG2E), m_sc.at[hh],
                         acc_sc.at[hh], tq, tq, diag, C_HD ** -0.5 * LOG2E)

    @pl.when(j < i)
    def _():
        block(False)

    @pl.when(j == i)
    def _():
        block(True)
        for hh, cols in heads:
            acc = acc_sc[hh]
            o_ref[:, cols] = (acc[:, 0:128] / acc[:, 128:256]).astype(BF16)


def _fox_attn(z, cum, batch, seq):
    nq = seq // C_TQ
    qi, kj = _tri_tables(nq)
    w = 128 * C_HPS
    qc, kc, vc = CQ0 // w, CK0 // w, CV0 // w
    hg = C_HEADS // C_HPS
    grid_spec = pltpu.PrefetchScalarGridSpec(
        num_scalar_prefetch=2,
        grid=(batch, hg, int(qi.shape[0])),
        in_specs=[
            pl.BlockSpec((C_TQ, w), lambda b, h, t, qi, kj: (b * nq + qi[t], qc + h)),
            pl.BlockSpec((C_TQ, w), lambda b, h, t, qi, kj: (b * nq + kj[t], kc + h)),
            pl.BlockSpec((C_TQ, w), lambda b, h, t, qi, kj: (b * nq + kj[t], vc + h)),
            pl.BlockSpec((C_HPS, 1, C_TQ), lambda b, h, t, qi, kj: (b * hg + h, 0, kj[t])),
        ],
        out_specs=pl.BlockSpec((C_TQ, w), lambda b, h, t, qi, kj: (b * nq + qi[t], h)),
        scratch_shapes=[
            pltpu.VMEM((C_HPS, C_TQ, 256), BF16),
            pltpu.VMEM((C_HPS, C_TQ, 128), F32),
            pltpu.VMEM((C_HPS, C_TQ, 256), F32),
        ],
    )
    return pl.pallas_call(
        _fox_attn_kernel,
        grid_spec=grid_spec,
        out_shape=jax.ShapeDtypeStruct((batch * seq, BRANCH_W), BF16),
        compiler_params=_cparams(("parallel", "parallel", "arbitrary")),
        name="fox_attn",
    )(qi, kj, z, z, z, cum)


def _swa_kernel(sink_ref, q_ref, kc_ref, kp_ref, vc_ref, vp_ref, bias_ref, o_ref):
    blk = SWA_BLOCK
    first = pl.program_id(1) == 0
    swap = lambda x: jnp.concatenate([x[:, D_HD:2 * D_HD], x[:, 0:D_HD]], axis=1)
    ones = jnp.ones((blk * (SWA_QB + 1), 128), BF16)
    kk_all = jnp.concatenate([kp_ref[...], kc_ref[...]], axis=0)
    vv_all = jnp.concatenate([vp_ref[...], vc_ref[...]], axis=0)
    kk_var = (kk_all, swap(kk_all))
    vv_var = (jnp.concatenate([vv_all, ones], axis=1), jnp.concatenate([swap(vv_all), ones], axis=1))
    q_all = q_ref[...] * jnp.asarray(D_HD ** -0.5, BF16)
    q_lane = lax.broadcasted_iota(jnp.int32, (2 * blk, 128), 1)
    o_lane = lax.broadcasted_iota(jnp.int32, (blk, 128), 1)
    for qb in range(SWA_QB):
        keys = slice(qb * blk, (qb + 2) * blk)
        for kv in range(D_KV):
            res = []
            for half in range(2):
                q2 = jnp.concatenate([q_all[qb * blk:(qb + 1) * blk, (2 * kv + tt) * 128:(2 * kv + tt + 1) * 128]
                                      for tt in range(2)], axis=0)
                q2 = jnp.where((q_lane < D_HD) if half == 0 else (q_lane >= D_HD), q2, jnp.zeros_like(q2))
                var = half ^ kv
                s = lax.dot_general(q2, kk_var[var][keys], (((1,), (1,)), ((), ())), preferred_element_type=F32)
                bias = bias_ref[1, kv, half]
                if qb == 0:
                    bias = jnp.where(first, bias_ref[0, kv, half], bias)
                lg = s + bias
                es, sinks = [], []
                for tt in range(2):
                    sink = sink_ref[4 * kv + half + 2 * tt]
                    lg_h = lg[tt * blk:(tt + 1) * blk]
                    m = jnp.maximum(jnp.max(lg_h, axis=-1, keepdims=True), sink)
                    es.append(jnp.exp(lg_h - m))
                    sinks.append(jnp.exp(sink - m))
                pv = jnp.dot(jnp.concatenate(es, axis=0).astype(BF16), vv_var[var][keys],
                             preferred_element_type=F32)
                res.append([pv[tt * blk:(tt + 1) * blk, 0:128] / (pv[tt * blk:(tt + 1) * blk, 128:256] + sinks[tt])
                            for tt in range(2)])
            for tt in range(2):
                tile = jnp.where(o_lane < D_HD, res[0][tt], res[1][tt])
                o_ref[qb * blk:(qb + 1) * blk, (2 * kv + tt) * 128:(2 * kv + tt + 1) * 128] = tile.astype(BF16)


def _swa_bias():
    slopes = np.exp2(-8.0 * np.arange(1, D_HEADS + 1, dtype=np.float32) / D_HEADS).astype(np.float32)
    qi = np.arange(SWA_BLOCK)
    kj = np.arange(2 * SWA_BLOCK) - SWA_BLOCK
    dist = (qi[:, None] - kj[None, :]).astype(np.float32)
    valid = (dist >= 0) & (dist < WINDOW)
    general = np.where(valid[None], -(slopes[:, None, None] * dist[None]), np.float32(NEG_INF))
    first = np.where((kj >= 0)[None, None, :], general, np.float32(NEG_INF))
    per_head = np.stack([first, general]).astype(np.float32)
    table = np.stack([np.stack([np.concatenate([per_head[:, 4 * kv + half], per_head[:, 4 * kv + half + 2]], axis=1)
                                for half in range(2)], axis=1) for kv in range(D_KV)], axis=1)
    return jnp.asarray(table)


def _swa(z, sinks, batch, seq):
    rows = SWA_QB * SWA_BLOCK
    ns = seq // rows
    qc, kc, vc = DQ0 // BRANCH_W, DK0 // 128, DV0 // 128
    cur = lambda b, n: b * ns + n
    prev = lambda b, n: jnp.maximum((b * ns + n) * SWA_QB - 1, 0)
    return pl.pallas_call(
        _swa_kernel,
        grid=(batch, ns),
        in_specs=[
            pl.BlockSpec(memory_space=pltpu.SMEM),
            pl.BlockSpec((rows, BRANCH_W), lambda b, n: (cur(b, n), qc)),
            pl.BlockSpec((rows, 128), lambda b, n: (cur(b, n), kc)),
            pl.BlockSpec((SWA_BLOCK, 128), lambda b, n: (prev(b, n), kc)),
            pl.BlockSpec((rows, 128), lambda b, n: (cur(b, n), vc)),
            pl.BlockSpec((SWA_BLOCK, 128), lambda b, n: (prev(b, n), vc)),
            pl.BlockSpec((2, D_KV, 2, 2 * SWA_BLOCK, 2 * SWA_BLOCK), lambda b, n: (0, 0, 0, 0, 0)),
        ],
        out_specs=pl.BlockSpec((rows, BRANCH_W), lambda b, n: (cur(b, n), 0)),
        out_shape=jax.ShapeDtypeStruct((batch * seq, BRANCH_W), BF16),
        compiler_params=_cparams(("parallel", "arbitrary")),
        name="swa",
    )(sinks, z, z, z, z, z, _swa_bias())


def _lru_kernel(x_ref, g_ref, cw_ref, cb_ref, wa_ref, ba_ref, wx_ref, bx_ref, lam_ref, o_ref,
                xpad_sc, apad_sc, upad_sc, hc_sc):
    ts, pad = LRU_TS, LRU_PAD

    @pl.when(pl.program_id(1) == 0)
    def _():
        xpad_sc[0:8, :] = jnp.zeros((8, BRANCH_W), F32)
        hc_sc[...] = jnp.zeros(hc_sc.shape, F32)
        apad_sc[0:pad, :] = jnp.ones((pad, BRANCH_W), F32)
        upad_sc[0:pad, :] = jnp.zeros((pad, BRANCH_W), F32)

    x = x_ref[...].astype(F32)
    xpad_sc[8:8 + ts, :] = x
    cw = cw_ref[...]
    xc = (cw[3:4] * x + cw[2:3] * xpad_sc[7:7 + ts, :] + cw[1:2] * xpad_sc[6:6 + ts, :]
          + cw[0:1] * xpad_sc[5:5 + ts, :] + cb_ref[...])
    xpad_sc[0:8, :] = x[ts - 8:ts]

    xcb = xc.astype(BF16)
    r = jax.nn.sigmoid(jnp.dot(xcb, wa_ref[...], preferred_element_type=F32) + ba_ref[...])
    gi = jax.nn.sigmoid(jnp.dot(xcb, wx_ref[...], preferred_element_type=F32) + bx_ref[...])
    log_a = (-B_C * r) * _softplus(-lam_ref[...])
    a = jnp.exp(log_a)
    u = jnp.sqrt(1.0 - a * a) * (gi * xc)

    sh = 1
    while sh < ts:
        apad_sc[pad:pad + ts, :] = a
        upad_sc[pad:pad + ts, :] = u
        u = a * upad_sc[pad - sh:pad - sh + ts, :] + u
        a = a * apad_sc[pad - sh:pad - sh + ts, :]
        sh *= 2
    h = a * hc_sc[...] + u
    hc_sc[...] = h[ts - 1:ts]
    g = g_ref[...].astype(F32)
    gelu = g * jax.nn.sigmoid((2.0 * math.sqrt(2.0 / math.pi)) * (g + 0.044715 * (g * g * g)))
    o_ref[...] = (gelu * h).astype(BF16)


def _lru(z, cw, cb, wa, ba, wx, bx, lam, l, batch, seq):
    ns = seq // LRU_TS
    xc_, gc_ = BX0 // BRANCH_W, BG0 // BRANCH_W
    const = lambda shape: pl.BlockSpec(shape, lambda b, s: (0, 0))
    return pl.pallas_call(
        _lru_kernel,
        grid=(batch, ns),
        in_specs=[
            pl.BlockSpec((LRU_TS, BRANCH_W), lambda b, s: (b * ns + s, xc_)),
            pl.BlockSpec((LRU_TS, BRANCH_W), lambda b, s: (b * ns + s, gc_)),
            const((B_CONV, BRANCH_W)), const((1, BRANCH_W)),
            pl.BlockSpec((None, BRANCH_W, BRANCH_W), lambda b, s: (l, 0, 0)), const((1, BRANCH_W)),
            pl.BlockSpec((None, BRANCH_W, BRANCH_W), lambda b, s: (l, 0, 0)), const((1, BRANCH_W)),
            const((1, BRANCH_W)),
        ],
        out_specs=pl.BlockSpec((LRU_TS, BRANCH_W), lambda b, s: (b * ns + s, 0)),
        out_shape=jax.ShapeDtypeStruct((batch * seq, BRANCH_W), BF16),
        scratch_shapes=[
            pltpu.VMEM((LRU_TS + 8, BRANCH_W), F32),
            pltpu.VMEM((LRU_TS + LRU_PAD, BRANCH_W), F32),
            pltpu.VMEM((LRU_TS + LRU_PAD, BRANCH_W), F32),
            pltpu.VMEM((1, BRANCH_W), F32),
        ],
        compiler_params=_cparams(("parallel", "arbitrary")),
        name="lru",
    )(z, z, cw, cb, wa, ba, wx, bx, lam)


def _merge_kernel(h_ref, ya_ref, yb_ref, yc_ref, yd_ref, zg_ref, bg_ref, wb_ref, wo_ref, gn_ref, o_ref, v_ref):
    ys = (ya_ref, yb_ref, yc_ref, yd_ref)
    mixed = None
    for n in range(N_BRANCH):
        proj = jnp.dot(ys[n][...], wb_ref[n], preferred_element_type=F32)
        gate = zg_ref[:, n * D_MODEL:(n + 1) * D_MODEL].astype(F32) + bg_ref[n:n + 1, :]
        term = jax.nn.sigmoid(gate) * proj
        mixed = term if mixed is None else mixed + term
    o = h_ref[...] + jnp.dot(mixed.astype(BF16), wo_ref[...], preferred_element_type=F32)
    o_ref[...] = o
    v_ref[...] = _rms(o, gn_ref[...]).astype(BF16)


def _merge(h, ya, yb, yc, yd, zg, bg, wb, wo, gn, l):
    t = h.shape[0]
    row = lambda shape: pl.BlockSpec(shape, lambda i: (i, 0))
    return pl.pallas_call(
        _merge_kernel,
        grid=(t // MERGE_TM,),
        in_specs=[
            row((MERGE_TM, D_MODEL)),
            row((MERGE_TM, BRANCH_W)), row((MERGE_TM, BRANCH_W)), row((MERGE_TM, BRANCH_W)),
            row((MERGE_TM, BRANCH_W)),
            row((MERGE_TM, ZG_W)),
            pl.BlockSpec((N_BRANCH, D_MODEL), lambda i: (0, 0)),
            pl.BlockSpec((None, N_BRANCH, BRANCH_W, D_MODEL), lambda i: (l, 0, 0, 0), pipeline_mode=pl.Buffered(1)),
            pl.BlockSpec((None, D_MODEL, D_MODEL), lambda i: (l, 0, 0), pipeline_mode=pl.Buffered(1)),
            pl.BlockSpec((1, D_MODEL), lambda i: (0, 0)),
        ],
        out_specs=[row((MERGE_TM, D_MODEL)), row((MERGE_TM, D_MODEL))],
        out_shape=[jax.ShapeDtypeStruct((t, D_MODEL), F32), jax.ShapeDtypeStruct((t, D_MODEL), BF16)],
        compiler_params=_cparams(("parallel",)),
        name="merge",
    )(h, ya, yb, yc, yd, zg, bg, wb, wo, gn)


def _ffn_kernel(v_ref, h_ref, wg_ref, wu_ref, wd_ref, gn_ref, o_ref, *u_ref, final_norm):
    j = pl.program_id(1)

    @pl.when(j == 0)
    def _():
        o_ref[...] = h_ref[...]

    v = v_ref[...]
    a = jnp.dot(v, wg_ref[...], preferred_element_type=F32)
    b = jnp.dot(v, wu_ref[...], preferred_element_type=F32)
    t = (a * jax.nn.sigmoid(a)) * b
    o_ref[...] += jnp.dot(t.astype(BF16), wd_ref[...], preferred_element_type=F32)

    @pl.when(j == pl.num_programs(1) - 1)
    def _():
        y = _rms(o_ref[...], gn_ref[...])
        if final_norm:
            o_ref[...] = y
        else:
            u_ref[0][...] = y.astype(BF16)


def _ffn(v, h, wg, wu, wd, gn, l, final_norm):
    t = h.shape[0]
    row = lambda: pl.BlockSpec((FFN_TM, D_MODEL), lambda i, j: (i, 0))
    out_specs = [row()] if final_norm else [row(), row()]
    out_shape = [jax.ShapeDtypeStruct((t, D_MODEL), F32)]
    if not final_norm:
        out_shape.append(jax.ShapeDtypeStruct((t, D_MODEL), BF16))
    return pl.pallas_call(
        functools.partial(_ffn_kernel, final_norm=final_norm),
        grid=(t // FFN_TM, D_FF // FFN_TF),
        in_specs=[
            row(), row(),
            pl.BlockSpec((None, D_MODEL, FFN_TF), lambda i, j: (l, 0, j)),
            pl.BlockSpec((None, D_MODEL, FFN_TF), lambda i, j: (l, 0, j)),
            pl.BlockSpec((None, FFN_TF, D_MODEL), lambda i, j: (l, j, 0)),
            pl.BlockSpec((1, D_MODEL), lambda i, j: (0, 0)),
        ],
        out_specs=out_specs,
        out_shape=out_shape,
        compiler_params=_cparams(("parallel", "arbitrary")),
        name="ffn_final" if final_norm else "ffn",
    )(v, h, wg, wu, wd, gn)


def _block_diag(w):
    depth, nb, c, d = w.shape
    eye = jnp.eye(nb, dtype=w.dtype)
    return jnp.einsum('lncd,nm->lncmd', w, eye).reshape(depth, nb * c, nb * d)


def kernel(x, norm_mix, w_in, b_gate, diff_lq1, diff_lk1, diff_lq2, diff_lk2, diff_subln, lru_conv_w, lru_conv_b,
           lru_wa, lru_ba, lru_wx, lru_bx, lru_lambda, fox_b_f, swa_sinks, w_branch, w_out, norm_ffn, w_ffn_gate,
           w_ffn_up, w_ffn_down, norm_final):
    batch, seq, d = x.shape
    depth = w_in.shape[0]
    assert d == D_MODEL and seq % max(A_TQ, C_TQ, LRU_TS, SWA_QB * SWA_BLOCK) == 0 and (batch * seq) % IN_TM == 0
    t = batch * seq

    w_zm, w_gz, w_d, w_cf = _split_w(w_in)
    wb = w_branch.astype(BF16)
    wo = w_out.astype(BF16)
    wg = w_ffn_gate.astype(BF16)
    wu = w_ffn_up.astype(BF16)
    wd = w_ffn_down.astype(BF16)
    wa = _block_diag(lru_wa).astype(BF16)
    wx = _block_diag(lru_wx).astype(BF16)
    lvec = jnp.stack([diff_lq1, diff_lk1, diff_lq2, diff_lk2], axis=1).astype(F32)
    bg = b_gate.reshape(depth, N_BRANCH, D_MODEL)
    gf = norm_final.reshape(1, D_MODEL)

    h = x.reshape(t, D_MODEL)
    u = _norm(h, norm_mix[0].reshape(1, D_MODEL))
    for l in range(depth):
        lam_init = 0.8 - 0.6 * math.exp(-0.3 * l)
        zg = _proj(u, w_gz, l, ZG_W, "proj_gate")
        zm = _proj(u, w_zm, l, ZM_W, "proj_mix")
        zd, cf = _proj_d(u, w_d, w_cf, l)

        ya = _diff_attn(zm, lvec[l], diff_subln[l].reshape(1, 2 * A_HD),
                        jnp.asarray([lam_init, 1.0 - lam_init], F32), batch, seq)
        yb = _lru(zm, lru_conv_w[l], lru_conv_b[l].reshape(1, -1), wa, lru_ba[l].reshape(1, -1), wx,
                  lru_bx[l].reshape(1, -1), lru_lambda[l].reshape(1, -1), l, batch, seq)
        cf_t = cf[:, :C_HEADS].reshape(batch, seq, C_HEADS).transpose(0, 2, 1).reshape(batch * C_HEADS, seq)
        bf_t = jnp.tile(fox_b_f[l], batch).reshape(batch * C_HEADS, 1)
        cum = _fox_cum(cf_t, bf_t).reshape(batch * C_HEADS, 1, seq)
        yc = _fox_attn(zm, cum, batch, seq)
        yd = _swa(zd, swa_sinks[l], batch, seq)

        h, v = _merge(h, ya, yb, yc, yd, zg, bg[l], wb, wo, norm_ffn[l].reshape(1, D_MODEL), l)
        if l == depth - 1:
            (h,) = _ffn(v, h, wg, wu, wd, gf, l, final_norm=True)
        else:
            h, u = _ffn(v, h, wg, wu, wd, norm_mix[l + 1].reshape(1, D_MODEL), l, final_norm=False)
    return h.reshape(batch, seq, D_MODEL)
```

```python
import functools
import math

import jax
import jax.numpy as jnp
import numpy as np
from jax import lax
from jax.experimental import pallas as pl
from jax.experimental.pallas import tpu as pltpu

F32 = jnp.float32
BF16 = jnp.bfloat16

D_MODEL = 2048
N_BRANCH = 4
BRANCH_W = D_MODEL // 4
A_HEADS = 4
A_HD = BRANCH_W // (2 * A_HEADS)
B_BLOCKS = 8
B_CONV = 4
B_C = 8.0
C_HEADS = 4
C_HD = BRANCH_W // C_HEADS
D_HEADS = 8
D_KV = 2
D_GROUP = D_HEADS // D_KV
D_HD = BRANCH_W // D_HEADS
WINDOW = 128
SWA_BLOCK = 128
SWA_QB = 4
D_FF = -(-8 * D_MODEL // (3 * 256)) * 256
RMS_EPS = 1e-6
NEG_INF = -1e30

AQ0 = 0
AK0 = AQ0 + BRANCH_W
AV0 = AK0 + BRANCH_W
BX0 = AV0 + BRANCH_W
BG0 = BX0 + BRANCH_W
CQ0 = BG0 + BRANCH_W
CK0 = CQ0 + BRANCH_W
CV0 = CK0 + BRANCH_W
ZM_W = CV0 + BRANCH_W
DQ0 = 0
DK0 = DQ0 + BRANCH_W
DV0 = DK0 + D_KV * D_HD
ZD_W = DV0 + D_KV * D_HD
ZG_W = N_BRANCH * D_MODEL
CF_PAD = 128

_R_CF0 = ZM_W
_R_DQ0 = _R_CF0 + C_HEADS
_R_GZ0 = _R_DQ0 + ZD_W

V7X_VMEM_LIMIT = 56 * 1024 * 1024

NORM_TM = 1024
IN_TM, IN_TN = 2048, 1024
FLASH_ROWS = 256
A_TQ, A_HPS = 1024, 1
C_TQ, C_HPS = 1024, 2
LOG2E = math.log2(math.e)
LRU_TS = 512
LRU_PAD = LRU_TS // 2
MERGE_TM = 256
FFN_TM, FFN_TF = 512, 512


def _cparams(sem):
    return pltpu.CompilerParams(dimension_semantics=sem, vmem_limit_bytes=V7X_VMEM_LIMIT)


def _rms(x, g):
    return x * lax.rsqrt(jnp.mean(x * x, axis=-1, keepdims=True) + RMS_EPS) * g


def _softplus(y):
    return jnp.maximum(y, 0.0) + jnp.log1p(jnp.exp(-jnp.abs(y)))


def _norm_kernel(x_ref, g_ref, o_ref):
    o_ref[...] = _rms(x_ref[...], g_ref[...]).astype(BF16)


def _norm(x, g):
    t = x.shape[0]
    return pl.pallas_call(
        _norm_kernel,
        grid=(t // NORM_TM,),
        in_specs=[pl.BlockSpec((NORM_TM, D_MODEL), lambda i: (i, 0)), pl.BlockSpec((1, D_MODEL), lambda i: (0, 0))],
        out_specs=pl.BlockSpec((NORM_TM, D_MODEL), lambda i: (i, 0)),
        out_shape=jax.ShapeDtypeStruct((t, D_MODEL), BF16),
        compiler_params=_cparams(("parallel",)),
        name="norm0",
    )(x, g)


_NT = (((1,), (1,)), ((), ()))


def _proj_kernel(u_ref, wt_ref, z_ref):
    z_ref[...] = lax.dot_general(u_ref[...], wt_ref[...], _NT, preferred_element_type=F32).astype(BF16)


def _proj(u, wt, l, name):
    t = u.shape[0]
    n_cols = wt.shape[1]
    return pl.pallas_call(
        _proj_kernel,
        grid=(t // IN_TM, n_cols // IN_TN),
        in_specs=[
            pl.BlockSpec((IN_TM, D_MODEL), lambda i, j: (i, 0)),
            pl.BlockSpec((None, IN_TN, D_MODEL), lambda i, j: (l, j, 0)),
        ],
        out_specs=pl.BlockSpec((IN_TM, IN_TN), lambda i, j: (i, j)),
        out_shape=jax.ShapeDtypeStruct((t, n_cols), BF16),
        compiler_params=_cparams(("parallel", "arbitrary")),
        name=name,
    )(u, wt)


def _proj_d_kernel(u_ref, wt_ref, wcft_ref, z_ref, cf_ref):
    u = u_ref[...]
    z_ref[...] = lax.dot_general(u, wt_ref[...], _NT, preferred_element_type=F32).astype(BF16)
    cf_ref[...] = lax.dot_general(u, wcft_ref[...], _NT, preferred_element_type=F32)


def _proj_d(u, wt, wcft, l):
    t = u.shape[0]
    return pl.pallas_call(
        _proj_d_kernel,
        grid=(t // IN_TM,),
        in_specs=[
            pl.BlockSpec((IN_TM, D_MODEL), lambda i: (i, 0)),
            pl.BlockSpec((None, ZD_W, D_MODEL), lambda i: (l, 0, 0)),
            pl.BlockSpec((None, CF_PAD, D_MODEL), lambda i: (l, 0, 0)),
        ],
        out_specs=[
            pl.BlockSpec((IN_TM, ZD_W), lambda i: (i, 0)),
            pl.BlockSpec((IN_TM, CF_PAD), lambda i: (i, 0)),
        ],
        out_shape=[jax.ShapeDtypeStruct((t, ZD_W), BF16), jax.ShapeDtypeStruct((t, CF_PAD), F32)],
        compiler_params=_cparams(("parallel",)),
        name="proj_d",
    )(u, wt, wcft)


def _tri_tables(n):
    qi = [i for i in range(n) for _ in range(i + 1)]
    kj = [j for i in range(n) for j in range(i + 1)]
    return jnp.asarray(qi, jnp.int32), jnp.asarray(kj, jnp.int32)


def _flash_rows(q_ref, k_ref, v2_sc, bias, m_sc, acc_sc, r0, diag_q0, scale):
    rows = FLASH_ROWS
    nk = k_ref.shape[0] if diag_q0 is None else diag_q0 + rows
    s = lax.dot_general(q_ref[r0:r0 + rows, :], k_ref[0:nk, :], (((1,), (1,)), ((), ())),
                        preferred_element_type=F32)
    s = s * scale + bias[:, 0:nk]
    if diag_q0 is not None:
        qpos = lax.broadcasted_iota(jnp.int32, s.shape, 0) + diag_q0
        kpos = lax.broadcasted_iota(jnp.int32, s.shape, 1)
        s = jnp.where(kpos <= qpos, s, NEG_INF)
    m_prev = m_sc[r0:r0 + rows, :]
    m_new = jnp.maximum(m_prev, jnp.max(s, axis=-1, keepdims=True))
    alpha = jnp.exp2(m_prev - m_new)
    p = jnp.exp2(s - jnp.concatenate([m_new] * (nk // 128), axis=1))
    pv = jnp.dot(p.astype(BF16), v2_sc[0:nk, :], preferred_element_type=F32)
    acc_sc[r0:r0 + rows, :] = jnp.concatenate([alpha, alpha], axis=1) * acc_sc[r0:r0 + rows, :] + pv
    m_sc[r0:r0 + rows, :] = m_new


def _flash_block(q_ref, k_ref, v2_sc, bias, m_sc, acc_sc, n_rows, tq, diag, scale):
    for r0 in range(0, n_rows, FLASH_ROWS):
        _flash_rows(q_ref, k_ref, v2_sc, bias, m_sc, acc_sc, r0, (r0 % tq) if diag else None, scale)


def _flash_init(v2_sc, m_sc, acc_sc):
    v2_sc[:, 128:256] = jnp.ones((v2_sc.shape[0], 128), BF16)
    m_sc[...] = jnp.full(m_sc.shape, -jnp.inf, F32)
    acc_sc[...] = jnp.zeros(acc_sc.shape, F32)


def _diff_attn_kernel(qi_ref, kj_ref, slope_ref, lami_ref, q_ref, k_ref, v_ref, lv_ref, sg_ref, o_ref,
                      q2_sc, v2_sc, m_sc, acc_sc):
    tq = A_TQ
    t = pl.program_id(2)
    i = qi_ref[t]
    j = kj_ref[t]
    heads = [(hh, slice(hh * 128, (hh + 1) * 128)) for hh in range(A_HPS)]

    @pl.when(j == 0)
    def _():
        for hh, cols in heads:
            q = q_ref[:, cols]
            lane = lax.broadcasted_iota(jnp.int32, q.shape, 1)
            qs = q * jnp.asarray(A_HD ** -0.5, BF16)
            zero = jnp.zeros_like(qs)
            q2_sc[hh, 0:tq, :] = jnp.where(lane < A_HD, qs, zero)
            q2_sc[hh, tq:2 * tq, :] = jnp.where(lane >= A_HD, qs, zero)
            _flash_init(v2_sc.at[hh], m_sc.at[hh], acc_sc.at[hh])

    kpos = (lax.broadcasted_iota(jnp.int32, (1, tq), 1) + (j - i) * tq).astype(F32)

    def block(diag):
        for hh, cols in heads:
            v2_sc[hh, :, 0:128] = v_ref[:, cols]
            bias = kpos * (slope_ref[pl.program_id(1) * A_HPS + hh] * LOG2E)
            _flash_block(q2_sc.at[hh], k_ref.at[:, cols], v2_sc.at[hh], bias, m_sc.at[hh], acc_sc.at[hh],
                         2 * tq, tq, diag, LOG2E)

    @pl.when(j < i)
    def _():
        block(False)

    @pl.when(j == i)
    def _():
        block(True)
        lv = lv_ref[...]
        lam = (jnp.exp(jnp.sum(lv[0:1] * lv[1:2], axis=-1, keepdims=True))
               - jnp.exp(jnp.sum(lv[2:3] * lv[3:4], axis=-1, keepdims=True)) + lami_ref[0])
        for hh, cols in heads:
            acc = acc_sc[hh]
            o = acc[0:tq, 0:128] / acc[0:tq, 128:256] - lam * (acc[tq:2 * tq, 0:128] / acc[tq:2 * tq, 128:256])
            o_ref[:, cols] = (_rms(o, sg_ref[...]) * lami_ref[1]).astype(BF16)


def _diff_attn(z, lvec, subln, lam_init, batch, seq):
    nq = seq // A_TQ
    qi, kj = _tri_tables(nq)
    slopes = jnp.asarray(np.exp2(-8.0 * np.arange(1, A_HEADS + 1, dtype=np.float32) / A_HEADS), F32)
    w = 128 * A_HPS
    qc, kc, vc = AQ0 // w, AK0 // w, AV0 // w
    smem = pl.BlockSpec(memory_space=pltpu.SMEM)
    grid_spec = pltpu.PrefetchScalarGridSpec(
        num_scalar_prefetch=2,
        grid=(batch, A_HEADS // A_HPS, int(qi.shape[0])),
        in_specs=[
            smem, smem,
            pl.BlockSpec((A_TQ, w), lambda b, h, t, qi, kj: (b * nq + qi[t], qc + h)),
            pl.BlockSpec((A_TQ, w), lambda b, h, t, qi, kj: (b * nq + kj[t], kc + h)),
            pl.BlockSpec((A_TQ, w), lambda b, h, t, qi, kj: (b * nq + kj[t], vc + h)),
            pl.BlockSpec((4, A_HD), lambda b, h, t, qi, kj: (0, 0)),
            pl.BlockSpec((1, 2 * A_HD), lambda b, h, t, qi, kj: (0, 0)),
        ],
        out_specs=pl.BlockSpec((A_TQ, w), lambda b, h, t, qi, kj: (b * nq + qi[t], h)),
        scratch_shapes=[
            pltpu.VMEM((A_HPS, 2 * A_TQ, 128), BF16),
            pltpu.VMEM((A_HPS, A_TQ, 256), BF16),
            pltpu.VMEM((A_HPS, 2 * A_TQ, 128), F32),
            pltpu.VMEM((A_HPS, 2 * A_TQ, 256), F32),
        ],
    )
    return pl.pallas_call(
        _diff_attn_kernel,
        grid_spec=grid_spec,
        out_shape=jax.ShapeDtypeStruct((batch * seq, BRANCH_W), BF16),
        compiler_params=_cparams(("parallel", "parallel", "arbitrary")),
        name="diff_attn",
    )(qi, kj, slopes, lam_init, z, z, z, lvec, subln)


def _fox_cum_kernel(cf_ref, bf_ref, o_ref):
    x = cf_ref[...] + bf_ref[...]
    logf = jnp.minimum(x, 0.0) - jnp.log1p(jnp.exp(-jnp.abs(x)))
    rows, seq = logf.shape
    lane = lax.broadcasted_iota(jnp.int32, (rows, 128), 1)
    carry = jnp.zeros((rows, 1), F32)
    for c in range(seq // 128):
        y = logf[:, c * 128:(c + 1) * 128]
        sh = 1
        while sh < 128:
            y = y + jnp.where(lane >= sh, pltpu.roll(y, sh, axis=1), 0.0)
            sh *= 2
        y = y + carry
        o_ref[:, c * 128:(c + 1) * 128] = y
        carry = y[:, 127:128]


def _fox_cum(cf_t, bf_t):
    rows, seq = cf_t.shape
    return pl.pallas_call(
        _fox_cum_kernel,
        grid=(1,),
        in_specs=[pl.BlockSpec((rows, seq), lambda i: (0, 0)), pl.BlockSpec((rows, 1), lambda i: (0, 0))],
        out_specs=pl.BlockSpec((rows, seq), lambda i: (0, 0)),
        out_shape=jax.ShapeDtypeStruct((rows, seq), F32),
        compiler_params=_cparams(("arbitrary",)),
        name="fox_cum",
    )(cf_t, bf_t)


def _fox_attn_kernel(qi_ref, kj_ref, q_ref, k_ref, v_ref, cum_ref, o_ref, v2_sc, m_sc, acc_sc):
    tq = C_TQ
    t = pl.program_id(2)
    i = qi_ref[t]
    j = kj_ref[t]
    heads = [(hh, slice(hh * 128, (hh + 1) * 128)) for hh in range(C_HPS)]

    @pl.when(j == 0)
    def _():
        for hh, _ in heads:
            _flash_init(v2_sc.at[hh], m_sc.at[hh], acc_sc.at[hh])

    def block(diag):
        for hh, cols in heads:
            v2_sc[hh, :, 0:128] = v_ref[:, cols]
            _flash_block(q_ref.at[:, cols], k_ref.at[:, cols], v2_sc.at[hh], cum_ref[hh] * (-LOG2E), m_sc.at[hh],
                         acc_sc.at[hh], tq, tq, diag, C_HD ** -0.5 * LOG2E)

    @pl.when(j < i)
    def _():
        block(False)

    @pl.when(j == i)
    def _():
        block(True)
        for hh, cols in heads:
            acc = acc_sc[hh]
            o_ref[:, cols] = (acc[:, 0:128] / acc[:, 128:256]).astype(BF16)


def _fox_attn(z, cum, batch, seq):
    nq = seq // C_TQ
    qi, kj = _tri_tables(nq)
    w = 128 * C_HPS
    qc, kc, vc = CQ0 // w, CK0 // w, CV0 // w
    hg = C_HEADS // C_HPS
    grid_spec = pltpu.PrefetchScalarGridSpec(
        num_scalar_prefetch=2,
        grid=(batch, hg, int(qi.shape[0])),
        in_specs=[
            pl.BlockSpec((C_TQ, w), lambda b, h, t, qi, kj: (b * nq + qi[t], qc + h)),
            pl.BlockSpec((C_TQ, w), lambda b, h, t, qi, kj: (b * nq + kj[t], kc + h)),
            pl.BlockSpec((C_TQ, w), lambda b, h, t, qi, kj: (b * nq + kj[t], vc + h)),
            pl.BlockSpec((C_HPS, 1, C_TQ), lambda b, h, t, qi, kj: (b * hg + h, 0, kj[t])),
        ],
        out_specs=pl.BlockSpec((C_TQ, w), lambda b, h, t, qi, kj: (b * nq + qi[t], h)),
        scratch_shapes=[
            pltpu.VMEM((C_HPS, C_TQ, 256), BF16),
            pltpu.VMEM((C_HPS, C_TQ, 128), F32),
            pltpu.VMEM((C_HPS, C_TQ, 256), F32),
        ],
    )
    return pl.pallas_call(
        _fox_attn_kernel,
        grid_spec=grid_spec,
        out_shape=jax.ShapeDtypeStruct((batch * seq, BRANCH_W), BF16),
        compiler_params=_cparams(("parallel", "parallel", "arbitrary")),
        name="fox_attn",
    )(qi, kj, z, z, z, cum)


def _swa_kernel(sink_ref, q_ref, kc_ref, kp_ref, vc_ref, vp_ref, bias_ref, o_ref):
    blk = SWA_BLOCK
    first = pl.program_id(1) == 0
    swap = lambda x: jnp.concatenate([x[:, D_HD:2 * D_HD], x[:, 0:D_HD]], axis=1)
    ones = jnp.ones((blk * (SWA_QB + 1), 128), BF16)
    kk_all = jnp.concatenate([kp_ref[...], kc_ref[...]], axis=0)
    vv_all = jnp.concatenate([vp_ref[...], vc_ref[...]], axis=0)
    kk_var = (kk_all, swap(kk_all))
    vv_var = (jnp.concatenate([vv_all, ones], axis=1), jnp.concatenate([swap(vv_all), ones], axis=1))
    q_all = q_ref[...] * jnp.asarray(D_HD ** -0.5, BF16)
    q_lane = lax.broadcasted_iota(jnp.int32, (2 * blk, 128), 1)
    o_lane = lax.broadcasted_iota(jnp.int32, (blk, 128), 1)
    for qb in range(SWA_QB):
        keys = slice(qb * blk, (qb + 2) * blk)
        for kv in range(D_KV):
            res = []
            for half in range(2):
                q2 = jnp.concatenate([q_all[qb * blk:(qb + 1) * blk, (2 * kv + tt) * 128:(2 * kv + tt + 1) * 128]
                                      for tt in range(2)], axis=0)
                q2 = jnp.where((q_lane < D_HD) if half == 0 else (q_lane >= D_HD), q2, jnp.zeros_like(q2))
                var = half ^ kv
                s = lax.dot_general(q2, kk_var[var][keys], (((1,), (1,)), ((), ())), preferred_element_type=F32)
                bias = bias_ref[1, kv, half]
                if qb == 0:
                    bias = jnp.where(first, bias_ref[0, kv, half], bias)
                lg = s + bias
                es, sinks = [], []
                for tt in range(2):
                    sink = sink_ref[4 * kv + half + 2 * tt]
                    lg_h = lg[tt * blk:(tt + 1) * blk]
                    m = jnp.maximum(jnp.max(lg_h, axis=-1, keepdims=True), sink)
                    es.append(jnp.exp(lg_h - m))
                    sinks.append(jnp.exp(sink - m))
                pv = jnp.dot(jnp.concatenate(es, axis=0).astype(BF16), vv_var[var][keys],
                             preferred_element_type=F32)
                res.append([pv[tt * blk:(tt + 1) * blk, 0:128] / (pv[tt * blk:(tt + 1) * blk, 128:256] + sinks[tt])
                            for tt in range(2)])
            for tt in range(2):
                tile = jnp.where(o_lane < D_HD, res[0][tt], res[1][tt])
                o_ref[qb * blk:(qb + 1) * blk, (2 * kv + tt) * 128:(2 * kv + tt + 1) * 128] = tile.astype(BF16)


def _swa_bias():
    slopes = np.exp2(-8.0 * np.arange(1, D_HEADS + 1, dtype=np.float32) / D_HEADS).astype(np.float32)
    qi = np.arange(SWA_BLOCK)
    kj = np.arange(2 * SWA_BLOCK) - SWA_BLOCK
    dist = (qi[:, None] - kj[None, :]).astype(np.float32)
    valid = (dist >= 0) & (dist < WINDOW)
    general = np.where(valid[None], -(slopes[:, None, None] * dist[None]), np.float32(NEG_INF))
    first = np.where((kj >= 0)[None, None, :], general, np.float32(NEG_INF))
    per_head = np.stack([first, general]).astype(np.float32)
    table = np.stack([np.stack([np.concatenate([per_head[:, 4 * kv + half], per_head[:, 4 * kv + half + 2]], axis=1)
                                for half in range(2)], axis=1) for kv in range(D_KV)], axis=1)
    return jnp.asarray(table)


def _swa(z, sinks, batch, seq):
    rows = SWA_QB * SWA_BLOCK
    ns = seq // rows
    qc, kc, vc = DQ0 // BRANCH_W, DK0 // 128, DV0 // 128
    cur = lambda b, n: b * ns + n
    prev = lambda b, n: jnp.maximum((b * ns + n) * SWA_QB - 1, 0)
    return pl.pallas_call(
        _swa_kernel,
        grid=(batch, ns),
        in_specs=[
            pl.BlockSpec(memory_space=pltpu.SMEM),
            pl.BlockSpec((rows, BRANCH_W), lambda b, n: (cur(b, n), qc)),
            pl.BlockSpec((rows, 128), lambda b, n: (cur(b, n), kc)),
            pl.BlockSpec((SWA_BLOCK, 128), lambda b, n: (prev(b, n), kc)),
            pl.BlockSpec((rows, 128), lambda b, n: (cur(b, n), vc)),
            pl.BlockSpec((SWA_BLOCK, 128), lambda b, n: (prev(b, n), vc)),
            pl.BlockSpec((2, D_KV, 2, 2 * SWA_BLOCK, 2 * SWA_BLOCK), lambda b, n: (0, 0, 0, 0, 0)),
        ],
        out_specs=pl.BlockSpec((rows, BRANCH_W), lambda b, n: (cur(b, n), 0)),
        out_shape=jax.ShapeDtypeStruct((batch * seq, BRANCH_W), BF16),
        compiler_params=_cparams(("parallel", "arbitrary")),
        name="swa",
    )(sinks, z, z, z, z, z, _swa_bias())


def _lru_kernel(x_ref, g_ref, cw_ref, cb_ref, wa_ref, ba_ref, wx_ref, bx_ref, lam_ref, o_ref,
                xpad_sc, apad_sc, upad_sc, hc_sc):
    ts, pad = LRU_TS, LRU_PAD

    @pl.when(pl.program_id(1) == 0)
    def _():
        xpad_sc[0:8, :] = jnp.zeros((8, BRANCH_W), F32)
        hc_sc[...] = jnp.zeros(hc_sc.shape, F32)
        apad_sc[0:pad, :] = jnp.ones((pad, BRANCH_W), F32)
        upad_sc[0:pad, :] = jnp.zeros((pad, BRANCH_W), F32)

    x = x_ref[...].astype(F32)
    xpad_sc[8:8 + ts, :] = x
    cw = cw_ref[...]
    xc = (cw[3:4] * x + cw[2:3] * xpad_sc[7:7 + ts, :] + cw[1:2] * xpad_sc[6:6 + ts, :]
          + cw[0:1] * xpad_sc[5:5 + ts, :] + cb_ref[...])
    xpad_sc[0:8, :] = x[ts - 8:ts]

    xcb = xc.astype(BF16)
    r = jax.nn.sigmoid(jnp.dot(xcb, wa_ref[...], preferred_element_type=F32) + ba_ref[...])
    gi = jax.nn.sigmoid(jnp.dot(xcb, wx_ref[...], preferred_element_type=F32) + bx_ref[...])
    log_a = (-B_C * r) * _softplus(-lam_ref[...])
    a = jnp.exp(log_a)
    u = jnp.sqrt(1.0 - a * a) * (gi * xc)

    sh = 1
    while sh < ts:
        apad_sc[pad:pad + ts, :] = a
        upad_sc[pad:pad + ts, :] = u
        u = a * upad_sc[pad - sh:pad - sh + ts, :] + u
        a = a * apad_sc[pad - sh:pad - sh + ts, :]
        sh *= 2
    h = a * hc_sc[...] + u
    hc_sc[...] = h[ts - 1:ts]
    g = g_ref[...].astype(F32)
    gelu = g * jax.nn.sigmoid((2.0 * math.sqrt(2.0 / math.pi)) * (g + 0.044715 * (g * g * g)))
    o_ref[...] = (gelu * h).astype(BF16)


def _lru(z, cw, cb, wa, ba, wx, bx, lam, l, batch, seq):
    ns = seq // LRU_TS
    xc_, gc_ = BX0 // BRANCH_W, BG0 // BRANCH_W
    const = lambda shape: pl.BlockSpec(shape, lambda b, s: (0, 0))
    return pl.pallas_call(
        _lru_kernel,
        grid=(batch, ns),
        in_specs=[
            pl.BlockSpec((LRU_TS, BRANCH_W), lambda b, s: (b * ns + s, xc_)),
            pl.BlockSpec((LRU_TS, BRANCH_W), lambda b, s: (b * ns + s, gc_)),
            const((B_CONV, BRANCH_W)), const((1, BRANCH_W)),
            pl.BlockSpec((None, BRANCH_W, BRANCH_W), lambda b, s: (l, 0, 0)), const((1, BRANCH_W)),
            pl.BlockSpec((None, BRANCH_W, BRANCH_W), lambda b, s: (l, 0, 0)), const((1, BRANCH_W)),
            const((1, BRANCH_W)),
        ],
        out_specs=pl.BlockSpec((LRU_TS, BRANCH_W), lambda b, s: (b * ns + s, 0)),
        out_shape=jax.ShapeDtypeStruct((batch * seq, BRANCH_W), BF16),
        scratch_shapes=[
            pltpu.VMEM((LRU_TS + 8, BRANCH_W), F32),
            pltpu.VMEM((LRU_TS + LRU_PAD, BRANCH_W), F32),
            pltpu.VMEM((LRU_TS + LRU_PAD, BRANCH_W), F32),
            pltpu.VMEM((1, BRANCH_W), F32),
        ],
        compiler_params=_cparams(("parallel", "arbitrary")),
        name="lru",
    )(z, z, cw, cb, wa, ba, wx, bx, lam)


def _merge_kernel(h_ref, ya_ref, yb_ref, yc_ref, yd_ref, zg_ref, bg_ref, wb_ref, wo_ref, gn_ref, o_ref, v_ref):
    ys = (ya_ref, yb_ref, yc_ref, yd_ref)
    mixed = None
    for n in range(N_BRANCH):
        proj = jnp.dot(ys[n][...], wb_ref[n], preferred_element_type=F32)
        gate = zg_ref[:, n * D_MODEL:(n + 1) * D_MODEL].astype(F32) + bg_ref[n:n + 1, :]
        term = jax.nn.sigmoid(gate) * proj
        mixed = term if mixed is None else mixed + term
    o = h_ref[...] + jnp.dot(mixed.astype(BF16), wo_ref[...], preferred_element_type=F32)
    o_ref[...] = o
    v_ref[...] = _rms(o, gn_ref[...]).astype(BF16)


def _merge(h, ya, yb, yc, yd, zg, bg, wb, wo, gn, l):
    t = h.shape[0]
    row = lambda shape: pl.BlockSpec(shape, lambda i: (i, 0))
    return pl.pallas_call(
        _merge_kernel,
        grid=(t // MERGE_TM,),
        in_specs=[
            row((MERGE_TM, D_MODEL)),
            row((MERGE_TM, BRANCH_W)), row((MERGE_TM, BRANCH_W)), row((MERGE_TM, BRANCH_W)),
            row((MERGE_TM, BRANCH_W)),
            row((MERGE_TM, ZG_W)),
            pl.BlockSpec((N_BRANCH, D_MODEL), lambda i: (0, 0)),
            pl.BlockSpec((None, N_BRANCH, BRANCH_W, D_MODEL), lambda i: (l, 0, 0, 0), pipeline_mode=pl.Buffered(1)),
            pl.BlockSpec((None, D_MODEL, D_MODEL), lambda i: (l, 0, 0), pipeline_mode=pl.Buffered(1)),
            pl.BlockSpec((1, D_MODEL), lambda i: (0, 0)),
        ],
        out_specs=[row((MERGE_TM, D_MODEL)), row((MERGE_TM, D_MODEL))],
        out_shape=[jax.ShapeDtypeStruct((t, D_MODEL), F32), jax.ShapeDtypeStruct((t, D_MODEL), BF16)],
        compiler_params=_cparams(("parallel",)),
        name="merge",
    )(h, ya, yb, yc, yd, zg, bg, wb, wo, gn)


def _ffn_kernel(v_ref, h_ref, wg_ref, wu_ref, wd_ref, gn_ref, o_ref, *u_ref, final_norm):
    j = pl.program_id(1)

    @pl.when(j == 0)
    def _():
        o_ref[...] = h_ref[...]

    v = v_ref[...]
    a = jnp.dot(v, wg_ref[...], preferred_element_type=F32)
    b = jnp.dot(v, wu_ref[...], preferred_element_type=F32)
    t = (a * jax.nn.sigmoid(a)) * b
    o_ref[...] += jnp.dot(t.astype(BF16), wd_ref[...], preferred_element_type=F32)

    @pl.when(j == pl.num_programs(1) - 1)
    def _():
        y = _rms(o_ref[...], gn_ref[...])
        if final_norm:
            o_ref[...] = y
        else:
            u_ref[0][...] = y.astype(BF16)


def _ffn(v, h, wg, wu, wd, gn, l, final_norm):
    t = h.shape[0]
    row = lambda: pl.BlockSpec((FFN_TM, D_MODEL), lambda i, j: (i, 0))
    out_specs = [row()] if final_norm else [row(), row()]
    out_shape = [jax.ShapeDtypeStruct((t, D_MODEL), F32)]
    if not final_norm:
        out_shape.append(jax.ShapeDtypeStruct((t, D_MODEL), BF16))
    return pl.pallas_call(
        functools.partial(_ffn_kernel, final_norm=final_norm),
        grid=(t // FFN_TM, D_FF // FFN_TF),
        in_specs=[
            row(), row(),
            pl.BlockSpec((None, D_MODEL, FFN_TF), lambda i, j: (l, 0, j)),
            pl.BlockSpec((None, D_MODEL, FFN_TF), lambda i, j: (l, 0, j)),
            pl.BlockSpec((None, FFN_TF, D_MODEL), lambda i, j: (l, j, 0)),
            pl.BlockSpec((1, D_MODEL), lambda i, j: (0, 0)),
        ],
        out_specs=out_specs,
        out_shape=out_shape,
        compiler_params=_cparams(("parallel", "arbitrary")),
        name="ffn_final" if final_norm else "ffn",
    )(v, h, wg, wu, wd, gn)


def _block_diag(w):
    depth, nb, c, d = w.shape
    eye = jnp.eye(nb, dtype=w.dtype)
    return jnp.einsum('lncd,nm->lncmd', w, eye).reshape(depth, nb * c, nb * d)


def kernel(x, norm_mix, w_in, b_gate, diff_lq1, diff_lk1, diff_lq2, diff_lk2, diff_subln, lru_conv_w, lru_conv_b,
           lru_wa, lru_ba, lru_wx, lru_bx, lru_lambda, fox_b_f, swa_sinks, w_branch, w_out, norm_ffn, w_ffn_gate,
           w_ffn_up, w_ffn_down, norm_final):
    batch, seq, d = x.shape
    depth = w_in.shape[0]
    assert d == D_MODEL and seq % max(A_TQ, C_TQ, LRU_TS, SWA_QB * SWA_BLOCK) == 0 and (batch * seq) % IN_TM == 0
    t = batch * seq

    piece = lambda lo, hi: jnp.swapaxes(w_in[:, :, lo:hi], 1, 2).astype(BF16)
    wt_zm = piece(0, ZM_W)
    wt_gz = piece(_R_GZ0, _R_GZ0 + ZG_W)
    wt_d = piece(_R_DQ0, _R_GZ0)
    wt_cf = jnp.pad(piece(_R_CF0, _R_DQ0), ((0, 0), (0, CF_PAD - C_HEADS), (0, 0)))
    wb = w_branch.astype(BF16)
    wo = w_out.astype(BF16)
    wg = w_ffn_gate.astype(BF16)
    wu = w_ffn_up.astype(BF16)
    wd = w_ffn_down.astype(BF16)
    wa = _block_diag(lru_wa).astype(BF16)
    wx = _block_diag(lru_wx).astype(BF16)
    lvec = jnp.stack([diff_lq1, diff_lk1, diff_lq2, diff_lk2], axis=1).astype(F32)
    bg = b_gate.reshape(depth, N_BRANCH, D_MODEL)
    gf = norm_final.reshape(1, D_MODEL)

    h = x.reshape(t, D_MODEL)
    u = _norm(h, norm_mix[0].reshape(1, D_MODEL))
    for l in range(depth):
        lam_init = 0.8 - 0.6 * math.exp(-0.3 * l)
        zg = _proj(u, wt_gz, l, "proj_gate")
        zm = _proj(u, wt_zm, l, "proj_mix")
        zd, cf = _proj_d(u, wt_d, wt_cf, l)

        ya = _diff_attn(zm, lvec[l], diff_subln[l].reshape(1, 2 * A_HD),
                        jnp.asarray([lam_init, 1.0 - lam_init], F32), batch, seq)
        yb = _lru(zm, lru_conv_w[l], lru_conv_b[l].reshape(1, -1), wa, lru_ba[l].reshape(1, -1), wx,
                  lru_bx[l].reshape(1, -1), lru_lambda[l].reshape(1, -1), l, batch, seq)
        cf_t = cf[:, :C_HEADS].reshape(batch, seq, C_HEADS).transpose(0, 2, 1).reshape(batch * C_HEADS, seq)
        bf_t = jnp.tile(fox_b_f[l], batch).reshape(batch * C_HEADS, 1)
        cum = _fox_cum(cf_t, bf_t).reshape(batch * C_HEADS, 1, seq)
        yc = _fox_attn(zm, cum, batch, seq)
        yd = _swa(zd, swa_sinks[l], batch, seq)

        h, v = _merge(h, ya, yb, yc, yd, zg, bg[l], wb, wo, norm_ffn[l].reshape(1, D_MODEL), l)
        if l == depth - 1:
            (h,) = _ffn(v, h, wg, wu, wd, gf, l, final_norm=True)
        else:
            h, u = _ffn(v, h, wg, wu, wd, norm_mix[l + 1].reshape(1, D_MODEL), l, final_norm=False)
    return h.reshape(batch, seq, D_MODEL)
```

```python
import functools
import math

import jax
import jax.numpy as jnp
import numpy as np
from jax import lax
from jax.experimental import pallas as pl
from jax.experimental.pallas import tpu as pltpu

F32 = jnp.float32
BF16 = jnp.bfloat16

D_MODEL = 2048
N_BRANCH = 4
BRANCH_W = D_MODEL // 4
A_HEADS = 4
A_HD = BRANCH_W // (2 * A_HEADS)
B_BLOCKS = 8
B_CONV = 4
B_C = 8.0
C_HEADS = 4
C_HD = BRANCH_W // C_HEADS
D_HEADS = 8
D_KV = 2
D_GROUP = D_HEADS // D_KV
D_HD = BRANCH_W // D_HEADS
WINDOW = 128
SWA_BLOCK = 128
SWA_QB = 4
D_FF = -(-8 * D_MODEL // (3 * 256)) * 256
RMS_EPS = 1e-6
NEG_INF = -1e30

AQ0 = 0
AK0 = AQ0 + BRANCH_W
AV0 = AK0 + BRANCH_W
BX0 = AV0 + BRANCH_W
BG0 = BX0 + BRANCH_W
CQ0 = BG0 + BRANCH_W
CK0 = CQ0 + BRANCH_W
CV0 = CK0 + BRANCH_W
ZM_W = CV0 + BRANCH_W
DQ0 = 0
DK0 = DQ0 + BRANCH_W
DV0 = DK0 + D_KV * D_HD
ZD_W = DV0 + D_KV * D_HD
ZG_W = N_BRANCH * D_MODEL
CF_PAD = 128

_R_CF0 = ZM_W
_R_DQ0 = _R_CF0 + C_HEADS
_R_GZ0 = _R_DQ0 + ZD_W

V7X_VMEM_LIMIT = 56 * 1024 * 1024

NORM_TM = 1024
IN_TM, IN_TN = 2048, 1024
FLASH_ROWS = 256
A_TQ, A_HPS = 1024, 2
C_TQ, C_HPS = 1024, 4
LOG2E = math.log2(math.e)
LRU_TS = 512
LRU_PAD = LRU_TS // 2
MERGE_TM = 256
FFN_TM, FFN_TF = 512, 512


def _cparams(sem):
    return pltpu.CompilerParams(dimension_semantics=sem, vmem_limit_bytes=V7X_VMEM_LIMIT)


def _rms(x, g):
    return x * lax.rsqrt(jnp.mean(x * x, axis=-1, keepdims=True) + RMS_EPS) * g


def _softplus(y):
    return jnp.maximum(y, 0.0) + jnp.log1p(jnp.exp(-jnp.abs(y)))


def _norm_kernel(x_ref, g_ref, o_ref):
    o_ref[...] = _rms(x_ref[...], g_ref[...]).astype(BF16)


def _norm(x, g):
    t = x.shape[0]
    return pl.pallas_call(
        _norm_kernel,
        grid=(t // NORM_TM,),
        in_specs=[pl.BlockSpec((NORM_TM, D_MODEL), lambda i: (i, 0)), pl.BlockSpec((1, D_MODEL), lambda i: (0, 0))],
        out_specs=pl.BlockSpec((NORM_TM, D_MODEL), lambda i: (i, 0)),
        out_shape=jax.ShapeDtypeStruct((t, D_MODEL), BF16),
        compiler_params=_cparams(("parallel",)),
        name="norm0",
    )(x, g)


_NT = (((1,), (1,)), ((), ()))


def _proj_kernel(u_ref, wt_ref, z_ref):
    z_ref[...] = lax.dot_general(u_ref[...], wt_ref[...], _NT, preferred_element_type=F32).astype(BF16)


def _proj(u, wt, l, name):
    t = u.shape[0]
    n_cols = wt.shape[1]
    return pl.pallas_call(
        _proj_kernel,
        grid=(t // IN_TM, n_cols // IN_TN),
        in_specs=[
            pl.BlockSpec((IN_TM, D_MODEL), lambda i, j: (i, 0)),
            pl.BlockSpec((None, IN_TN, D_MODEL), lambda i, j: (l, j, 0)),
        ],
        out_specs=pl.BlockSpec((IN_TM, IN_TN), lambda i, j: (i, j)),
        out_shape=jax.ShapeDtypeStruct((t, n_cols), BF16),
        compiler_params=_cparams(("parallel", "arbitrary")),
        name=name,
    )(u, wt)


def _proj_d_kernel(u_ref, wt_ref, wcft_ref, z_ref, cf_ref):
    u = u_ref[...]
    z_ref[...] = lax.dot_general(u, wt_ref[...], _NT, preferred_element_type=F32).astype(BF16)
    cf_ref[...] = lax.dot_general(u, wcft_ref[...], _NT, preferred_element_type=F32)


def _proj_d(u, wt, wcft, l):
    t = u.shape[0]
    return pl.pallas_call(
        _proj_d_kernel,
        grid=(t // IN_TM,),
        in_specs=[
            pl.BlockSpec((IN_TM, D_MODEL), lambda i: (i, 0)),
            pl.BlockSpec((None, ZD_W, D_MODEL), lambda i: (l, 0, 0)),
            pl.BlockSpec((None, CF_PAD, D_MODEL), lambda i: (l, 0, 0)),
        ],
        out_specs=[
            pl.BlockSpec((IN_TM, ZD_W), lambda i: (i, 0)),
            pl.BlockSpec((IN_TM, CF_PAD), lambda i: (i, 0)),
        ],
        out_shape=[jax.ShapeDtypeStruct((t, ZD_W), BF16), jax.ShapeDtypeStruct((t, CF_PAD), F32)],
        compiler_params=_cparams(("parallel",)),
        name="proj_d",
    )(u, wt, wcft)


def _tri_tables(n):
    qi = [i for i in range(n) for _ in range(i + 1)]
    kj = [j for i in range(n) for j in range(i + 1)]
    return jnp.asarray(qi, jnp.int32), jnp.asarray(kj, jnp.int32)


def _flash_rows(q_ref, k_ref, v2_sc, bias, m_sc, acc_sc, r0, diag_q0, scale):
    rows = FLASH_ROWS
    nk = k_ref.shape[0] if diag_q0 is None else diag_q0 + rows
    s = lax.dot_general(q_ref[r0:r0 + rows, :], k_ref[0:nk, :], (((1,), (1,)), ((), ())),
                        preferred_element_type=F32)
    s = s * scale + bias[:, 0:nk]
    if diag_q0 is not None:
        qpos = lax.broadcasted_iota(jnp.int32, s.shape, 0) + diag_q0
        kpos = lax.broadcasted_iota(jnp.int32, s.shape, 1)
        s = jnp.where(kpos <= qpos, s, NEG_INF)
    m_prev = m_sc[r0:r0 + rows, :]
    m_new = jnp.maximum(m_prev, jnp.max(s, axis=-1, keepdims=True))
    alpha = jnp.exp2(m_prev - m_new)
    p = jnp.exp2(s - jnp.concatenate([m_new] * (nk // 128), axis=1))
    pv = jnp.dot(p.astype(BF16), v2_sc[0:nk, :], preferred_element_type=F32)
    acc_sc[r0:r0 + rows, :] = jnp.concatenate([alpha, alpha], axis=1) * acc_sc[r0:r0 + rows, :] + pv
    m_sc[r0:r0 + rows, :] = m_new


def _flash_block(q_ref, k_ref, v2_sc, bias, m_sc, acc_sc, n_rows, tq, diag, scale):
    for r0 in range(0, n_rows, FLASH_ROWS):
        _flash_rows(q_ref, k_ref, v2_sc, bias, m_sc, acc_sc, r0, (r0 % tq) if diag else None, scale)


def _flash_init(v2_sc, m_sc, acc_sc):
    v2_sc[:, 128:256] = jnp.ones((v2_sc.shape[0], 128), BF16)
    m_sc[...] = jnp.full(m_sc.shape, -jnp.inf, F32)
    acc_sc[...] = jnp.zeros(acc_sc.shape, F32)


def _diff_attn_kernel(qi_ref, kj_ref, slope_ref, lami_ref, q_ref, k_ref, v_ref, lv_ref, sg_ref, o_ref,
                      q2_sc, v2_sc, m_sc, acc_sc):
    tq = A_TQ
    t = pl.program_id(2)
    i = qi_ref[t]
    j = kj_ref[t]
    heads = [(hh, slice(hh * 128, (hh + 1) * 128)) for hh in range(A_HPS)]

    @pl.when(j == 0)
    def _():
        for hh, cols in heads:
            q = q_ref[:, cols]
            lane = lax.broadcasted_iota(jnp.int32, q.shape, 1)
            qs = q * jnp.asarray(A_HD ** -0.5, BF16)
            zero = jnp.zeros_like(qs)
            q2_sc[hh, 0:tq, :] = jnp.where(lane < A_HD, qs, zero)
            q2_sc[hh, tq:2 * tq, :] = jnp.where(lane >= A_HD, qs, zero)
            _flash_init(v2_sc.at[hh], m_sc.at[hh], acc_sc.at[hh])

    kpos = (lax.broadcasted_iota(jnp.int32, (1, tq), 1) + (j - i) * tq).astype(F32)

    def block(diag):
        for hh, cols in heads:
            v2_sc[hh, :, 0:128] = v_ref[:, cols]
            bias = kpos * (slope_ref[pl.program_id(1) * A_HPS + hh] * LOG2E)
            _flash_block(q2_sc.at[hh], k_ref.at[:, cols], v2_sc.at[hh], bias, m_sc.at[hh], acc_sc.at[hh],
                         2 * tq, tq, diag, LOG2E)

    @pl.when(j < i)
    def _():
        block(False)

    @pl.when(j == i)
    def _():
        block(True)
        lv = lv_ref[...]
        lam = (jnp.exp(jnp.sum(lv[0:1] * lv[1:2], axis=-1, keepdims=True))
               - jnp.exp(jnp.sum(lv[2:3] * lv[3:4], axis=-1, keepdims=True)) + lami_ref[0])
        for hh, cols in heads:
            acc = acc_sc[hh]
            o = acc[0:tq, 0:128] / acc[0:tq, 128:256] - lam * (acc[tq:2 * tq, 0:128] / acc[tq:2 * tq, 128:256])
            o_ref[:, cols] = (_rms(o, sg_ref[...]) * lami_ref[1]).astype(BF16)


def _diff_attn(z, lvec, subln, lam_init, batch, seq):
    nq = seq // A_TQ
    qi, kj = _tri_tables(nq)
    slopes = jnp.asarray(np.exp2(-8.0 * np.arange(1, A_HEADS + 1, dtype=np.float32) / A_HEADS), F32)
    w = 128 * A_HPS
    qc, kc, vc = AQ0 // w, AK0 // w, AV0 // w
    smem = pl.BlockSpec(memory_space=pltpu.SMEM)
    grid_spec = pltpu.PrefetchScalarGridSpec(
        num_scalar_prefetch=2,
        grid=(batch, A_HEADS // A_HPS, int(qi.shape[0])),
        in_specs=[
            smem, smem,
            pl.BlockSpec((A_TQ, w), lambda b, h, t, qi, kj: (b * nq + qi[t], qc + h)),
            pl.BlockSpec((A_TQ, w), lambda b, h, t, qi, kj: (b * nq + kj[t], kc + h)),
            pl.BlockSpec((A_TQ, w), lambda b, h, t, qi, kj: (b * nq + kj[t], vc + h)),
            pl.BlockSpec((4, A_HD), lambda b, h, t, qi, kj: (0, 0)),
            pl.BlockSpec((1, 2 * A_HD), lambda b, h, t, qi, kj: (0, 0)),
        ],
        out_specs=pl.BlockSpec((A_TQ, w), lambda b, h, t, qi, kj: (b * nq + qi[t], h)),
        scratch_shapes=[
            pltpu.VMEM((A_HPS, 2 * A_TQ, 128), BF16),
            pltpu.VMEM((A_HPS, A_TQ, 256), BF16),
            pltpu.VMEM((A_HPS, 2 * A_TQ, 128), F32),
            pltpu.VMEM((A_HPS, 2 * A_TQ, 256), F32),
        ],
    )
    return pl.pallas_call(
        _diff_attn_kernel,
        grid_spec=grid_spec,
        out_shape=jax.ShapeDtypeStruct((batch * seq, BRANCH_W), BF16),
        compiler_params=_cparams(("parallel", "parallel", "arbitrary")),
        name="diff_attn",
    )(qi, kj, slopes, lam_init, z, z, z, lvec, subln)


def _fox_cum_kernel(cf_ref, bf_ref, o_ref):
    x = cf_ref[...] + bf_ref[...]
    logf = jnp.minimum(x, 0.0) - jnp.log1p(jnp.exp(-jnp.abs(x)))
    rows, seq = logf.shape
    lane = lax.broadcasted_iota(jnp.int32, (rows, 128), 1)
    carry = jnp.zeros((rows, 1), F32)
    for c in range(seq // 128):
        y = logf[:, c * 128:(c + 1) * 128]
        sh = 1
        while sh < 128:
            y = y + jnp.where(lane >= sh, pltpu.roll(y, sh, axis=1), 0.0)
            sh *= 2
        y = y + carry
        o_ref[:, c * 128:(c + 1) * 128] = y
        carry = y[:, 127:128]


def _fox_cum(cf_t, bf_t):
    rows, seq = cf_t.shape
    return pl.pallas_call(
        _fox_cum_kernel,
        grid=(1,),
        in_specs=[pl.BlockSpec((rows, seq), lambda i: (0, 0)), pl.BlockSpec((rows, 1), lambda i: (0, 0))],
        out_specs=pl.BlockSpec((rows, seq), lambda i: (0, 0)),
        out_shape=jax.ShapeDtypeStruct((rows, seq), F32),
        compiler_params=_cparams(("arbitrary",)),
        name="fox_cum",
    )(cf_t, bf_t)


def _fox_attn_kernel(qi_ref, kj_ref, q_ref, k_ref, v_ref, cum_ref, o_ref, v2_sc, m_sc, acc_sc):
    tq = C_TQ
    t = pl.program_id(2)
    i = qi_ref[t]
    j = kj_ref[t]
    heads = [(hh, slice(hh * 128, (hh + 1) * 128)) for hh in range(C_HPS)]

    @pl.when(j == 0)
    def _():
        for hh, _ in heads:
            _flash_init(v2_sc.at[hh], m_sc.at[hh], acc_sc.at[hh])

    def block(diag):
        for hh, cols in heads:
            v2_sc[hh, :, 0:128] = v_ref[:, cols]
            _flash_block(q_ref.at[:, cols], k_ref.at[:, cols], v2_sc.at[hh], cum_ref[hh] * (-LOG2E), m_sc.at[hh],
                         acc_sc.at[hh], tq, tq, diag, C_HD ** -0.5 * LOG2E)

    @pl.when(j < i)
    def _():
        block(False)

    @pl.when(j == i)
    def _():
        block(True)
        for hh, cols in heads:
            acc = acc_sc[hh]
            o_ref[:, cols] = (acc[:, 0:128] / acc[:, 128:256]).astype(BF16)


def _fox_attn(z, cum, batch, seq):
    nq = seq // C_TQ
    qi, kj = _tri_tables(nq)
    w = 128 * C_HPS
    qc, kc, vc = CQ0 // w, CK0 // w, CV0 // w
    hg = C_HEADS // C_HPS
    grid_spec = pltpu.PrefetchScalarGridSpec(
        num_scalar_prefetch=2,
        grid=(batch, hg, int(qi.shape[0])),
        in_specs=[
            pl.BlockSpec((C_TQ, w), lambda b, h, t, qi, kj: (b * nq + qi[t], qc + h)),
            pl.BlockSpec((C_TQ, w), lambda b, h, t, qi, kj: (b * nq + kj[t], kc + h)),
            pl.BlockSpec((C_TQ, w), lambda b, h, t, qi, kj: (b * nq + kj[t], vc + h)),
            pl.BlockSpec((C_HPS, 1, C_TQ), lambda b, h, t, qi, kj: (b * hg + h, 0, kj[t])),
        ],
        out_specs=pl.BlockSpec((C_TQ, w), lambda b, h, t, qi, kj: (b * nq + qi[t], h)),
        scratch_shapes=[
            pltpu.VMEM((C_HPS, C_TQ, 256), BF16),
            pltpu.VMEM((C_HPS, C_TQ, 128), F32),
            pltpu.VMEM((C_HPS, C_TQ, 256), F32),
        ],
    )
    return pl.pallas_call(
        _fox_attn_kernel,
        grid_spec=grid_spec,
        out_shape=jax.ShapeDtypeStruct((batch * seq, BRANCH_W), BF16),
        compiler_params=_cparams(("parallel", "parallel", "arbitrary")),
        name="fox_attn",
    )(qi, kj, z, z, z, cum)


def _swa_kernel(sink_ref, q_ref, kc_ref, kp_ref, vc_ref, vp_ref, bias_ref, o_ref):
    blk = SWA_BLOCK
    first = pl.program_id(1) == 0
    swap = lambda x: jnp.concatenate([x[:, D_HD:2 * D_HD], x[:, 0:D_HD]], axis=1)
    ones = jnp.ones((blk * (SWA_QB + 1), 128), BF16)
    kk_all = jnp.concatenate([kp_ref[...], kc_ref[...]], axis=0)
    vv_all = jnp.concatenate([vp_ref[...], vc_ref[...]], axis=0)
    kk_var = (kk_all, swap(kk_all))
    vv_var = (jnp.concatenate([vv_all, ones], axis=1), jnp.concatenate([swap(vv_all), ones], axis=1))
    q_all = q_ref[...] * jnp.asarray(D_HD ** -0.5, BF16)
    q_lane = lax.broadcasted_iota(jnp.int32, (2 * blk, 128), 1)
    o_lane = lax.broadcasted_iota(jnp.int32, (blk, 128), 1)
    for qb in range(SWA_QB):
        keys = slice(qb * blk, (qb + 2) * blk)
        for kv in range(D_KV):
            res = []
            for half in range(2):
                q2 = jnp.concatenate([q_all[qb * blk:(qb + 1) * blk, (2 * kv + tt) * 128:(2 * kv + tt + 1) * 128]
                                      for tt in range(2)], axis=0)
                q2 = jnp.where((q_lane < D_HD) if half == 0 else (q_lane >= D_HD), q2, jnp.zeros_like(q2))
                var = half ^ kv
                s = lax.dot_general(q2, kk_var[var][keys], (((1,), (1,)), ((), ())), preferred_element_type=F32)
                bias = bias_ref[1, kv, half]
                if qb == 0:
                    bias = jnp.where(first, bias_ref[0, kv, half], bias)
                lg = s + bias
                es, sinks = [], []
                for tt in range(2):
                    sink = sink_ref[4 * kv + half + 2 * tt]
                    lg_h = lg[tt * blk:(tt + 1) * blk]
                    m = jnp.maximum(jnp.max(lg_h, axis=-1, keepdims=True), sink)
                    es.append(jnp.exp(lg_h - m))
                    sinks.append(jnp.exp(sink - m))
                pv = jnp.dot(jnp.concatenate(es, axis=0).astype(BF16), vv_var[var][keys],
                             preferred_element_type=F32)
                res.append([pv[tt * blk:(tt + 1) * blk, 0:128] / (pv[tt * blk:(tt + 1) * blk, 128:256] + sinks[tt])
                            for tt in range(2)])
            for tt in range(2):
                tile = jnp.where(o_lane < D_HD, res[0][tt], res[1][tt])
                o_ref[qb * blk:(qb + 1) * blk, (2 * kv + tt) * 128:(2 * kv + tt + 1) * 128] = tile.astype(BF16)


def _swa_bias():
    slopes = np.exp2(-8.0 * np.arange(1, D_HEADS + 1, dtype=np.float32) / D_HEADS).astype(np.float32)
    qi = np.arange(SWA_BLOCK)
    kj = np.arange(2 * SWA_BLOCK) - SWA_BLOCK
    dist = (qi[:, None] - kj[None, :]).astype(np.float32)
    valid = (dist >= 0) & (dist < WINDOW)
    general = np.where(valid[None], -(slopes[:, None, None] * dist[None]), np.float32(NEG_INF))
    first = np.where((kj >= 0)[None, None, :], general, np.float32(NEG_INF))
    per_head = np.stack([first, general]).astype(np.float32)
    table = np.stack([np.stack([np.concatenate([per_head[:, 4 * kv + half], per_head[:, 4 * kv + half + 2]], axis=1)
                                for half in range(2)], axis=1) for kv in range(D_KV)], axis=1)
    return jnp.asarray(table)


def _swa(z, sinks, batch, seq):
    rows = SWA_QB * SWA_BLOCK
    ns = seq // rows
    qc, kc, vc = DQ0 // BRANCH_W, DK0 // 128, DV0 // 128
    cur = lambda b, n: b * ns + n
    prev = lambda b, n: jnp.maximum((b * ns + n) * SWA_QB - 1, 0)
    return pl.pallas_call(
        _swa_kernel,
        grid=(batch, ns),
        in_specs=[
            pl.BlockSpec(memory_space=pltpu.SMEM),
            pl.BlockSpec((rows, BRANCH_W), lambda b, n: (cur(b, n), qc)),
            pl.BlockSpec((rows, 128), lambda b, n: (cur(b, n), kc)),
            pl.BlockSpec((SWA_BLOCK, 128), lambda b, n: (prev(b, n), kc)),
            pl.BlockSpec((rows, 128), lambda b, n: (cur(b, n), vc)),
            pl.BlockSpec((SWA_BLOCK, 128), lambda b, n: (prev(b, n), vc)),
            pl.BlockSpec((2, D_KV, 2, 2 * SWA_BLOCK, 2 * SWA_BLOCK), lambda b, n: (0, 0, 0, 0, 0)),
        ],
        out_specs=pl.BlockSpec((rows, BRANCH_W), lambda b, n: (cur(b, n), 0)),
        out_shape=jax.ShapeDtypeStruct((batch * seq, BRANCH_W), BF16),
        compiler_params=_cparams(("parallel", "arbitrary")),
        name="swa",
    )(sinks, z, z, z, z, z, _swa_bias())


def _lru_kernel(x_ref, g_ref, cw_ref, cb_ref, wa_ref, ba_ref, wx_ref, bx_ref, lam_ref, o_ref,
                xpad_sc, apad_sc, upad_sc, hc_sc):
    ts, pad = LRU_TS, LRU_PAD

    @pl.when(pl.program_id(1) == 0)
    def _():
        xpad_sc[0:8, :] = jnp.zeros((8, BRANCH_W), F32)
        hc_sc[...] = jnp.zeros(hc_sc.shape, F32)
        apad_sc[0:pad, :] = jnp.ones((pad, BRANCH_W), F32)
        upad_sc[0:pad, :] = jnp.zeros((pad, BRANCH_W), F32)

    x = x_ref[...].astype(F32)
    xpad_sc[8:8 + ts, :] = x
    cw = cw_ref[...]
    xc = (cw[3:4] * x + cw[2:3] * xpad_sc[7:7 + ts, :] + cw[1:2] * xpad_sc[6:6 + ts, :]
          + cw[0:1] * xpad_sc[5:5 + ts, :] + cb_ref[...])
    xpad_sc[0:8, :] = x[ts - 8:ts]

    xcb = xc.astype(BF16)
    r = jax.nn.sigmoid(jnp.dot(xcb, wa_ref[...], preferred_element_type=F32) + ba_ref[...])
    gi = jax.nn.sigmoid(jnp.dot(xcb, wx_ref[...], preferred_element_type=F32) + bx_ref[...])
    log_a = (-B_C * r) * _softplus(-lam_ref[...])
    a = jnp.exp(log_a)
    u = jnp.sqrt(1.0 - a * a) * (gi * xc)

    sh = 1
    while sh < ts:
        apad_sc[pad:pad + ts, :] = a
        upad_sc[pad:pad + ts, :] = u
        u = a * upad_sc[pad - sh:pad - sh + ts, :] + u
        a = a * apad_sc[pad - sh:pad - sh + ts, :]
        sh *= 2
    h = a * hc_sc[...] + u
    hc_sc[...] = h[ts - 1:ts]
    g = g_ref[...].astype(F32)
    gelu = g * jax.nn.sigmoid((2.0 * math.sqrt(2.0 / math.pi)) * (g + 0.044715 * (g * g * g)))
    o_ref[...] = (gelu * h).astype(BF16)


def _lru(z, cw, cb, wa, ba, wx, bx, lam, l, batch, seq):
    ns = seq // LRU_TS
    xc_, gc_ = BX0 // BRANCH_W, BG0 // BRANCH_W
    const = lambda shape: pl.BlockSpec(shape, lambda b, s: (0, 0))
    return pl.pallas_call(
        _lru_kernel,
        grid=(batch, ns),
        in_specs=[
            pl.BlockSpec((LRU_TS, BRANCH_W), lambda b, s: (b * ns + s, xc_)),
            pl.BlockSpec((LRU_TS, BRANCH_W), lambda b, s: (b * ns + s, gc_)),
            const((B_CONV, BRANCH_W)), const((1, BRANCH_W)),
            pl.BlockSpec((None, BRANCH_W, BRANCH_W), lambda b, s: (l, 0, 0)), const((1, BRANCH_W)),
            pl.BlockSpec((None, BRANCH_W, BRANCH_W), lambda b, s: (l, 0, 0)), const((1, BRANCH_W)),
            const((1, BRANCH_W)),
        ],
        out_specs=pl.BlockSpec((LRU_TS, BRANCH_W), lambda b, s: (b * ns + s, 0)),
        out_shape=jax.ShapeDtypeStruct((batch * seq, BRANCH_W), BF16),
        scratch_shapes=[
            pltpu.VMEM((LRU_TS + 8, BRANCH_W), F32),
            pltpu.VMEM((LRU_TS + LRU_PAD, BRANCH_W), F32),
            pltpu.VMEM((LRU_TS + LRU_PAD, BRANCH_W), F32),
            pltpu.VMEM((1, BRANCH_W), F32),
        ],
        compiler_params=_cparams(("parallel", "arbitrary")),
        name="lru",
    )(z, z, cw, cb, wa, ba, wx, bx, lam)


def _merge_kernel(h_ref, ya_ref, yb_ref, yc_ref, yd_ref, zg_ref, bg_ref, wb_ref, wo_ref, gn_ref, o_ref, v_ref):
    ys = (ya_ref, yb_ref, yc_ref, yd_ref)
    mixed = None
    for n in range(N_BRANCH):
        proj = jnp.dot(ys[n][...], wb_ref[n], preferred_element_type=F32)
        gate = zg_ref[:, n * D_MODEL:(n + 1) * D_MODEL].astype(F32) + bg_ref[n:n + 1, :]
        term = jax.nn.sigmoid(gate) * proj
        mixed = term if mixed is None else mixed + term
    o = h_ref[...] + jnp.dot(mixed.astype(BF16), wo_ref[...], preferred_element_type=F32)
    o_ref[...] = o
    v_ref[...] = _rms(o, gn_ref[...]).astype(BF16)


def _merge(h, ya, yb, yc, yd, zg, bg, wb, wo, gn, l):
    t = h.shape[0]
    row = lambda shape: pl.BlockSpec(shape, lambda i: (i, 0))
    return pl.pallas_call(
        _merge_kernel,
        grid=(t // MERGE_TM,),
        in_specs=[
            row((MERGE_TM, D_MODEL)),
            row((MERGE_TM, BRANCH_W)), row((MERGE_TM, BRANCH_W)), row((MERGE_TM, BRANCH_W)),
            row((MERGE_TM, BRANCH_W)),
            row((MERGE_TM, ZG_W)),
            pl.BlockSpec((N_BRANCH, D_MODEL), lambda i: (0, 0)),
            pl.BlockSpec((None, N_BRANCH, BRANCH_W, D_MODEL), lambda i: (l, 0, 0, 0), pipeline_mode=pl.Buffered(1)),
            pl.BlockSpec((None, D_MODEL, D_MODEL), lambda i: (l, 0, 0), pipeline_mode=pl.Buffered(1)),
            pl.BlockSpec((1, D_MODEL), lambda i: (0, 0)),
        ],
        out_specs=[row((MERGE_TM, D_MODEL)), row((MERGE_TM, D_MODEL))],
        out_shape=[jax.ShapeDtypeStruct((t, D_MODEL), F32), jax.ShapeDtypeStruct((t, D_MODEL), BF16)],
        compiler_params=_cparams(("parallel",)),
        name="merge",
    )(h, ya, yb, yc, yd, zg, bg, wb, wo, gn)


def _ffn_kernel(v_ref, h_ref, wg_ref, wu_ref, wd_ref, gn_ref, o_ref, *u_ref, final_norm):
    j = pl.program_id(1)

    @pl.when(j == 0)
    def _():
        o_ref[...] = h_ref[...]

    v = v_ref[...]
    a = jnp.dot(v, wg_ref[...], preferred_element_type=F32)
    b = jnp.dot(v, wu_ref[...], preferred_element_type=F32)
    t = (a * jax.nn.sigmoid(a)) * b
    o_ref[...] += jnp.dot(t.astype(BF16), wd_ref[...], preferred_element_type=F32)

    @pl.when(j == pl.num_programs(1) - 1)
    def _():
        y = _rms(o_ref[...], gn_ref[...])
        if final_norm:
            o_ref[...] = y
        else:
            u_ref[0][...] = y.astype(BF16)


def _ffn(v, h, wg, wu, wd, gn, l, final_norm):
    t = h.shape[0]
    row = lambda: pl.BlockSpec((FFN_TM, D_MODEL), lambda i, j: (i, 0))
    out_specs = [row()] if final_norm else [row(), row()]
    out_shape = [jax.ShapeDtypeStruct((t, D_MODEL), F32)]
    if not final_norm:
        out_shape.append(jax.ShapeDtypeStruct((t, D_MODEL), BF16))
    return pl.pallas_call(
        functools.partial(_ffn_kernel, final_norm=final_norm),
        grid=(t // FFN_TM, D_FF // FFN_TF),
        in_specs=[
            row(), row(),
            pl.BlockSpec((None, D_MODEL, FFN_TF), lambda i, j: (l, 0, j)),
            pl.BlockSpec((None, D_MODEL, FFN_TF), lambda i, j: (l, 0, j)),
            pl.BlockSpec((None, FFN_TF, D_MODEL), lambda i, j: (l, j, 0)),
            pl.BlockSpec((1, D_MODEL), lambda i, j: (0, 0)),
        ],
        out_specs=out_specs,
        out_shape=out_shape,
        compiler_params=_cparams(("parallel", "arbitrary")),
        name="ffn_final" if final_norm else "ffn",
    )(v, h, wg, wu, wd, gn)


def _block_diag(w):
    depth, nb, c, d = w.shape
    eye = jnp.eye(nb, dtype=w.dtype)
    return jnp.einsum('lncd,nm->lncmd', w, eye).reshape(depth, nb * c, nb * d)


def kernel(x, norm_mix, w_in, b_gate, diff_lq1, diff_lk1, diff_lq2, diff_lk2, diff_subln, lru_conv_w, lru_conv_b,
           lru_wa, lru_ba, lru_wx, lru_bx, lru_lambda, fox_b_f, swa_sinks, w_branch, w_out, norm_ffn, w_ffn_gate,
           w_ffn_up, w_ffn_down, norm_final):
    batch, seq, d = x.shape
    depth = w_in.shape[0]
    assert d == D_MODEL and seq % max(A_TQ, C_TQ, LRU_TS, SWA_QB * SWA_BLOCK) == 0 and (batch * seq) % IN_TM == 0
    t = batch * seq

    piece = lambda lo, hi: jnp.swapaxes(w_in[:, :, lo:hi], 1, 2).astype(BF16)
    wt_zm = piece(0, ZM_W)
    wt_gz = piece(_R_GZ0, _R_GZ0 + ZG_W)
    wt_d = piece(_R_DQ0, _R_GZ0)
    wt_cf = jnp.pad(piece(_R_CF0, _R_DQ0), ((0, 0), (0, CF_PAD - C_HEADS), (0, 0)))
    wb = w_branch.astype(BF16)
    wo = w_out.astype(BF16)
    wg = w_ffn_gate.astype(BF16)
    wu = w_ffn_up.astype(BF16)
    wd = w_ffn_down.astype(BF16)
    wa = _block_diag(lru_wa).astype(BF16)
    wx = _block_diag(lru_wx).astype(BF16)
    lvec = jnp.stack([diff_lq1, diff_lk1, diff_lq2, diff_lk2], axis=1).astype(F32)
    bg = b_gate.reshape(depth, N_BRANCH, D_MODEL)
    gf = norm_final.reshape(1, D_MODEL)

    h = x.reshape(t, D_MODEL)
    u = _norm(h, norm_mix[0].reshape(1, D_MODEL))
    for l in range(depth):
        lam_init = 0.8 - 0.6 * math.exp(-0.3 * l)
        zg = _proj(u, wt_gz, l, "proj_gate")
        zm = _proj(u, wt_zm, l, "proj_mix")
        zd, cf = _proj_d(u, wt_d, wt_cf, l)

        ya = _diff_attn(zm, lvec[l], diff_subln[l].reshape(1, 2 * A_HD),
                        jnp.asarray([lam_init, 1.0 - lam_init], F32), batch, seq)
        yb = _lru(zm, lru_conv_w[l], lru_conv_b[l].reshape(1, -1), wa, lru_ba[l].reshape(1, -1), wx,
                  lru_bx[l].reshape(1, -1), lru_lambda[l].reshape(1, -1), l, batch, seq)
        cf_t = cf[:, :C_HEADS].reshape(batch, seq, C_HEADS).transpose(0, 2, 1).reshape(batch * C_HEADS, seq)
        bf_t = jnp.tile(fox_b_f[l], batch).reshape(batch * C_HEADS, 1)
        cum = _fox_cum(cf_t, bf_t).reshape(batch * C_HEADS, 1, seq)
        yc = _fox_attn(zm, cum, batch, seq)
        yd = _swa(zd, swa_sinks[l], batch, seq)

        h, v = _merge(h, ya, yb, yc, yd, zg, bg[l], wb, wo, norm_ffn[l].reshape(1, D_MODEL), l)
        if l == depth - 1:
            (h,) = _ffn(v, h, wg, wu, wd, gf, l, final_norm=True)
        else:
            h, u = _ffn(v, h, wg, wu, wd, norm_mix[l + 1].reshape(1, D_MODEL), l, final_norm=False)
    return h.reshape(batch, seq, D_MODEL)
```

```python
import functools
import math

import jax
import jax.numpy as jnp
import numpy as np
from jax import lax
from jax.experimental import pallas as pl
from jax.experimental.pallas import tpu as pltpu

F32 = jnp.float32
BF16 = jnp.bfloat16

D_MODEL = 2048
N_BRANCH = 4
BRANCH_W = D_MODEL // 4
A_HEADS = 4
A_HD = BRANCH_W // (2 * A_HEADS)
B_BLOCKS = 8
B_CONV = 4
B_C = 8.0
C_HEADS = 4
C_HD = BRANCH_W // C_HEADS
D_HEADS = 8
D_KV = 2
D_GROUP = D_HEADS // D_KV
D_HD = BRANCH_W // D_HEADS
WINDOW = 128
SWA_BLOCK = 128
SWA_QB = 4
D_FF = -(-8 * D_MODEL // (3 * 256)) * 256
RMS_EPS = 1e-6
NEG_INF = -1e30

AQ0 = 0
AK0 = AQ0 + BRANCH_W
AV0 = AK0 + BRANCH_W
BX0 = AV0 + BRANCH_W
BG0 = BX0 + BRANCH_W
CQ0 = BG0 + BRANCH_W
CK0 = CQ0 + BRANCH_W
CV0 = CK0 + BRANCH_W
ZM_W = CV0 + BRANCH_W
DQ0 = 0
DK0 = DQ0 + BRANCH_W
DV0 = DK0 + D_KV * D_HD
ZD_W = DV0 + D_KV * D_HD
ZG_W = N_BRANCH * D_MODEL
CF_PAD = 128
WT_GZ0 = ZM_W
WT_D0 = WT_GZ0 + ZG_W
WT_CF0 = WT_D0 + ZD_W

_R_CF0 = ZM_W
_R_DQ0 = _R_CF0 + C_HEADS
_R_GZ0 = _R_DQ0 + ZD_W

V7X_VMEM_LIMIT = 56 * 1024 * 1024

NORM_TM = 1024
IN_TM, IN_TN = 2048, 1024
FLASH_ROWS = 256
A_TQ, A_HPS = 1024, 2
C_TQ, C_HPS = 1024, 4
LOG2E = math.log2(math.e)
LRU_TS = 512
LRU_PAD = LRU_TS // 2
MERGE_TM = 256
FFN_TM, FFN_TF = 512, 512


def _cparams(sem):
    return pltpu.CompilerParams(dimension_semantics=sem, vmem_limit_bytes=V7X_VMEM_LIMIT)


def _rms(x, g):
    return x * lax.rsqrt(jnp.mean(x * x, axis=-1, keepdims=True) + RMS_EPS) * g


def _softplus(y):
    return jnp.maximum(y, 0.0) + jnp.log1p(jnp.exp(-jnp.abs(y)))


def _norm_kernel(x_ref, g_ref, o_ref):
    o_ref[...] = _rms(x_ref[...], g_ref[...]).astype(BF16)


def _norm(x, g):
    t = x.shape[0]
    return pl.pallas_call(
        _norm_kernel,
        grid=(t // NORM_TM,),
        in_specs=[pl.BlockSpec((NORM_TM, D_MODEL), lambda i: (i, 0)), pl.BlockSpec((1, D_MODEL), lambda i: (0, 0))],
        out_specs=pl.BlockSpec((NORM_TM, D_MODEL), lambda i: (i, 0)),
        out_shape=jax.ShapeDtypeStruct((t, D_MODEL), BF16),
        compiler_params=_cparams(("parallel",)),
        name="norm0",
    )(x, g)


_NT = (((1,), (1,)), ((), ()))


def _proj_kernel(u_ref, wt_ref, z_ref):
    z_ref[...] = lax.dot_general(u_ref[...], wt_ref[...], _NT, preferred_element_type=F32).astype(BF16)


def _proj(u, wt, l, row0, n_cols, name):
    t = u.shape[0]
    blk0 = row0 // IN_TN
    return pl.pallas_call(
        _proj_kernel,
        grid=(t // IN_TM, n_cols // IN_TN),
        in_specs=[
            pl.BlockSpec((IN_TM, D_MODEL), lambda i, j: (i, 0)),
            pl.BlockSpec((None, IN_TN, D_MODEL), lambda i, j: (l, blk0 + j, 0)),
        ],
        out_specs=pl.BlockSpec((IN_TM, IN_TN), lambda i, j: (i, j)),
        out_shape=jax.ShapeDtypeStruct((t, n_cols), BF16),
        compiler_params=_cparams(("parallel", "arbitrary")),
        name=name,
    )(u, wt)


def _proj_d_kernel(u_ref, wt_ref, wcft_ref, z_ref, cf_ref):
    u = u_ref[...]
    z_ref[...] = lax.dot_general(u, wt_ref[...], _NT, preferred_element_type=F32).astype(BF16)
    cf_ref[...] = lax.dot_general(u, wcft_ref[...], _NT, preferred_element_type=F32)


def _proj_d(u, wt, l):
    t = u.shape[0]
    return pl.pallas_call(
        _proj_d_kernel,
        grid=(t // IN_TM,),
        in_specs=[
            pl.BlockSpec((IN_TM, D_MODEL), lambda i: (i, 0)),
            pl.BlockSpec((None, ZD_W, D_MODEL), lambda i: (l, WT_D0 // ZD_W, 0)),
            pl.BlockSpec((None, CF_PAD, D_MODEL), lambda i: (l, WT_CF0 // CF_PAD, 0)),
        ],
        out_specs=[
            pl.BlockSpec((IN_TM, ZD_W), lambda i: (i, 0)),
            pl.BlockSpec((IN_TM, CF_PAD), lambda i: (i, 0)),
        ],
        out_shape=[jax.ShapeDtypeStruct((t, ZD_W), BF16), jax.ShapeDtypeStruct((t, CF_PAD), F32)],
        compiler_params=_cparams(("parallel",)),
        name="proj_d",
    )(u, wt, wt)


def _tri_tables(n):
    qi = [i for i in range(n) for _ in range(i + 1)]
    kj = [j for i in range(n) for j in range(i + 1)]
    return jnp.asarray(qi, jnp.int32), jnp.asarray(kj, jnp.int32)


def _flash_rows(q_ref, k_ref, v2_sc, bias, m_sc, acc_sc, r0, diag_q0, scale):
    rows = FLASH_ROWS
    nk = k_ref.shape[0] if diag_q0 is None else diag_q0 + rows
    s = lax.dot_general(q_ref[r0:r0 + rows, :], k_ref[0:nk, :], (((1,), (1,)), ((), ())),
                        preferred_element_type=F32)
    s = s * scale + bias[:, 0:nk]
    if diag_q0 is not None:
        qpos = lax.broadcasted_iota(jnp.int32, s.shape, 0) + diag_q0
        kpos = lax.broadcasted_iota(jnp.int32, s.shape, 1)
        s = jnp.where(kpos <= qpos, s, NEG_INF)
    m_prev = m_sc[r0:r0 + rows, :]
    m_new = jnp.maximum(m_prev, jnp.max(s, axis=-1, keepdims=True))
    alpha = jnp.exp2(m_prev - m_new)
    p = jnp.exp2(s - jnp.concatenate([m_new] * (nk // 128), axis=1))
    pv = jnp.dot(p.astype(BF16), v2_sc[0:nk, :], preferred_element_type=F32)
    acc_sc[r0:r0 + rows, :] = jnp.concatenate([alpha, alpha], axis=1) * acc_sc[r0:r0 + rows, :] + pv
    m_sc[r0:r0 + rows, :] = m_new


def _flash_block(q_ref, k_ref, v2_sc, bias, m_sc, acc_sc, n_rows, tq, diag, scale):
    for r0 in range(0, n_rows, FLASH_ROWS):
        _flash_rows(q_ref, k_ref, v2_sc, bias, m_sc, acc_sc, r0, (r0 % tq) if diag else None, scale)


def _flash_init(v2_sc, m_sc, acc_sc):
    v2_sc[:, 128:256] = jnp.ones((v2_sc.shape[0], 128), BF16)
    m_sc[...] = jnp.full(m_sc.shape, -jnp.inf, F32)
    acc_sc[...] = jnp.zeros(acc_sc.shape, F32)


def _diff_attn_kernel(qi_ref, kj_ref, slope_ref, lami_ref, q_ref, k_ref, v_ref, lv_ref, sg_ref, o_ref,
                      q2_sc, v2_sc, m_sc, acc_sc):
    tq = A_TQ
    t = pl.program_id(2)
    i = qi_ref[t]
    j = kj_ref[t]
    heads = [(hh, slice(hh * 128, (hh + 1) * 128)) for hh in range(A_HPS)]

    @pl.when(j == 0)
    def _():
        for hh, cols in heads:
            q = q_ref[:, cols]
            lane = lax.broadcasted_iota(jnp.int32, q.shape, 1)
            qs = q * jnp.asarray(A_HD ** -0.5, BF16)
            zero = jnp.zeros_like(qs)
            q2_sc[hh, 0:tq, :] = jnp.where(lane < A_HD, qs, zero)
            q2_sc[hh, tq:2 * tq, :] = jnp.where(lane >= A_HD, qs, zero)
            _flash_init(v2_sc.at[hh], m_sc.at[hh], acc_sc.at[hh])

    kpos = (lax.broadcasted_iota(jnp.int32, (1, tq), 1) + (j - i) * tq).astype(F32)

    def block(diag):
        for hh, cols in heads:
            v2_sc[hh, :, 0:128] = v_ref[:, cols]
            bias = kpos * (slope_ref[pl.program_id(1) * A_HPS + hh] * LOG2E)
            _flash_block(q2_sc.at[hh], k_ref.at[:, cols], v2_sc.at[hh], bias, m_sc.at[hh], acc_sc.at[hh],
                         2 * tq, tq, diag, LOG2E)

    @pl.when(j < i)
    def _():
        block(False)

    @pl.when(j == i)
    def _():
        block(True)
        lv = lv_ref[...]
        lam = (jnp.exp(jnp.sum(lv[0:1] * lv[1:2], axis=-1, keepdims=True))
               - jnp.exp(jnp.sum(lv[2:3] * lv[3:4], axis=-1, keepdims=True)) + lami_ref[0])
        for hh, cols in heads:
            acc = acc_sc[hh]
            o = acc[0:tq, 0:128] / acc[0:tq, 128:256] - lam * (acc[tq:2 * tq, 0:128] / acc[tq:2 * tq, 128:256])
            o_ref[:, cols] = (_rms(o, sg_ref[...]) * lami_ref[1]).astype(BF16)


def _diff_attn(z, lvec, subln, lam_init, batch, seq):
    nq = seq // A_TQ
    qi, kj = _tri_tables(nq)
    slopes = jnp.asarray(np.exp2(-8.0 * np.arange(1, A_HEADS + 1, dtype=np.float32) / A_HEADS), F32)
    w = 128 * A_HPS
    qc, kc, vc = AQ0 // w, AK0 // w, AV0 // w
    smem = pl.BlockSpec(memory_space=pltpu.SMEM)
    grid_spec = pltpu.PrefetchScalarGridSpec(
        num_scalar_prefetch=2,
        grid=(batch, A_HEADS // A_HPS, int(qi.shape[0])),
        in_specs=[
            smem, smem,
            pl.BlockSpec((A_TQ, w), lambda b, h, t, qi, kj: (b * nq + qi[t], qc + h)),
            pl.BlockSpec((A_TQ, w), lambda b, h, t, qi, kj: (b * nq + kj[t], kc + h)),
            pl.BlockSpec((A_TQ, w), lambda b, h, t, qi, kj: (b * nq + kj[t], vc + h)),
            pl.BlockSpec((4, A_HD), lambda b, h, t, qi, kj: (0, 0)),
            pl.BlockSpec((1, 2 * A_HD), lambda b, h, t, qi, kj: (0, 0)),
        ],
        out_specs=pl.BlockSpec((A_TQ, w), lambda b, h, t, qi, kj: (b * nq + qi[t], h)),
        scratch_shapes=[
            pltpu.VMEM((A_HPS, 2 * A_TQ, 128), BF16),
            pltpu.VMEM((A_HPS, A_TQ, 256), BF16),
            pltpu.VMEM((A_HPS, 2 * A_TQ, 128), F32),
            pltpu.VMEM((A_HPS, 2 * A_TQ, 256), F32),
        ],
    )
    return pl.pallas_call(
        _diff_attn_kernel,
        grid_spec=grid_spec,
        out_shape=jax.ShapeDtypeStruct((batch * seq, BRANCH_W), BF16),
        compiler_params=_cparams(("parallel", "parallel", "arbitrary")),
        name="diff_attn",
    )(qi, kj, slopes, lam_init, z, z, z, lvec, subln)


def _fox_cum_kernel(cf_ref, bf_ref, o_ref):
    x = cf_ref[...] + bf_ref[...]
    logf = jnp.minimum(x, 0.0) - jnp.log1p(jnp.exp(-jnp.abs(x)))
    rows, seq = logf.shape
    lane = lax.broadcasted_iota(jnp.int32, (rows, 128), 1)
    carry = jnp.zeros((rows, 1), F32)
    for c in range(seq // 128):
        y = logf[:, c * 128:(c + 1) * 128]
        sh = 1
        while sh < 128:
            y = y + jnp.where(lane >= sh, pltpu.roll(y, sh, axis=1), 0.0)
            sh *= 2
        y = y + carry
        o_ref[:, c * 128:(c + 1) * 128] = y
        carry = y[:, 127:128]


def _fox_cum(cf_t, bf_t):
    rows, seq = cf_t.shape
    return pl.pallas_call(
        _fox_cum_kernel,
        grid=(1,),
        in_specs=[pl.BlockSpec((rows, seq), lambda i: (0, 0)), pl.BlockSpec((rows, 1), lambda i: (0, 0))],
        out_specs=pl.BlockSpec((rows, seq), lambda i: (0, 0)),
        out_shape=jax.ShapeDtypeStruct((rows, seq), F32),
        compiler_params=_cparams(("arbitrary",)),
        name="fox_cum",
    )(cf_t, bf_t)


def _fox_attn_kernel(qi_ref, kj_ref, q_ref, k_ref, v_ref, cum_ref, o_ref, v2_sc, m_sc, acc_sc):
    tq = C_TQ
    t = pl.program_id(2)
    i = qi_ref[t]
    j = kj_ref[t]
    heads = [(hh, slice(hh * 128, (hh + 1) * 128)) for hh in range(C_HPS)]

    @pl.when(j == 0)
    def _():
        for hh, _ in heads:
            _flash_init(v2_sc.at[hh], m_sc.at[hh], acc_sc.at[hh])

    def block(diag):
        for hh, cols in heads:
            v2_sc[hh, :, 0:128] = v_ref[:, cols]
            _flash_block(q_ref.at[:, cols], k_ref.at[:, cols], v2_sc.at[hh], cum_ref[hh] * (-LOG2E), m_sc.at[hh],
                         acc_sc.at[hh], tq, tq, diag, C_HD ** -0.5 * LOG2E)

    @pl.when(j < i)
    def _():
        block(False)

    @pl.when(j == i)
    def _():
        block(True)
        for hh, cols in heads:
            acc = acc_sc[hh]
            o_ref[:, cols] = (acc[:, 0:128] / acc[:, 128:256]).astype(BF16)


def _fox_attn(z, cum, batch, seq):
    nq = seq // C_TQ
    qi, kj = _tri_tables(nq)
    w = 128 * C_HPS
    qc, kc, vc = CQ0 // w, CK0 // w, CV0 // w
    hg = C_HEADS // C_HPS
    grid_spec = pltpu.PrefetchScalarGridSpec(
        num_scalar_prefetch=2,
        grid=(batch, hg, int(qi.shape[0])),
        in_specs=[
            pl.BlockSpec((C_TQ, w), lambda b, h, t, qi, kj: (b * nq + qi[t], qc + h)),
            pl.BlockSpec((C_TQ, w), lambda b, h, t, qi, kj: (b * nq + kj[t], kc + h)),
            pl.BlockSpec((C_TQ, w), lambda b, h, t, qi, kj: (b * nq + kj[t], vc + h)),
            pl.BlockSpec((C_HPS, 1, C_TQ), lambda b, h, t, qi, kj: (b * hg + h, 0, kj[t])),
        ],
        out_specs=pl.BlockSpec((C_TQ, w), lambda b, h, t, qi, kj: (b * nq + qi[t], h)),
        scratch_shapes=[
            pltpu.VMEM((C_HPS, C_TQ, 256), BF16),
            pltpu.VMEM((C_HPS, C_TQ, 128), F32),
            pltpu.VMEM((C_HPS, C_TQ, 256), F32),
        ],
    )
    return pl.pallas_call(
        _fox_attn_kernel,
        grid_spec=grid_spec,
        out_shape=jax.ShapeDtypeStruct((batch * seq, BRANCH_W), BF16),
        compiler_params=_cparams(("parallel", "parallel", "arbitrary")),
        name="fox_attn",
    )(qi, kj, z, z, z, cum)


def _swa_kernel(sink_ref, q_ref, kc_ref, kp_ref, vc_ref, vp_ref, bias_ref, o_ref):
    blk = SWA_BLOCK
    first = pl.program_id(1) == 0
    swap = lambda x: jnp.concatenate([x[:, D_HD:2 * D_HD], x[:, 0:D_HD]], axis=1)
    ones = jnp.ones((blk * (SWA_QB + 1), 128), BF16)
    kk_all = jnp.concatenate([kp_ref[...], kc_ref[...]], axis=0)
    vv_all = jnp.concatenate([vp_ref[...], vc_ref[...]], axis=0)
    kk_var = (kk_all, swap(kk_all))
    vv_var = (jnp.concatenate([vv_all, ones], axis=1), jnp.concatenate([swap(vv_all), ones], axis=1))
    q_all = q_ref[...] * jnp.asarray(D_HD ** -0.5, BF16)
    q_lane = lax.broadcasted_iota(jnp.int32, (2 * blk, 128), 1)
    o_lane = lax.broadcasted_iota(jnp.int32, (blk, 128), 1)
    for qb in range(SWA_QB):
        keys = slice(qb * blk, (qb + 2) * blk)
        for kv in range(D_KV):
            res = []
            for half in range(2):
                q2 = jnp.concatenate([q_all[qb * blk:(qb + 1) * blk, (2 * kv + tt) * 128:(2 * kv + tt + 1) * 128]
                                      for tt in range(2)], axis=0)
                q2 = jnp.where((q_lane < D_HD) if half == 0 else (q_lane >= D_HD), q2, jnp.zeros_like(q2))
                var = half ^ kv
                s = lax.dot_general(q2, kk_var[var][keys], (((1,), (1,)), ((), ())), preferred_element_type=F32)
                bias = bias_ref[1, kv, half]
                if qb == 0:
                    bias = jnp.where(first, bias_ref[0, kv, half], bias)
                lg = s + bias
                es, sinks = [], []
                for tt in range(2):
                    sink = sink_ref[4 * kv + half + 2 * tt]
                    lg_h = lg[tt * blk:(tt + 1) * blk]
                    m = jnp.maximum(jnp.max(lg_h, axis=-1, keepdims=True), sink)
                    es.append(jnp.exp(lg_h - m))
                    sinks.append(jnp.exp(sink - m))
                pv = jnp.dot(jnp.concatenate(es, axis=0).astype(BF16), vv_var[var][keys],
                             preferred_element_type=F32)
                res.append([pv[tt * blk:(tt + 1) * blk, 0:128] / (pv[tt * blk:(tt + 1) * blk, 128:256] + sinks[tt])
                            for tt in range(2)])
            for tt in range(2):
                tile = jnp.where(o_lane < D_HD, res[0][tt], res[1][tt])
                o_ref[qb * blk:(qb + 1) * blk, (2 * kv + tt) * 128:(2 * kv + tt + 1) * 128] = tile.astype(BF16)


def _swa_bias():
    slopes = np.exp2(-8.0 * np.arange(1, D_HEADS + 1, dtype=np.float32) / D_HEADS).astype(np.float32)
    qi = np.arange(SWA_BLOCK)
    kj = np.arange(2 * SWA_BLOCK) - SWA_BLOCK
    dist = (qi[:, None] - kj[None, :]).astype(np.float32)
    valid = (dist >= 0) & (dist < WINDOW)
    general = np.where(valid[None], -(slopes[:, None, None] * dist[None]), np.float32(NEG_INF))
    first = np.where((kj >= 0)[None, None, :], general, np.float32(NEG_INF))
    per_head = np.stack([first, general]).astype(np.float32)
    table = np.stack([np.stack([np.concatenate([per_head[:, 4 * kv + half], per_head[:, 4 * kv + half + 2]], axis=1)
                                for half in range(2)], axis=1) for kv in range(D_KV)], axis=1)
    return jnp.asarray(table)


def _swa(z, sinks, batch, seq):
    rows = SWA_QB * SWA_BLOCK
    ns = seq // rows
    qc, kc, vc = DQ0 // BRANCH_W, DK0 // 128, DV0 // 128
    cur = lambda b, n: b * ns + n
    prev = lambda b, n: jnp.maximum((b * ns + n) * SWA_QB - 1, 0)
    return pl.pallas_call(
        _swa_kernel,
        grid=(batch, ns),
        in_specs=[
            pl.BlockSpec(memory_space=pltpu.SMEM),
            pl.BlockSpec((rows, BRANCH_W), lambda b, n: (cur(b, n), qc)),
            pl.BlockSpec((rows, 128), lambda b, n: (cur(b, n), kc)),
            pl.BlockSpec((SWA_BLOCK, 128), lambda b, n: (prev(b, n), kc)),
            pl.BlockSpec((rows, 128), lambda b, n: (cur(b, n), vc)),
            pl.BlockSpec((SWA_BLOCK, 128), lambda b, n: (prev(b, n), vc)),
            pl.BlockSpec((2, D_KV, 2, 2 * SWA_BLOCK, 2 * SWA_BLOCK), lambda b, n: (0, 0, 0, 0, 0)),
        ],
        out_specs=pl.BlockSpec((rows, BRANCH_W), lambda b, n: (cur(b, n), 0)),
        out_shape=jax.ShapeDtypeStruct((batch * seq, BRANCH_W), BF16),
        compiler_params=_cparams(("parallel", "arbitrary")),
        name="swa",
    )(sinks, z, z, z, z, z, _swa_bias())


def _lru_kernel(x_ref, g_ref, cw_ref, cb_ref, wa_ref, ba_ref, wx_ref, bx_ref, lam_ref, o_ref,
                xpad_sc, apad_sc, upad_sc, hc_sc):
    ts, pad = LRU_TS, LRU_PAD

    @pl.when(pl.program_id(1) == 0)
    def _():
        xpad_sc[0:8, :] = jnp.zeros((8, BRANCH_W), F32)
        hc_sc[...] = jnp.zeros(hc_sc.shape, F32)
        apad_sc[0:pad, :] = jnp.ones((pad, BRANCH_W), F32)
        upad_sc[0:pad, :] = jnp.zeros((pad, BRANCH_W), F32)

    x = x_ref[...].astype(F32)
    xpad_sc[8:8 + ts, :] = x
    cw = cw_ref[...]
    xc = (cw[3:4] * x + cw[2:3] * xpad_sc[7:7 + ts, :] + cw[1:2] * xpad_sc[6:6 + ts, :]
          + cw[0:1] * xpad_sc[5:5 + ts, :] + cb_ref[...])
    xpad_sc[0:8, :] = x[ts - 8:ts]

    xcb = xc.astype(BF16)
    r = jax.nn.sigmoid(jnp.dot(xcb, wa_ref[...], preferred_element_type=F32) + ba_ref[...])
    gi = jax.nn.sigmoid(jnp.dot(xcb, wx_ref[...], preferred_element_type=F32) + bx_ref[...])
    log_a = (-B_C * r) * _softplus(-lam_ref[...])
    a = jnp.exp(log_a)
    u = jnp.sqrt(1.0 - a * a) * (gi * xc)

    sh = 1
    while sh < ts:
        apad_sc[pad:pad + ts, :] = a
        upad_sc[pad:pad + ts, :] = u
        u = a * upad_sc[pad - sh:pad - sh + ts, :] + u
        a = a * apad_sc[pad - sh:pad - sh + ts, :]
        sh *= 2
    h = a * hc_sc[...] + u
    hc_sc[...] = h[ts - 1:ts]
    g = g_ref[...].astype(F32)
    gelu = g * jax.nn.sigmoid((2.0 * math.sqrt(2.0 / math.pi)) * (g + 0.044715 * (g * g * g)))
    o_ref[...] = (gelu * h).astype(BF16)


def _lru(z, cw, cb, wa, ba, wx, bx, lam, l, batch, seq):
    ns = seq // LRU_TS
    xc_, gc_ = BX0 // BRANCH_W, BG0 // BRANCH_W
    const = lambda shape: pl.BlockSpec(shape, lambda b, s: (0, 0))
    return pl.pallas_call(
        _lru_kernel,
        grid=(batch, ns),
        in_specs=[
            pl.BlockSpec((LRU_TS, BRANCH_W), lambda b, s: (b * ns + s, xc_)),
            pl.BlockSpec((LRU_TS, BRANCH_W), lambda b, s: (b * ns + s, gc_)),
            const((B_CONV, BRANCH_W)), const((1, BRANCH_W)),
            pl.BlockSpec((None, BRANCH_W, BRANCH_W), lambda b, s: (l, 0, 0)), const((1, BRANCH_W)),
            pl.BlockSpec((None, BRANCH_W, BRANCH_W), lambda b, s: (l, 0, 0)), const((1, BRANCH_W)),
            const((1, BRANCH_W)),
        ],
        out_specs=pl.BlockSpec((LRU_TS, BRANCH_W), lambda b, s: (b * ns + s, 0)),
        out_shape=jax.ShapeDtypeStruct((batch * seq, BRANCH_W), BF16),
        scratch_shapes=[
            pltpu.VMEM((LRU_TS + 8, BRANCH_W), F32),
            pltpu.VMEM((LRU_TS + LRU_PAD, BRANCH_W), F32),
            pltpu.VMEM((LRU_TS + LRU_PAD, BRANCH_W), F32),
            pltpu.VMEM((1, BRANCH_W), F32),
        ],
        compiler_params=_cparams(("parallel", "arbitrary")),
        name="lru",
    )(z, z, cw, cb, wa, ba, wx, bx, lam)


def _merge_kernel(h_ref, ya_ref, yb_ref, yc_ref, yd_ref, zg_ref, bg_ref, wb_ref, wo_ref, gn_ref, o_ref, v_ref):
    ys = (ya_ref, yb_ref, yc_ref, yd_ref)
    mixed = None
    for n in range(N_BRANCH):
        proj = jnp.dot(ys[n][...], wb_ref[n], preferred_element_type=F32)
        gate = zg_ref[:, n * D_MODEL:(n + 1) * D_MODEL].astype(F32) + bg_ref[n:n + 1, :]
        term = jax.nn.sigmoid(gate) * proj
        mixed = term if mixed is None else mixed + term
    o = h_ref[...] + jnp.dot(mixed.astype(BF16), wo_ref[...], preferred_element_type=F32)
    o_ref[...] = o
    v_ref[...] = _rms(o, gn_ref[...]).astype(BF16)


def _merge(h, ya, yb, yc, yd, zg, bg, wb, wo, gn, l):
    t = h.shape[0]
    row = lambda shape: pl.BlockSpec(shape, lambda i: (i, 0))
    return pl.pallas_call(
        _merge_kernel,
        grid=(t // MERGE_TM,),
        in_specs=[
            row((MERGE_TM, D_MODEL)),
            row((MERGE_TM, BRANCH_W)), row((MERGE_TM, BRANCH_W)), row((MERGE_TM, BRANCH_W)),
            row((MERGE_TM, BRANCH_W)),
            row((MERGE_TM, ZG_W)),
            pl.BlockSpec((N_BRANCH, D_MODEL), lambda i: (0, 0)),
            pl.BlockSpec((None, N_BRANCH, BRANCH_W, D_MODEL), lambda i: (l, 0, 0, 0), pipeline_mode=pl.Buffered(1)),
            pl.BlockSpec((None, D_MODEL, D_MODEL), lambda i: (l, 0, 0), pipeline_mode=pl.Buffered(1)),
            pl.BlockSpec((1, D_MODEL), lambda i: (0, 0)),
        ],
        out_specs=[row((MERGE_TM, D_MODEL)), row((MERGE_TM, D_MODEL))],
        out_shape=[jax.ShapeDtypeStruct((t, D_MODEL), F32), jax.ShapeDtypeStruct((t, D_MODEL), BF16)],
        compiler_params=_cparams(("parallel",)),
        name="merge",
    )(h, ya, yb, yc, yd, zg, bg, wb, wo, gn)


def _ffn_kernel(v_ref, h_ref, wg_ref, wu_ref, wd_ref, gn_ref, o_ref, *u_ref, final_norm):
    j = pl.program_id(1)

    @pl.when(j == 0)
    def _():
        o_ref[...] = h_ref[...]

    v = v_ref[...]
    a = jnp.dot(v, wg_ref[...], preferred_element_type=F32)
    b = jnp.dot(v, wu_ref[...], preferred_element_type=F32)
    t = (a * jax.nn.sigmoid(a)) * b
    o_ref[...] += jnp.dot(t.astype(BF16), wd_ref[...], preferred_element_type=F32)

    @pl.when(j == pl.num_programs(1) - 1)
    def _():
        y = _rms(o_ref[...], gn_ref[...])
        if final_norm:
            o_ref[...] = y
        else:
            u_ref[0][...] = y.astype(BF16)


def _ffn(v, h, wg, wu, wd, gn, l, final_norm):
    t = h.shape[0]
    row = lambda: pl.BlockSpec((FFN_TM, D_MODEL), lambda i, j: (i, 0))
    out_specs = [row()] if final_norm else [row(), row()]
    out_shape = [jax.ShapeDtypeStruct((t, D_MODEL), F32)]
    if not final_norm:
        out_shape.append(jax.ShapeDtypeStruct((t, D_MODEL), BF16))
    return pl.pallas_call(
        functools.partial(_ffn_kernel, final_norm=final_norm),
        grid=(t // FFN_TM, D_FF // FFN_TF),
        in_specs=[
            row(), row(),
            pl.BlockSpec((None, None, D_MODEL, FFN_TF), lambda i, j: (l, j, 0, 0)),
            pl.BlockSpec((None, None, D_MODEL, FFN_TF), lambda i, j: (l, j, 0, 0)),
            pl.BlockSpec((None, FFN_TF, D_MODEL), lambda i, j: (l, j, 0)),
            pl.BlockSpec((1, D_MODEL), lambda i, j: (0, 0)),
        ],
        out_specs=out_specs,
        out_shape=out_shape,
        compiler_params=_cparams(("parallel", "arbitrary")),
        name="ffn_final" if final_norm else "ffn",
    )(v, h, wg, wu, wd, gn)


def _block_diag(w):
    depth, nb, c, d = w.shape
    eye = jnp.eye(nb, dtype=w.dtype)
    return jnp.einsum('lncd,nm->lncmd', w, eye).reshape(depth, nb * c, nb * d)


def kernel(x, norm_mix, w_in, b_gate, diff_lq1, diff_lk1, diff_lq2, diff_lk2, diff_subln, lru_conv_w, lru_conv_b,
           lru_wa, lru_ba, lru_wx, lru_bx, lru_lambda, fox_b_f, swa_sinks, w_branch, w_out, norm_ffn, w_ffn_gate,
           w_ffn_up, w_ffn_down, norm_final):
    batch, seq, d = x.shape
    depth = w_in.shape[0]
    assert d == D_MODEL and seq % max(A_TQ, C_TQ, LRU_TS, SWA_QB * SWA_BLOCK) == 0 and (batch * seq) % IN_TM == 0
    t = batch * seq

    piece = lambda lo, hi: jnp.swapaxes(w_in[:, :, lo:hi], 1, 2)
    wt = jnp.concatenate([piece(0, ZM_W), piece(_R_GZ0, _R_GZ0 + ZG_W), piece(_R_DQ0, _R_GZ0), piece(_R_CF0, _R_DQ0),
                          jnp.zeros((depth, CF_PAD - C_HEADS, D_MODEL), F32)], axis=1).astype(BF16)
    wb = w_branch.astype(BF16)
    wo = w_out.astype(BF16)
    col_blocks = lambda w: w.reshape(depth, D_MODEL, D_FF // FFN_TF, FFN_TF).transpose(0, 2, 1, 3).astype(BF16)
    wg = col_blocks(w_ffn_gate)
    wu = col_blocks(w_ffn_up)
    wd = w_ffn_down.astype(BF16)
    wa = _block_diag(lru_wa).astype(BF16)
    wx = _block_diag(lru_wx).astype(BF16)
    lvec = jnp.stack([diff_lq1, diff_lk1, diff_lq2, diff_lk2], axis=1).astype(F32)
    bg = b_gate.reshape(depth, N_BRANCH, D_MODEL)
    gf = norm_final.reshape(1, D_MODEL)

    h = x.reshape(t, D_MODEL)
    u = _norm(h, norm_mix[0].reshape(1, D_MODEL))
    for l in range(depth):
        lam_init = 0.8 - 0.6 * math.exp(-0.3 * l)
        zg = _proj(u, wt, l, WT_GZ0, ZG_W, "proj_gate")
        zm = _proj(u, wt, l, 0, ZM_W, "proj_mix")
        zd, cf = _proj_d(u, wt, l)

        ya = _diff_attn(zm, lvec[l], diff_subln[l].reshape(1, 2 * A_HD),
                        jnp.asarray([lam_init, 1.0 - lam_init], F32), batch, seq)
        yb = _lru(zm, lru_conv_w[l], lru_conv_b[l].reshape(1, -1), wa, lru_ba[l].reshape(1, -1), wx,
                  lru_bx[l].reshape(1, -1), lru_lambda[l].reshape(1, -1), l, batch, seq)
        cf_t = cf[:, :C_HEADS].reshape(batch, seq, C_HEADS).transpose(0, 2, 1).reshape(batch * C_HEADS, seq)
        bf_t = jnp.tile(fox_b_f[l], batch).reshape(batch * C_HEADS, 1)
        cum = _fox_cum(cf_t, bf_t).reshape(batch * C_HEADS, 1, seq)
        yc = _fox_attn(zm, cum, batch, seq)
        yd = _swa(zd, swa_sinks[l], batch, seq)

        h, v = _merge(h, ya, yb, yc, yd, zg, bg[l], wb, wo, norm_ffn[l].reshape(1, D_MODEL), l)
        if l == depth - 1:
            (h,) = _ffn(v, h, wg, wu, wd, gf, l, final_norm=True)
        else:
            h, u = _ffn(v, h, wg, wu, wd, norm_mix[l + 1].reshape(1, D_MODEL), l, final_norm=False)
    return h.reshape(batch, seq, D_MODEL)
```

```python
import functools
import math

import jax
import jax.numpy as jnp
import numpy as np
from jax import lax
from jax.experimental import pallas as pl
from jax.experimental.pallas import tpu as pltpu

F32 = jnp.float32
BF16 = jnp.bfloat16

D_MODEL = 2048
N_BRANCH = 4
BRANCH_W = D_MODEL // 4
A_HEADS = 4
A_HD = BRANCH_W // (2 * A_HEADS)
B_BLOCKS = 8
B_CONV = 4
B_C = 8.0
C_HEADS = 4
C_HD = BRANCH_W // C_HEADS
D_HEADS = 8
D_KV = 2
D_GROUP = D_HEADS // D_KV
D_HD = BRANCH_W // D_HEADS
WINDOW = 128
SWA_BLOCK = 128
SWA_QB = 4
D_FF = -(-8 * D_MODEL // (3 * 256)) * 256
RMS_EPS = 1e-6
NEG_INF = -1e30

AQ0 = 0
AK0 = AQ0 + BRANCH_W
AV0 = AK0 + BRANCH_W
BX0 = AV0 + BRANCH_W
BG0 = BX0 + BRANCH_W
CQ0 = BG0 + BRANCH_W
CK0 = CQ0 + BRANCH_W
CV0 = CK0 + BRANCH_W
ZM_W = CV0 + BRANCH_W
DQ0 = 0
DK0 = DQ0 + BRANCH_W
DV0 = DK0 + D_KV * D_HD
ZD_W = DV0 + D_KV * D_HD
ZG_W = N_BRANCH * D_MODEL
CF_PAD = 128

_R_CF0 = ZM_W
_R_DQ0 = _R_CF0 + C_HEADS
_R_GZ0 = _R_DQ0 + ZD_W

V7X_VMEM_LIMIT = 56 * 1024 * 1024

NORM_TM = 1024
IN_TM, IN_TN = 2048, 1024
FLASH_ROWS = 256
A_TQ, A_HPS = 1024, 2
C_TQ, C_HPS = 1024, 4
LOG2E = math.log2(math.e)
LRU_TS = 512
LRU_PAD = LRU_TS // 2
MERGE_TM = 256
FFN_TM, FFN_TF = 512, 512


def _cparams(sem):
    return pltpu.CompilerParams(dimension_semantics=sem, vmem_limit_bytes=V7X_VMEM_LIMIT)


def _rms(x, g):
    return x * lax.rsqrt(jnp.mean(x * x, axis=-1, keepdims=True) + RMS_EPS) * g


def _softplus(y):
    return jnp.maximum(y, 0.0) + jnp.log1p(jnp.exp(-jnp.abs(y)))


def _norm_kernel(x_ref, g_ref, o_ref):
    o_ref[...] = _rms(x_ref[...], g_ref[...]).astype(BF16)


def _norm(x, g):
    t = x.shape[0]
    return pl.pallas_call(
        _norm_kernel,
        grid=(t // NORM_TM,),
        in_specs=[pl.BlockSpec((NORM_TM, D_MODEL), lambda i: (i, 0)), pl.BlockSpec((1, D_MODEL), lambda i: (0, 0))],
        out_specs=pl.BlockSpec((NORM_TM, D_MODEL), lambda i: (i, 0)),
        out_shape=jax.ShapeDtypeStruct((t, D_MODEL), BF16),
        compiler_params=_cparams(("parallel",)),
        name="norm0",
    )(x, g)


_NT = (((1,), (1,)), ((), ()))


def _proj_kernel(u_ref, wt_ref, z_ref):
    z_ref[...] = lax.dot_general(u_ref[...], wt_ref[...], _NT, preferred_element_type=F32).astype(BF16)


def _proj(u, wt, l, name):
    t = u.shape[0]
    n_cols = wt.shape[1]
    return pl.pallas_call(
        _proj_kernel,
        grid=(t // IN_TM, n_cols // IN_TN),
        in_specs=[
            pl.BlockSpec((IN_TM, D_MODEL), lambda i, j: (i, 0)),
            pl.BlockSpec((None, IN_TN, D_MODEL), lambda i, j: (l, j, 0)),
        ],
        out_specs=pl.BlockSpec((IN_TM, IN_TN), lambda i, j: (i, j)),
        out_shape=jax.ShapeDtypeStruct((t, n_cols), BF16),
        compiler_params=_cparams(("parallel", "arbitrary")),
        name=name,
    )(u, wt)


def _proj_d_kernel(u_ref, wt_ref, wcft_ref, z_ref, cf_ref):
    u = u_ref[...]
    z_ref[...] = lax.dot_general(u, wt_ref[...], _NT, preferred_element_type=F32).astype(BF16)
    cf_ref[...] = lax.dot_general(u, wcft_ref[...], _NT, preferred_element_type=F32)


def _proj_d(u, wt, wcft, l):
    t = u.shape[0]
    return pl.pallas_call(
        _proj_d_kernel,
        grid=(t // IN_TM,),
        in_specs=[
            pl.BlockSpec((IN_TM, D_MODEL), lambda i: (i, 0)),
            pl.BlockSpec((None, ZD_W, D_MODEL), lambda i: (l, 0, 0)),
            pl.BlockSpec((None, CF_PAD, D_MODEL), lambda i: (l, 0, 0)),
        ],
        out_specs=[
            pl.BlockSpec((IN_TM, ZD_W), lambda i: (i, 0)),
            pl.BlockSpec((IN_TM, CF_PAD), lambda i: (i, 0)),
        ],
        out_shape=[jax.ShapeDtypeStruct((t, ZD_W), BF16), jax.ShapeDtypeStruct((t, CF_PAD), F32)],
        compiler_params=_cparams(("parallel",)),
        name="proj_d",
    )(u, wt, wcft)


def _tri_tables(n):
    qi = [i for i in range(n) for _ in range(i + 1)]
    kj = [j for i in range(n) for j in range(i + 1)]
    return jnp.asarray(qi, jnp.int32), jnp.asarray(kj, jnp.int32)


def _flash_rows(q_ref, k_ref, v2_sc, bias, m_sc, acc_sc, r0, diag_q0, scale):
    rows = FLASH_ROWS
    nk = k_ref.shape[0] if diag_q0 is None else diag_q0 + rows
    s = lax.dot_general(q_ref[r0:r0 + rows, :], k_ref[0:nk, :], (((1,), (1,)), ((), ())),
                        preferred_element_type=F32)
    s = s * scale + bias[:, 0:nk]
    if diag_q0 is not None:
        qpos = lax.broadcasted_iota(jnp.int32, s.shape, 0) + diag_q0
        kpos = lax.broadcasted_iota(jnp.int32, s.shape, 1)
        s = jnp.where(kpos <= qpos, s, NEG_INF)
    m_prev = m_sc[r0:r0 + rows, :]
    m_new = jnp.maximum(m_prev, jnp.max(s, axis=-1, keepdims=True))
    alpha = jnp.exp2(m_prev - m_new)
    p = jnp.exp2(s - jnp.concatenate([m_new] * (nk // 128), axis=1))
    pv = jnp.dot(p.astype(BF16), v2_sc[0:nk, :], preferred_element_type=F32)
    acc_sc[r0:r0 + rows, :] = jnp.concatenate([alpha, alpha], axis=1) * acc_sc[r0:r0 + rows, :] + pv
    m_sc[r0:r0 + rows, :] = m_new


def _flash_block(q_ref, k_ref, v2_sc, bias, m_sc, acc_sc, n_rows, tq, diag, scale):
    for r0 in range(0, n_rows, FLASH_ROWS):
        _flash_rows(q_ref, k_ref, v2_sc, bias, m_sc, acc_sc, r0, (r0 % tq) if diag else None, scale)


def _flash_init(v2_sc, m_sc, acc_sc):
    v2_sc[:, 128:256] = jnp.ones((v2_sc.shape[0], 128), BF16)
    m_sc[...] = jnp.full(m_sc.shape, -jnp.inf, F32)
    acc_sc[...] = jnp.zeros(acc_sc.shape, F32)


def _diff_attn_kernel(qi_ref, kj_ref, slope_ref, lami_ref, q_ref, k_ref, v_ref, lv_ref, sg_ref, o_ref,
                      q2_sc, v2_sc, m_sc, acc_sc):
    tq = A_TQ
    t = pl.program_id(2)
    i = qi_ref[t]
    j = kj_ref[t]
    heads = [(hh, slice(hh * 128, (hh + 1) * 128)) for hh in range(A_HPS)]

    @pl.when(j == 0)
    def _():
        for hh, cols in heads:
            q = q_ref[:, cols]
            lane = lax.broadcasted_iota(jnp.int32, q.shape, 1)
            qs = q * jnp.asarray(A_HD ** -0.5, BF16)
            zero = jnp.zeros_like(qs)
            q2_sc[hh, 0:tq, :] = jnp.where(lane < A_HD, qs, zero)
            q2_sc[hh, tq:2 * tq, :] = jnp.where(lane >= A_HD, qs, zero)
            _flash_init(v2_sc.at[hh], m_sc.at[hh], acc_sc.at[hh])

    kpos = (lax.broadcasted_iota(jnp.int32, (1, tq), 1) + (j - i) * tq).astype(F32)

    def block(diag):
        for hh, cols in heads:
            v2_sc[hh, :, 0:128] = v_ref[:, cols]
            bias = kpos * (slope_ref[pl.program_id(1) * A_HPS + hh] * LOG2E)
            _flash_block(q2_sc.at[hh], k_ref.at[:, cols], v2_sc.at[hh], bias, m_sc.at[hh], acc_sc.at[hh],
                         2 * tq, tq, diag, LOG2E)

    @pl.when(j < i)
    def _():
        block(False)

    @pl.when(j == i)
    def _():
        block(True)
        lv = lv_ref[...]
        lam = (jnp.exp(jnp.sum(lv[0:1] * lv[1:2], axis=-1, keepdims=True))
               - jnp.exp(jnp.sum(lv[2:3] * lv[3:4], axis=-1, keepdims=True)) + lami_ref[0])
        for hh, cols in heads:
            acc = acc_sc[hh]
            o = acc[0:tq, 0:128] / acc[0:tq, 128:256] - lam * (acc[tq:2 * tq, 0:128] / acc[tq:2 * tq, 128:256])
            o_ref[:, cols] = (_rms(o, sg_ref[...]) * lami_ref[1]).astype(BF16)


def _diff_attn(z, lvec, subln, lam_init, batch, seq):
    nq = seq // A_TQ
    qi, kj = _tri_tables(nq)
    slopes = jnp.asarray(np.exp2(-8.0 * np.arange(1, A_HEADS + 1, dtype=np.float32) / A_HEADS), F32)
    w = 128 * A_HPS
    qc, kc, vc = AQ0 // w, AK0 // w, AV0 // w
    smem = pl.BlockSpec(memory_space=pltpu.SMEM)
    grid_spec = pltpu.PrefetchScalarGridSpec(
        num_scalar_prefetch=2,
        grid=(batch, A_HEADS // A_HPS, int(qi.shape[0])),
        in_specs=[
            smem, smem,
            pl.BlockSpec((A_TQ, w), lambda b, h, t, qi, kj: (b * nq + qi[t], qc + h)),
            pl.BlockSpec((A_TQ, w), lambda b, h, t, qi, kj: (b * nq + kj[t], kc + h)),
            pl.BlockSpec((A_TQ, w), lambda b, h, t, qi, kj: (b * nq + kj[t], vc + h)),
            pl.BlockSpec((4, A_HD), lambda b, h, t, qi, kj: (0, 0)),
            pl.BlockSpec((1, 2 * A_HD), lambda b, h, t, qi, kj: (0, 0)),
        ],
        out_specs=pl.BlockSpec((A_TQ, w), lambda b, h, t, qi, kj: (b * nq + qi[t], h)),
        scratch_shapes=[
            pltpu.VMEM((A_HPS, 2 * A_TQ, 128), BF16),
            pltpu.VMEM((A_HPS, A_TQ, 256), BF16),
            pltpu.VMEM((A_HPS, 2 * A_TQ, 128), F32),
            pltpu.VMEM((A_HPS, 2 * A_TQ, 256), F32),
        ],
    )
    return pl.pallas_call(
        _diff_attn_kernel,
        grid_spec=grid_spec,
        out_shape=jax.ShapeDtypeStruct((batch * seq, BRANCH_W), BF16),
        compiler_params=_cparams(("parallel", "parallel", "arbitrary")),
        name="diff_attn",
    )(qi, kj, slopes, lam_init, z, z, z, lvec, subln)


def _fox_cum_kernel(cf_ref, bf_ref, o_ref):
    x = cf_ref[...] + bf_ref[...]
    logf = jnp.minimum(x, 0.0) - jnp.log1p(jnp.exp(-jnp.abs(x)))
    rows, seq = logf.shape
    lane = lax.broadcasted_iota(jnp.int32, (rows, 128), 1)
    carry = jnp.zeros((rows, 1), F32)
    for c in range(seq // 128):
        y = logf[:, c * 128:(c + 1) * 128]
        sh = 1
        while sh < 128:
            y = y + jnp.where(lane >= sh, pltpu.roll(y, sh, axis=1), 0.0)
            sh *= 2
        y = y + carry
        o_ref[:, c * 128:(c + 1) * 128] = y
        carry = y[:, 127:128]


def _fox_cum(cf_t, bf_t):
    rows, seq = cf_t.shape
    return pl.pallas_call(
        _fox_cum_kernel,
        grid=(1,),
        in_specs=[pl.BlockSpec((rows, seq), lambda i: (0, 0)), pl.BlockSpec((rows, 1), lambda i: (0, 0))],
        out_specs=pl.BlockSpec((rows, seq), lambda i: (0, 0)),
        out_shape=jax.ShapeDtypeStruct((rows, seq), F32),
        compiler_params=_cparams(("arbitrary",)),
        name="fox_cum",
    )(cf_t, bf_t)


def _fox_attn_kernel(qi_ref, kj_ref, q_ref, k_ref, v_ref, cum_ref, o_ref, v2_sc, m_sc, acc_sc):
    tq = C_TQ
    t = pl.program_id(2)
    i = qi_ref[t]
    j = kj_ref[t]
    heads = [(hh, slice(hh * 128, (hh + 1) * 128)) for hh in range(C_HPS)]

    @pl.when(j == 0)
    def _():
        for hh, _ in heads:
            _flash_init(v2_sc.at[hh], m_sc.at[hh], acc_sc.at[hh])

    def block(diag):
        for hh, cols in heads:
            v2_sc[hh, :, 0:128] = v_ref[:, cols]
            _flash_block(q_ref.at[:, cols], k_ref.at[:, cols], v2_sc.at[hh], cum_ref[hh] * (-LOG2E), m_sc.at[hh],
                         acc_sc.at[hh], tq, tq, diag, C_HD ** -0.5 * LOG2E)

    @pl.when(j < i)
    def _():
        block(False)

    @pl.when(j == i)
    def _():
        block(True)
        for hh, cols in heads:
            acc = acc_sc[hh]
            o_ref[:, cols] = (acc[:, 0:128] / acc[:, 128:256]).astype(BF16)


def _fox_attn(z, cum, batch, seq):
    nq = seq // C_TQ
    qi, kj = _tri_tables(nq)
    w = 128 * C_HPS
    qc, kc, vc = CQ0 // w, CK0 // w, CV0 // w
    hg = C_HEADS // C_HPS
    grid_spec = pltpu.PrefetchScalarGridSpec(
        num_scalar_prefetch=2,
        grid=(batch, hg, int(qi.shape[0])),
        in_specs=[
            pl.BlockSpec((C_TQ, w), lambda b, h, t, qi, kj: (b * nq + qi[t], qc + h)),
            pl.BlockSpec((C_TQ, w), lambda b, h, t, qi, kj: (b * nq + kj[t], kc + h)),
            pl.BlockSpec((C_TQ, w), lambda b, h, t, qi, kj: (b * nq + kj[t], vc + h)),
            pl.BlockSpec((C_HPS, 1, C_TQ), lambda b, h, t, qi, kj: (b * hg + h, 0, kj[t])),
        ],
        out_specs=pl.BlockSpec((C_TQ, w), lambda b, h, t, qi, kj: (b * nq + qi[t], h)),
        scratch_shapes=[
            pltpu.VMEM((C_HPS, C_TQ, 256), BF16),
            pltpu.VMEM((C_HPS, C_TQ, 128), F32),
            pltpu.VMEM((C_HPS, C_TQ, 256), F32),
        ],
    )
    return pl.pallas_call(
        _fox_attn_kernel,
        grid_spec=grid_spec,
        out_shape=jax.ShapeDtypeStruct((batch * seq, BRANCH_W), BF16),
        compiler_params=_cparams(("parallel", "parallel", "arbitrary")),
        name="fox_attn",
    )(qi, kj, z, z, z, cum)


def _swa_kernel(sink_ref, q_ref, kc_ref, kp_ref, vc_ref, vp_ref, bias_ref, o_ref):
    blk = SWA_BLOCK
    first = pl.program_id(1) == 0
    swap = lambda x: jnp.concatenate([x[:, D_HD:2 * D_HD], x[:, 0:D_HD]], axis=1)
    ones = jnp.ones((blk * (SWA_QB + 1), 128), BF16)
    kk_all = jnp.concatenate([kp_ref[...], kc_ref[...]], axis=0)
    vv_all = jnp.concatenate([vp_ref[...], vc_ref[...]], axis=0)
    kk_var = (kk_all, swap(kk_all))
    vv_var = (jnp.concatenate([vv_all, ones], axis=1), jnp.concatenate([swap(vv_all), ones], axis=1))
    q_all = q_ref[...] * jnp.asarray(D_HD ** -0.5, BF16)
    q_lane = lax.broadcasted_iota(jnp.int32, (2 * blk, 128), 1)
    o_lane = lax.broadcasted_iota(jnp.int32, (blk, 128), 1)
    for qb in range(SWA_QB):
        keys = slice(qb * blk, (qb + 2) * blk)
        for kv in range(D_KV):
            res = []
            for half in range(2):
                q2 = jnp.concatenate([q_all[qb * blk:(qb + 1) * blk, (2 * kv + tt) * 128:(2 * kv + tt + 1) * 128]
                                      for tt in range(2)], axis=0)
                q2 = jnp.where((q_lane < D_HD) if half == 0 else (q_lane >= D_HD), q2, jnp.zeros_like(q2))
                var = half ^ kv
                s = lax.dot_general(q2, kk_var[var][keys], (((1,), (1,)), ((), ())), preferred_element_type=F32)
                bias = bias_ref[1, kv, half]
                if qb == 0:
                    bias = jnp.where(first, bias_ref[0, kv, half], bias)
                lg = s + bias
                es, sinks = [], []
                for tt in range(2):
                    sink = sink_ref[4 * kv + half + 2 * tt]
                    lg_h = lg[tt * blk:(tt + 1) * blk]
                    m = jnp.maximum(jnp.max(lg_h, axis=-1, keepdims=True), sink)
                    es.append(jnp.exp(lg_h - m))
                    sinks.append(jnp.exp(sink - m))
                pv = jnp.dot(jnp.concatenate(es, axis=0).astype(BF16), vv_var[var][keys],
                             preferred_element_type=F32)
                res.append([pv[tt * blk:(tt + 1) * blk, 0:128] / (pv[tt * blk:(tt + 1) * blk, 128:256] + sinks[tt])
                            for tt in range(2)])
            for tt in range(2):
                tile = jnp.where(o_lane < D_HD, res[0][tt], res[1][tt])
                o_ref[qb * blk:(qb + 1) * blk, (2 * kv + tt) * 128:(2 * kv + tt + 1) * 128] = tile.astype(BF16)


def _swa_bias():
    slopes = np.exp2(-8.0 * np.arange(1, D_HEADS + 1, dtype=np.float32) / D_HEADS).astype(np.float32)
    qi = np.arange(SWA_BLOCK)
    kj = np.arange(2 * SWA_BLOCK) - SWA_BLOCK
    dist = (qi[:, None] - kj[None, :]).astype(np.float32)
    valid = (dist >= 0) & (dist < WINDOW)
    general = np.where(valid[None], -(slopes[:, None, None] * dist[None]), np.float32(NEG_INF))
    first = np.where((kj >= 0)[None, None, :], general, np.float32(NEG_INF))
    per_head = np.stack([first, general]).astype(np.float32)
    table = np.stack([np.stack([np.concatenate([per_head[:, 4 * kv + half], per_head[:, 4 * kv + half + 2]], axis=1)
                                for half in range(2)], axis=1) for kv in range(D_KV)], axis=1)
    return jnp.asarray(table)


def _swa(z, sinks, batch, seq):
    rows = SWA_QB * SWA_BLOCK
    ns = seq // rows
    qc, kc, vc = DQ0 // BRANCH_W, DK0 // 128, DV0 // 128
    cur = lambda b, n: b * ns + n
    prev = lambda b, n: jnp.maximum((b * ns + n) * SWA_QB - 1, 0)
    return pl.pallas_call(
        _swa_kernel,
        grid=(batch, ns),
        in_specs=[
            pl.BlockSpec(memory_space=pltpu.SMEM),
            pl.BlockSpec((rows, BRANCH_W), lambda b, n: (cur(b, n), qc)),
            pl.BlockSpec((rows, 128), lambda b, n: (cur(b, n), kc)),
            pl.BlockSpec((SWA_BLOCK, 128), lambda b, n: (prev(b, n), kc)),
            pl.BlockSpec((rows, 128), lambda b, n: (cur(b, n), vc)),
            pl.BlockSpec((SWA_BLOCK, 128), lambda b, n: (prev(b, n), vc)),
            pl.BlockSpec((2, D_KV, 2, 2 * SWA_BLOCK, 2 * SWA_BLOCK), lambda b, n: (0, 0, 0, 0, 0)),
        ],
        out_specs=pl.BlockSpec((rows, BRANCH_W), lambda b, n: (cur(b, n), 0)),
        out_shape=jax.ShapeDtypeStruct((batch * seq, BRANCH_W), BF16),
        compiler_params=_cparams(("parallel", "arbitrary")),
        name="swa",
    )(sinks, z, z, z, z, z, _swa_bias())


def _lru_kernel(x_ref, g_ref, cw_ref, cb_ref, wa_ref, ba_ref, wx_ref, bx_ref, lam_ref, o_ref,
                xpad_sc, apad_sc, upad_sc, hc_sc):
    ts, pad = LRU_TS, LRU_PAD

    @pl.when(pl.program_id(1) == 0)
    def _():
        xpad_sc[0:8, :] = jnp.zeros((8, BRANCH_W), F32)
        hc_sc[...] = jnp.zeros(hc_sc.shape, F32)
        apad_sc[0:pad, :] = jnp.ones((pad, BRANCH_W), F32)
        upad_sc[0:pad, :] = jnp.zeros((pad, BRANCH_W), F32)

    x = x_ref[...].astype(F32)
    xpad_sc[8:8 + ts, :] = x
    cw = cw_ref[...]
    xc = (cw[3:4] * x + cw[2:3] * xpad_sc[7:7 + ts, :] + cw[1:2] * xpad_sc[6:6 + ts, :]
          + cw[0:1] * xpad_sc[5:5 + ts, :] + cb_ref[...])
    xpad_sc[0:8, :] = x[ts - 8:ts]

    xcb = xc.astype(BF16)
    r = jax.nn.sigmoid(jnp.dot(xcb, wa_ref[...], preferred_element_type=F32) + ba_ref[...])
    gi = jax.nn.sigmoid(jnp.dot(xcb, wx_ref[...], preferred_element_type=F32) + bx_ref[...])
    log_a = (-B_C * r) * _softplus(-lam_ref[...])
    a = jnp.exp(log_a)
    u = jnp.sqrt(1.0 - a * a) * (gi * xc)

    sh = 1
    while sh < ts:
        apad_sc[pad:pad + ts, :] = a
        upad_sc[pad:pad + ts, :] = u
        u = a * upad_sc[pad - sh:pad - sh + ts, :] + u
        a = a * apad_sc[pad - sh:pad - sh + ts, :]
        sh *= 2
    h = a * hc_sc[...] + u
    hc_sc[...] = h[ts - 1:ts]
    g = g_ref[...].astype(F32)
    gelu = g * jax.nn.sigmoid((2.0 * math.sqrt(2.0 / math.pi)) * (g + 0.044715 * (g * g * g)))
    o_ref[...] = (gelu * h).astype(BF16)


def _lru(z, cw, cb, wa, ba, wx, bx, lam, l, batch, seq):
    ns = seq // LRU_TS
    xc_, gc_ = BX0 // BRANCH_W, BG0 // BRANCH_W
    const = lambda shape: pl.BlockSpec(shape, lambda b, s: (0, 0))
    return pl.pallas_call(
        _lru_kernel,
        grid=(batch, ns),
        in_specs=[
            pl.BlockSpec((LRU_TS, BRANCH_W), lambda b, s: (b * ns + s, xc_)),
            pl.BlockSpec((LRU_TS, BRANCH_W), lambda b, s: (b * ns + s, gc_)),
            const((B_CONV, BRANCH_W)), const((1, BRANCH_W)),
            pl.BlockSpec((None, BRANCH_W, BRANCH_W), lambda b, s: (l, 0, 0)), const((1, BRANCH_W)),
            pl.BlockSpec((None, BRANCH_W, BRANCH_W), lambda b, s: (l, 0, 0)), const((1, BRANCH_W)),
            const((1, BRANCH_W)),
        ],
        out_specs=pl.BlockSpec((LRU_TS, BRANCH_W), lambda b, s: (b * ns + s, 0)),
        out_shape=jax.ShapeDtypeStruct((batch * seq, BRANCH_W), BF16),
        scratch_shapes=[
            pltpu.VMEM((LRU_TS + 8, BRANCH_W), F32),
            pltpu.VMEM((LRU_TS + LRU_PAD, BRANCH_W), F32),
            pltpu.VMEM((LRU_TS + LRU_PAD, BRANCH_W), F32),
            pltpu.VMEM((1, BRANCH_W), F32),
        ],
        compiler_params=_cparams(("parallel", "arbitrary")),
        name="lru",
    )(z, z, cw, cb, wa, ba, wx, bx, lam)


def _merge_kernel(h_ref, ya_ref, yb_ref, yc_ref, yd_ref, zg_ref, bg_ref, wb_ref, wo_ref, gn_ref, o_ref, v_ref):
    ys = (ya_ref, yb_ref, yc_ref, yd_ref)
    mixed = None
    for n in range(N_BRANCH):
        proj = jnp.dot(ys[n][...], wb_ref[n], preferred_element_type=F32)
        gate = zg_ref[:, n * D_MODEL:(n + 1) * D_MODEL].astype(F32) + bg_ref[n:n + 1, :]
        term = jax.nn.sigmoid(gate) * proj
        mixed = term if mixed is None else mixed + term
    o = h_ref[...] + jnp.dot(mixed.astype(BF16), wo_ref[...], preferred_element_type=F32)
    o_ref[...] = o
    v_ref[...] = _rms(o, gn_ref[...]).astype(BF16)


def _merge(h, ya, yb, yc, yd, zg, bg, wb, wo, gn, l):
    t = h.shape[0]
    row = lambda shape: pl.BlockSpec(shape, lambda i: (i, 0))
    return pl.pallas_call(
        _merge_kernel,
        grid=(t // MERGE_TM,),
        in_specs=[
            row((MERGE_TM, D_MODEL)),
            row((MERGE_TM, BRANCH_W)), row((MERGE_TM, BRANCH_W)), row((MERGE_TM, BRANCH_W)),
            row((MERGE_TM, BRANCH_W)),
            row((MERGE_TM, ZG_W)),
            pl.BlockSpec((N_BRANCH, D_MODEL), lambda i: (0, 0)),
            pl.BlockSpec((None, N_BRANCH, BRANCH_W, D_MODEL), lambda i: (l, 0, 0, 0), pipeline_mode=pl.Buffered(1)),
            pl.BlockSpec((None, D_MODEL, D_MODEL), lambda i: (l, 0, 0), pipeline_mode=pl.Buffered(1)),
            pl.BlockSpec((1, D_MODEL), lambda i: (0, 0)),
        ],
        out_specs=[row((MERGE_TM, D_MODEL)), row((MERGE_TM, D_MODEL))],
        out_shape=[jax.ShapeDtypeStruct((t, D_MODEL), F32), jax.ShapeDtypeStruct((t, D_MODEL), BF16)],
        compiler_params=_cparams(("parallel",)),
        name="merge",
    )(h, ya, yb, yc, yd, zg, bg, wb, wo, gn)


def _ffn_kernel(v_ref, h_ref, wg_hbm, wu_hbm, wd_hbm, gn_ref, o_ref, *rest, l, final_norm):
    wg_buf, wu_buf, wd_buf, sem = rest[-4:]
    nf = D_FF // FFN_TF
    i = pl.program_id(0)
    total = pl.num_programs(0) * nf

    def weight_copies(j, slot):
        c0 = pl.multiple_of(j * FFN_TF, FFN_TF)
        return (pltpu.make_async_copy(wg_hbm.at[l, :, pl.ds(c0, FFN_TF)], wg_buf.at[slot], sem.at[0, slot]),
                pltpu.make_async_copy(wu_hbm.at[l, :, pl.ds(c0, FFN_TF)], wu_buf.at[slot], sem.at[1, slot]),
                pltpu.make_async_copy(wd_hbm.at[l, pl.ds(c0, FFN_TF), :], wd_buf.at[slot], sem.at[2, slot]))

    @pl.when(i == 0)
    def _():
        for c in weight_copies(0, 0):
            c.start()

    o_ref[...] = h_ref[...]

    def column_block(j, carry):
        g = i * nf + j
        slot = g % 2

        @pl.when(g + 1 < total)
        def _():
            for c in weight_copies(jnp.where(j + 1 == nf, 0, j + 1), 1 - slot):
                c.start()

        for c in weight_copies(j, slot):
            c.wait()
        v = v_ref[...]
        a = jnp.dot(v, wg_buf[slot], preferred_element_type=F32)
        b = jnp.dot(v, wu_buf[slot], preferred_element_type=F32)
        t = (a * jax.nn.sigmoid(a)) * b
        o_ref[...] += jnp.dot(t.astype(BF16), wd_buf[slot], preferred_element_type=F32)
        return carry

    lax.fori_loop(0, nf, column_block, 0)

    y = _rms(o_ref[...], gn_ref[...])
    if final_norm:
        o_ref[...] = y
    else:
        rest[0][...] = y.astype(BF16)


def _ffn(v, h, wg, wu, wd, gn, l, final_norm):
    t = h.shape[0]
    row = lambda: pl.BlockSpec((FFN_TM, D_MODEL), lambda i: (i, 0))
    hbm = pl.BlockSpec(memory_space=pl.ANY)
    out_specs = [row()] if final_norm else [row(), row()]
    out_shape = [jax.ShapeDtypeStruct((t, D_MODEL), F32)]
    if not final_norm:
        out_shape.append(jax.ShapeDtypeStruct((t, D_MODEL), BF16))
    return pl.pallas_call(
        functools.partial(_ffn_kernel, l=l, final_norm=final_norm),
        grid=(t // FFN_TM,),
        in_specs=[row(), row(), hbm, hbm, hbm, pl.BlockSpec((1, D_MODEL), lambda i: (0, 0))],
        out_specs=out_specs,
        out_shape=out_shape,
        scratch_shapes=[
            pltpu.VMEM((2, D_MODEL, FFN_TF), BF16),
            pltpu.VMEM((2, D_MODEL, FFN_TF), BF16),
            pltpu.VMEM((2, FFN_TF, D_MODEL), BF16),
            pltpu.SemaphoreType.DMA((3, 2)),
        ],
        compiler_params=_cparams(("arbitrary",)),
        name="ffn_final" if final_norm else "ffn",
    )(v, h, wg, wu, wd, gn)


def _block_diag(w):
    depth, nb, c, d = w.shape
    eye = jnp.eye(nb, dtype=w.dtype)
    return jnp.einsum('lncd,nm->lncmd', w, eye).reshape(depth, nb * c, nb * d)


def kernel(x, norm_mix, w_in, b_gate, diff_lq1, diff_lk1, diff_lq2, diff_lk2, diff_subln, lru_conv_w, lru_conv_b,
           lru_wa, lru_ba, lru_wx, lru_bx, lru_lambda, fox_b_f, swa_sinks, w_branch, w_out, norm_ffn, w_ffn_gate,
           w_ffn_up, w_ffn_down, norm_final):
    batch, seq, d = x.shape
    depth = w_in.shape[0]
    assert d == D_MODEL and seq % max(A_TQ, C_TQ, LRU_TS, SWA_QB * SWA_BLOCK) == 0 and (batch * seq) % IN_TM == 0
    t = batch * seq

    piece = lambda lo, hi: jnp.swapaxes(w_in[:, :, lo:hi], 1, 2).astype(BF16)
    wt_zm = piece(0, ZM_W)
    wt_gz = piece(_R_GZ0, _R_GZ0 + ZG_W)
    wt_d = piece(_R_DQ0, _R_GZ0)
    wt_cf = jnp.pad(piece(_R_CF0, _R_DQ0), ((0, 0), (0, CF_PAD - C_HEADS), (0, 0)))
    wb = w_branch.astype(BF16)
    wo = w_out.astype(BF16)
    wg = w_ffn_gate.astype(BF16)
    wu = w_ffn_up.astype(BF16)
    wd = w_ffn_down.astype(BF16)
    wa = _block_diag(lru_wa).astype(BF16)
    wx = _block_diag(lru_wx).astype(BF16)
    lvec = jnp.stack([diff_lq1, diff_lk1, diff_lq2, diff_lk2], axis=1).astype(F32)
    bg = b_gate.reshape(depth, N_BRANCH, D_MODEL)
    gf = norm_final.reshape(1, D_MODEL)

    h = x.reshape(t, D_MODEL)
    u = _norm(h, norm_mix[0].reshape(1, D_MODEL))
    for l in range(depth):
        lam_init = 0.8 - 0.6 * math.exp(-0.3 * l)
        zg = _proj(u, wt_gz, l, "proj_gate")
        zm = _proj(u, wt_zm, l, "proj_mix")
        zd, cf = _proj_d(u, wt_d, wt_cf, l)

        ya = _diff_attn(zm, lvec[l], diff_subln[l].reshape(1, 2 * A_HD),
                        jnp.asarray([lam_init, 1.0 - lam_init], F32), batch, seq)
        yb = _lru(zm, lru_conv_w[l], lru_conv_b[l].reshape(1, -1), wa, lru_ba[l].reshape(1, -1), wx,
                  lru_bx[l].reshape(1, -1), lru_lambda[l].reshape(1, -1), l, batch, seq)
        cf_t = cf[:, :C_HEADS].reshape(batch, seq, C_HEADS).transpose(0, 2, 1).reshape(batch * C_HEADS, seq)
        bf_t = jnp.tile(fox_b_f[l], batch).reshape(batch * C_HEADS, 1)
        cum = _fox_cum(cf_t, bf_t).reshape(batch * C_HEADS, 1, seq)
        yc = _fox_attn(zm, cum, batch, seq)
        yd = _swa(zd, swa_sinks[l], batch, seq)

        h, v = _merge(h, ya, yb, yc, yd, zg, bg[l], wb, wo, norm_ffn[l].reshape(1, D_MODEL), l)
        if l == depth - 1:
            (h,) = _ffn(v, h, wg, wu, wd, gf, l, final_norm=True)
        else:
            h, u = _ffn(v, h, wg, wu, wd, norm_mix[l + 1].reshape(1, D_MODEL), l, final_norm=False)
    return h.reshape(batch, seq, D_MODEL)
```

```python
import functools
import math

import jax
import jax.numpy as jnp
import numpy as np
from jax import lax
from jax.experimental import pallas as pl
from jax.experimental.pallas import tpu as pltpu

F32 = jnp.float32
BF16 = jnp.bfloat16
LANES = 128

D_MODEL = 2048
N_BRANCH = 4
BRANCH_W = D_MODEL // 4
A_HEADS = 4
A_HD = BRANCH_W // (2 * A_HEADS)
B_BLOCKS = 8
B_CONV = 4
B_C = 8.0
C_HEADS = 4
C_HD = BRANCH_W // C_HEADS
D_HEADS = 8
D_KV = 2
D_GROUP = D_HEADS // D_KV
D_HD = BRANCH_W // D_HEADS
WINDOW = 128
SWA_BLOCK = 128
SWA_QB = 4
D_FF = -(-8 * D_MODEL // (3 * 256)) * 256
RMS_EPS = 1e-6
NEG_INF = -1e30

AQ0 = 0
AK0 = AQ0 + BRANCH_W
AV0 = AK0 + BRANCH_W
BX0 = AV0 + BRANCH_W
BG0 = BX0 + BRANCH_W
CQ0 = BG0 + BRANCH_W
CK0 = CQ0 + BRANCH_W
CV0 = CK0 + BRANCH_W
ZM_W = CV0 + BRANCH_W
DQ0 = 0
DK0 = DQ0 + BRANCH_W
DV0 = DK0 + D_KV * D_HD
ZD_W = DV0 + D_KV * D_HD
ZG_W = N_BRANCH * D_MODEL
CF_PAD = 128

_R_CF0 = ZM_W
_R_DQ0 = _R_CF0 + C_HEADS
_R_GZ0 = _R_DQ0 + ZD_W

V7X_VMEM_LIMIT = 56 * 1024 * 1024

NORM_TM = 1024
IN_TM, IN_TN = 2048, 1024
FLASH_ROWS = 256
A_TQ, A_HPS = 1024, 4
C_TQ, C_HPS = 1024, 4
LOG2E = math.log2(math.e)
LRU_TS = 512
LRU_PAD = LRU_TS // 2
MERGE_TM = 256
FFN_TM, FFN_TF = 512, 512


def _cparams(sem):
    return pltpu.CompilerParams(dimension_semantics=sem, vmem_limit_bytes=V7X_VMEM_LIMIT)


def _rms(x, g):
    return x * lax.rsqrt(jnp.mean(x * x, axis=-1, keepdims=True) + RMS_EPS) * g


def _softplus(y):
    return jnp.maximum(y, 0.0) + jnp.log1p(jnp.exp(-jnp.abs(y)))


def _norm_kernel(x_ref, g_ref, o_ref):
    o_ref[...] = _rms(x_ref[...], g_ref[...]).astype(BF16)


def _norm(x, g):
    t = x.shape[0]
    return pl.pallas_call(
        _norm_kernel,
        grid=(t // NORM_TM,),
        in_specs=[pl.BlockSpec((NORM_TM, D_MODEL), lambda i: (i, 0)), pl.BlockSpec((1, D_MODEL), lambda i: (0, 0))],
        out_specs=pl.BlockSpec((NORM_TM, D_MODEL), lambda i: (i, 0)),
        out_shape=jax.ShapeDtypeStruct((t, D_MODEL), BF16),
        compiler_params=_cparams(("parallel",)),
        name="norm0",
    )(x, g)


_NT = (((1,), (1,)), ((), ()))


def _proj_kernel(u_ref, wt_ref, z_ref):
    z_ref[...] = lax.dot_general(u_ref[...], wt_ref[...], _NT, preferred_element_type=F32).astype(BF16)


def _proj(u, wt, l, name):
    t = u.shape[0]
    n_cols = wt.shape[1]
    return pl.pallas_call(
        _proj_kernel,
        grid=(t // IN_TM, n_cols // IN_TN),
        in_specs=[
            pl.BlockSpec((IN_TM, D_MODEL), lambda i, j: (i, 0)),
            pl.BlockSpec((None, IN_TN, D_MODEL), lambda i, j: (l, j, 0)),
        ],
        out_specs=pl.BlockSpec((IN_TM, IN_TN), lambda i, j: (i, j)),
        out_shape=jax.ShapeDtypeStruct((t, n_cols), BF16),
        compiler_params=_cparams(("parallel", "arbitrary")),
        name=name,
    )(u, wt)


def _proj_d_kernel(u_ref, wt_ref, wcft_ref, z_ref, cf_ref):
    u = u_ref[...]
    z_ref[...] = lax.dot_general(u, wt_ref[...], _NT, preferred_element_type=F32).astype(BF16)
    cf_ref[...] = lax.dot_general(u, wcft_ref[...], _NT, preferred_element_type=F32)


def _proj_d(u, wt, wcft, l):
    t = u.shape[0]
    return pl.pallas_call(
        _proj_d_kernel,
        grid=(t // IN_TM,),
        in_specs=[
            pl.BlockSpec((IN_TM, D_MODEL), lambda i: (i, 0)),
            pl.BlockSpec((None, ZD_W, D_MODEL), lambda i: (l, 0, 0)),
            pl.BlockSpec((None, CF_PAD, D_MODEL), lambda i: (l, 0, 0)),
        ],
        out_specs=[
            pl.BlockSpec((IN_TM, ZD_W), lambda i: (i, 0)),
            pl.BlockSpec((IN_TM, CF_PAD), lambda i: (i, 0)),
        ],
        out_shape=[jax.ShapeDtypeStruct((t, ZD_W), BF16), jax.ShapeDtypeStruct((t, CF_PAD), F32)],
        compiler_params=_cparams(("parallel",)),
        name="proj_d",
    )(u, wt, wcft)


def _tri_tables(n):
    qi = [i for i in range(n) for _ in range(i + 1)]
    kj = [j for i in range(n) for j in range(i + 1)]
    return jnp.asarray(qi, jnp.int32), jnp.asarray(kj, jnp.int32)


def _flash_rows(q_ref, k_ref, v2_sc, bias, m_sc, acc_sc, r0, diag_q0, scale):
    rows = FLASH_ROWS
    nk = k_ref.shape[0] if diag_q0 is None else diag_q0 + rows
    s = lax.dot_general(q_ref[r0:r0 + rows, :], k_ref[0:nk, :], (((1,), (1,)), ((), ())),
                        preferred_element_type=F32)
    s = s * scale + bias[:, 0:nk]
    if diag_q0 is not None:
        qpos = lax.broadcasted_iota(jnp.int32, s.shape, 0) + diag_q0
        kpos = lax.broadcasted_iota(jnp.int32, s.shape, 1)
        s = jnp.where(kpos <= qpos, s, NEG_INF)
    m_prev = m_sc[r0:r0 + rows, :]
    m_new = jnp.maximum(m_prev, jnp.max(s, axis=-1, keepdims=True))
    alpha = jnp.exp2(m_prev - m_new)
    p = jnp.exp2(s - jnp.concatenate([m_new] * (nk // LANES), axis=1))
    pv = jnp.dot(p.astype(BF16), v2_sc[0:nk, :], preferred_element_type=F32)
    acc_sc[r0:r0 + rows, :] = jnp.concatenate([alpha, alpha], axis=1) * acc_sc[r0:r0 + rows, :] + pv
    m_sc[r0:r0 + rows, :] = m_new


def _flash_block(q_ref, k_ref, v2_sc, bias, m_sc, acc_sc, n_rows, tq, diag, scale):
    for r0 in range(0, n_rows, FLASH_ROWS):
        _flash_rows(q_ref, k_ref, v2_sc, bias, m_sc, acc_sc, r0, (r0 % tq) if diag else None, scale)


def _flash_init(v2_sc, m_sc, acc_sc):
    v2_sc[:, LANES:2 * LANES] = jnp.ones((v2_sc.shape[0], LANES), BF16)
    m_sc[...] = jnp.full(m_sc.shape, -jnp.inf, F32)
    acc_sc[...] = jnp.zeros(acc_sc.shape, F32)


def _diff_attn_kernel(qi_ref, kj_ref, slope_ref, lami_ref, q_ref, k_ref, v_ref, lv_ref, sg_ref, o_ref,
                      q2_sc, v2_sc, m_sc, acc_sc):
    tq = A_TQ
    t = pl.program_id(2)
    i = qi_ref[t]
    j = kj_ref[t]
    heads = [(hh, slice(hh * LANES, (hh + 1) * LANES)) for hh in range(A_HPS)]

    @pl.when(j == 0)
    def _():
        for hh, cols in heads:
            q = q_ref[:, cols]
            lane = lax.broadcasted_iota(jnp.int32, q.shape, 1)
            qs = q * jnp.asarray(A_HD ** -0.5, BF16)
            zero = jnp.zeros_like(qs)
            q2_sc[hh, 0:tq, :] = jnp.where(lane < A_HD, qs, zero)
            q2_sc[hh, tq:2 * tq, :] = jnp.where(lane >= A_HD, qs, zero)
            _flash_init(v2_sc.at[hh], m_sc.at[hh], acc_sc.at[hh])

    kpos = (lax.broadcasted_iota(jnp.int32, (1, tq), 1) + (j - i) * tq).astype(F32)

    def block(diag):
        for hh, cols in heads:
            v2_sc[hh, :, 0:LANES] = v_ref[:, cols]
            bias = kpos * (slope_ref[pl.program_id(1) * A_HPS + hh] * LOG2E)
            _flash_block(q2_sc.at[hh], k_ref.at[:, cols], v2_sc.at[hh], bias, m_sc.at[hh], acc_sc.at[hh],
                         2 * tq, tq, diag, LOG2E)

    @pl.when(j < i)
    def _():
        block(False)

    @pl.when(j == i)
    def _():
        block(True)
        lv = lv_ref[...]
        lam = (jnp.exp(jnp.sum(lv[0:1] * lv[1:2], axis=-1, keepdims=True))
               - jnp.exp(jnp.sum(lv[2:3] * lv[3:4], axis=-1, keepdims=True)) + lami_ref[0])
        for hh, cols in heads:
            acc = acc_sc[hh]
            o = acc[0:tq, 0:LANES] / acc[0:tq, LANES:2 * LANES] - lam * (acc[tq:2 * tq, 0:LANES] / acc[tq:2 * tq, LANES:2 * LANES])
            o_ref[:, cols] = (_rms(o, sg_ref[...]) * lami_ref[1]).astype(BF16)


def _diff_attn(z, lvec, subln, lam_init, batch, seq):
    nq = seq // A_TQ
    qi, kj = _tri_tables(nq)
    slopes = jnp.asarray(np.exp2(-8.0 * np.arange(1, A_HEADS + 1, dtype=np.float32) / A_HEADS), F32)
    w = LANES * A_HPS
    qc, kc, vc = AQ0 // w, AK0 // w, AV0 // w
    smem = pl.BlockSpec(memory_space=pltpu.SMEM)
    grid_spec = pltpu.PrefetchScalarGridSpec(
        num_scalar_prefetch=2,
        grid=(batch, A_HEADS // A_HPS, int(qi.shape[0])),
        in_specs=[
            smem, smem,
            pl.BlockSpec((A_TQ, w), lambda b, h, t, qi, kj: (b * nq + qi[t], qc + h)),
            pl.BlockSpec((A_TQ, w), lambda b, h, t, qi, kj: (b * nq + kj[t], kc + h)),
            pl.BlockSpec((A_TQ, w), lambda b, h, t, qi, kj: (b * nq + kj[t], vc + h)),
            pl.BlockSpec((4, A_HD), lambda b, h, t, qi, kj: (0, 0)),
            pl.BlockSpec((1, 2 * A_HD), lambda b, h, t, qi, kj: (0, 0)),
        ],
        out_specs=pl.BlockSpec((A_TQ, w), lambda b, h, t, qi, kj: (b * nq + qi[t], h)),
        scratch_shapes=[
            pltpu.VMEM((A_HPS, 2 * A_TQ, LANES), BF16),
            pltpu.VMEM((A_HPS, A_TQ, 2 * LANES), BF16),
            pltpu.VMEM((A_HPS, 2 * A_TQ, LANES), F32),
            pltpu.VMEM((A_HPS, 2 * A_TQ, 2 * LANES), F32),
        ],
    )
    return pl.pallas_call(
        _diff_attn_kernel,
        grid_spec=grid_spec,
        out_shape=jax.ShapeDtypeStruct((batch * seq, BRANCH_W), BF16),
        compiler_params=_cparams(("parallel", "parallel", "arbitrary")),
        name="diff_attn",
    )(qi, kj, slopes, lam_init, z, z, z, lvec, subln)


def _fox_cum_kernel(cf_ref, bf_ref, o_ref):
    x = cf_ref[...] + bf_ref[...]
    logf = jnp.minimum(x, 0.0) - jnp.log1p(jnp.exp(-jnp.abs(x)))
    rows, seq = logf.shape
    lane = lax.broadcasted_iota(jnp.int32, (rows, LANES), 1)
    carry = jnp.zeros((rows, 1), F32)
    for c in range(seq // LANES):
        y = logf[:, c * LANES:(c + 1) * LANES]
        sh = 1
        while sh < LANES:
            y = y + jnp.where(lane >= sh, pltpu.roll(y, sh, axis=1), 0.0)
            sh *= 2
        y = y + carry
        o_ref[:, c * LANES:(c + 1) * LANES] = y
        carry = y[:, LANES - 1:LANES]


def _fox_cum(cf_t, bf_t):
    rows, seq = cf_t.shape
    return pl.pallas_call(
        _fox_cum_kernel,
        grid=(1,),
        in_specs=[pl.BlockSpec((rows, seq), lambda i: (0, 0)), pl.BlockSpec((rows, 1), lambda i: (0, 0))],
        out_specs=pl.BlockSpec((rows, seq), lambda i: (0, 0)),
        out_shape=jax.ShapeDtypeStruct((rows, seq), F32),
        compiler_params=_cparams(("arbitrary",)),
        name="fox_cum",
    )(cf_t, bf_t)


def _fox_attn_kernel(qi_ref, kj_ref, q_ref, k_ref, v_ref, cum_ref, o_ref, v2_sc, m_sc, acc_sc):
    tq = C_TQ
    t = pl.program_id(2)
    i = qi_ref[t]
    j = kj_ref[t]
    heads = [(hh, slice(hh * LANES, (hh + 1) * LANES)) for hh in range(C_HPS)]

    @pl.when(j == 0)
    def _():
        for hh, _ in heads:
            _flash_init(v2_sc.at[hh], m_sc.at[hh], acc_sc.at[hh])

    def block(diag):
        for hh, cols in heads:
            v2_sc[hh, :, 0:LANES] = v_ref[:, cols]
            _flash_block(q_ref.at[:, cols], k_ref.at[:, cols], v2_sc.at[hh], cum_ref[hh] * (-LOG2E), m_sc.at[hh],
                         acc_sc.at[hh], tq, tq, diag, C_HD ** -0.5 * LOG2E)

    @pl.when(j < i)
    def _():
        block(False)

    @pl.when(j == i)
    def _():
        block(True)
        for hh, cols in heads:
            acc = acc_sc[hh]
            o_ref[:, cols] = (acc[:, 0:LANES] / acc[:, LANES:2 * LANES]).astype(BF16)


def _fox_attn(z, cum, batch, seq):
    nq = seq // C_TQ
    qi, kj = _tri_tables(nq)
    w = LANES * C_HPS
    qc, kc, vc = CQ0 // w, CK0 // w, CV0 // w
    hg = C_HEADS // C_HPS
    grid_spec = pltpu.PrefetchScalarGridSpec(
        num_scalar_prefetch=2,
        grid=(batch, hg, int(qi.shape[0])),
        in_specs=[
            pl.BlockSpec((C_TQ, w), lambda b, h, t, qi, kj: (b * nq + qi[t], qc + h)),
            pl.BlockSpec((C_TQ, w), lambda b, h, t, qi, kj: (b * nq + kj[t], kc + h)),
            pl.BlockSpec((C_TQ, w), lambda b, h, t, qi, kj: (b * nq + kj[t], vc + h)),
            pl.BlockSpec((C_HPS, 1, C_TQ), lambda b, h, t, qi, kj: (b * hg + h, 0, kj[t])),
        ],
        out_specs=pl.BlockSpec((C_TQ, w), lambda b, h, t, qi, kj: (b * nq + qi[t], h)),
        scratch_shapes=[
            pltpu.VMEM((C_HPS, C_TQ, 2 * LANES), BF16),
            pltpu.VMEM((C_HPS, C_TQ, LANES), F32),
            pltpu.VMEM((C_HPS, C_TQ, 2 * LANES), F32),
        ],
    )
    return pl.pallas_call(
        _fox_attn_kernel,
        grid_spec=grid_spec,
        out_shape=jax.ShapeDtypeStruct((batch * seq, BRANCH_W), BF16),
        compiler_params=_cparams(("parallel", "parallel", "arbitrary")),
        name="fox_attn",
    )(qi, kj, z, z, z, cum)


def _swa_kernel(sink_ref, q_ref, kc_ref, kp_ref, vc_ref, vp_ref, bias_ref, o_ref):
    blk = SWA_BLOCK
    first = pl.program_id(1) == 0
    swap = lambda x: jnp.concatenate([x[:, D_HD:2 * D_HD], x[:, 0:D_HD]], axis=1)
    ones = jnp.ones((blk * (SWA_QB + 1), LANES), BF16)
    kk_all = jnp.concatenate([kp_ref[...], kc_ref[...]], axis=0)
    vv_all = jnp.concatenate([vp_ref[...], vc_ref[...]], axis=0)
    kk_var = (kk_all, swap(kk_all))
    vv_var = (jnp.concatenate([vv_all, ones], axis=1), jnp.concatenate([swap(vv_all), ones], axis=1))
    q_all = q_ref[...] * jnp.asarray(D_HD ** -0.5, BF16)
    q_lane = lax.broadcasted_iota(jnp.int32, (2 * blk, LANES), 1)
    o_lane = lax.broadcasted_iota(jnp.int32, (blk, LANES), 1)
    for qb in range(SWA_QB):
        keys = slice(qb * blk, (qb + 2) * blk)
        for kv in range(D_KV):
            res = []
            for half in range(2):
                q2 = jnp.concatenate([q_all[qb * blk:(qb + 1) * blk, (2 * kv + tt) * LANES:(2 * kv + tt + 1) * LANES]
                                      for tt in range(2)], axis=0)
                q2 = jnp.where((q_lane < D_HD) if half == 0 else (q_lane >= D_HD), q2, jnp.zeros_like(q2))
                var = half ^ kv
                s = lax.dot_general(q2, kk_var[var][keys], (((1,), (1,)), ((), ())), preferred_element_type=F32)
                bias = bias_ref[1, kv, half]
                if qb == 0:
                    bias = jnp.where(first, bias_ref[0, kv, half], bias)
                lg = s + bias
                es, sinks = [], []
                for tt in range(2):
                    sink = sink_ref[4 * kv + half + 2 * tt]
                    lg_h = lg[tt * blk:(tt + 1) * blk]
                    m = jnp.maximum(jnp.max(lg_h, axis=-1, keepdims=True), sink)
                    es.append(jnp.exp(lg_h - m))
                    sinks.append(jnp.exp(sink - m))
                pv = jnp.dot(jnp.concatenate(es, axis=0).astype(BF16), vv_var[var][keys],
                             preferred_element_type=F32)
                res.append([pv[tt * blk:(tt + 1) * blk, 0:LANES] / (pv[tt * blk:(tt + 1) * blk, LANES:2 * LANES] + sinks[tt])
                            for tt in range(2)])
            for tt in range(2):
                tile = jnp.where(o_lane < D_HD, res[0][tt], res[1][tt])
                o_ref[qb * blk:(qb + 1) * blk, (2 * kv + tt) * LANES:(2 * kv + tt + 1) * LANES] = tile.astype(BF16)


def _swa_bias():
    slopes = np.exp2(-8.0 * np.arange(1, D_HEADS + 1, dtype=np.float32) / D_HEADS).astype(np.float32)
    qi = np.arange(SWA_BLOCK)
    kj = np.arange(2 * SWA_BLOCK) - SWA_BLOCK
    dist = (qi[:, None] - kj[None, :]).astype(np.float32)
    valid = (dist >= 0) & (dist < WINDOW)
    general = np.where(valid[None], -(slopes[:, None, None] * dist[None]), np.float32(NEG_INF))
    first = np.where((kj >= 0)[None, None, :], general, np.float32(NEG_INF))
    per_head = np.stack([first, general]).astype(np.float32)
    table = np.stack([np.stack([np.concatenate([per_head[:, 4 * kv + half], per_head[:, 4 * kv + half + 2]], axis=1)
                                for half in range(2)], axis=1) for kv in range(D_KV)], axis=1)
    return jnp.asarray(table)


def _swa(z, sinks, batch, seq):
    rows = SWA_QB * SWA_BLOCK
    ns = seq // rows
    qc, kc, vc = DQ0 // BRANCH_W, DK0 // LANES, DV0 // LANES
    cur = lambda b, n: b * ns + n
    prev = lambda b, n: jnp.maximum((b * ns + n) * SWA_QB - 1, 0)
    return pl.pallas_call(
        _swa_kernel,
        grid=(batch, ns),
        in_specs=[
            pl.BlockSpec(memory_space=pltpu.SMEM),
            pl.BlockSpec((rows, BRANCH_W), lambda b, n: (cur(b, n), qc)),
            pl.BlockSpec((rows, LANES), lambda b, n: (cur(b, n), kc)),
            pl.BlockSpec((SWA_BLOCK, LANES), lambda b, n: (prev(b, n), kc)),
            pl.BlockSpec((rows, LANES), lambda b, n: (cur(b, n), vc)),
            pl.BlockSpec((SWA_BLOCK, LANES), lambda b, n: (prev(b, n), vc)),
            pl.BlockSpec((2, D_KV, 2, 2 * SWA_BLOCK, 2 * SWA_BLOCK), lambda b, n: (0, 0, 0, 0, 0)),
        ],
        out_specs=pl.BlockSpec((rows, BRANCH_W), lambda b, n: (cur(b, n), 0)),
        out_shape=jax.ShapeDtypeStruct((batch * seq, BRANCH_W), BF16),
        compiler_params=_cparams(("parallel", "arbitrary")),
        name="swa",
    )(sinks, z, z, z, z, z, _swa_bias())


def _lru_kernel(x_ref, g_ref, cw_ref, cb_ref, wa_ref, ba_ref, wx_ref, bx_ref, lam_ref, o_ref,
                xpad_sc, apad_sc, upad_sc, hc_sc):
    ts, pad = LRU_TS, LRU_PAD

    @pl.when(pl.program_id(1) == 0)
    def _():
        xpad_sc[0:8, :] = jnp.zeros((8, BRANCH_W), F32)
        hc_sc[...] = jnp.zeros(hc_sc.shape, F32)
        apad_sc[0:pad, :] = jnp.ones((pad, BRANCH_W), F32)
        upad_sc[0:pad, :] = jnp.zeros((pad, BRANCH_W), F32)

    x = x_ref[...].astype(F32)
    xpad_sc[8:8 + ts, :] = x
    cw = cw_ref[...]
    xc = (cw[3:4] * x + cw[2:3] * xpad_sc[7:7 + ts, :] + cw[1:2] * xpad_sc[6:6 + ts, :]
          + cw[0:1] * xpad_sc[5:5 + ts, :] + cb_ref[...])
    xpad_sc[0:8, :] = x[ts - 8:ts]

    xcb = xc.astype(BF16)
    r = jax.nn.sigmoid(jnp.dot(xcb, wa_ref[...], preferred_element_type=F32) + ba_ref[...])
    gi = jax.nn.sigmoid(jnp.dot(xcb, wx_ref[...], preferred_element_type=F32) + bx_ref[...])
    log_a = (-B_C * r) * _softplus(-lam_ref[...])
    a = jnp.exp(log_a)
    u = jnp.sqrt(1.0 - a * a) * (gi * xc)

    sh = 1
    while sh < ts:
        apad_sc[pad:pad + ts, :] = a
        upad_sc[pad:pad + ts, :] = u
        u = a * upad_sc[pad - sh:pad - sh + ts, :] + u
        a = a * apad_sc[pad - sh:pad - sh + ts, :]
        sh *= 2
    h = a * hc_sc[...] + u
    hc_sc[...] = h[ts - 1:ts]
    g = g_ref[...].astype(F32)
    gelu = g * jax.nn.sigmoid((2.0 * math.sqrt(2.0 / math.pi)) * (g + 0.044715 * (g * g * g)))
    o_ref[...] = (gelu * h).astype(BF16)


def _lru(z, cw, cb, wa, ba, wx, bx, lam, l, batch, seq):
    ns = seq // LRU_TS
    xc_, gc_ = BX0 // BRANCH_W, BG0 // BRANCH_W
    const = lambda shape: pl.BlockSpec(shape, lambda b, s: (0, 0))
    return pl.pallas_call(
        _lru_kernel,
        grid=(batch, ns),
        in_specs=[
            pl.BlockSpec((LRU_TS, BRANCH_W), lambda b, s: (b * ns + s, xc_)),
            pl.BlockSpec((LRU_TS, BRANCH_W), lambda b, s: (b * ns + s, gc_)),
            const((B_CONV, BRANCH_W)), const((1, BRANCH_W)),
            pl.BlockSpec((None, BRANCH_W, BRANCH_W), lambda b, s: (l, 0, 0)), const((1, BRANCH_W)),
            pl.BlockSpec((None, BRANCH_W, BRANCH_W), lambda b, s: (l, 0, 0)), const((1, BRANCH_W)),
            const((1, BRANCH_W)),
        ],
        out_specs=pl.BlockSpec((LRU_TS, BRANCH_W), lambda b, s: (b * ns + s, 0)),
        out_shape=jax.ShapeDtypeStruct((batch * seq, BRANCH_W), BF16),
        scratch_shapes=[
            pltpu.VMEM((LRU_TS + 8, BRANCH_W), F32),
            pltpu.VMEM((LRU_TS + LRU_PAD, BRANCH_W), F32),
            pltpu.VMEM((LRU_TS + LRU_PAD, BRANCH_W), F32),
            pltpu.VMEM((1, BRANCH_W), F32),
        ],
        compiler_params=_cparams(("parallel", "arbitrary")),
        name="lru",
    )(z, z, cw, cb, wa, ba, wx, bx, lam)


def _merge_kernel(h_ref, ya_ref, yb_ref, yc_ref, yd_ref, zg_ref, bg_ref, wb_ref, wo_ref, gn_ref, o_ref, v_ref):
    ys = (ya_ref, yb_ref, yc_ref, yd_ref)
    mixed = None
    for n in range(N_BRANCH):
        proj = jnp.dot(ys[n][...], wb_ref[n], preferred_element_type=F32)
        gate = zg_ref[:, n * D_MODEL:(n + 1) * D_MODEL].astype(F32) + bg_ref[n:n + 1, :]
        term = jax.nn.sigmoid(gate) * proj
        mixed = term if mixed is None else mixed + term
    o = h_ref[...] + jnp.dot(mixed.astype(BF16), wo_ref[...], preferred_element_type=F32)
    o_ref[...] = o
    v_ref[...] = _rms(o, gn_ref[...]).astype(BF16)


def _merge(h, ya, yb, yc, yd, zg, bg, wb, wo, gn, l):
    t = h.shape[0]
    row = lambda shape: pl.BlockSpec(shape, lambda i: (i, 0))
    return pl.pallas_call(
        _merge_kernel,
        grid=(t // MERGE_TM,),
        in_specs=[
            row((MERGE_TM, D_MODEL)),
            row((MERGE_TM, BRANCH_W)), row((MERGE_TM, BRANCH_W)), row((MERGE_TM, BRANCH_W)),
            row((MERGE_TM, BRANCH_W)),
            row((MERGE_TM, ZG_W)),
            pl.BlockSpec((N_BRANCH, D_MODEL), lambda i: (0, 0)),
            pl.BlockSpec((None, N_BRANCH, BRANCH_W, D_MODEL), lambda i: (l, 0, 0, 0), pipeline_mode=pl.Buffered(1)),
            pl.BlockSpec((None, D_MODEL, D_MODEL), lambda i: (l, 0, 0), pipeline_mode=pl.Buffered(1)),
            pl.BlockSpec((1, D_MODEL), lambda i: (0, 0)),
        ],
        out_specs=[row((MERGE_TM, D_MODEL)), row((MERGE_TM, D_MODEL))],
        out_shape=[jax.ShapeDtypeStruct((t, D_MODEL), F32), jax.ShapeDtypeStruct((t, D_MODEL), BF16)],
        compiler_params=_cparams(("parallel",)),
        name="merge",
    )(h, ya, yb, yc, yd, zg, bg, wb, wo, gn)


def _ffn_kernel(v_ref, h_ref, wg_ref, wu_ref, wd_ref, gn_ref, o_ref, *u_ref, final_norm):
    j = pl.program_id(1)

    @pl.when(j == 0)
    def _():
        o_ref[...] = h_ref[...]

    v = v_ref[...]
    a = jnp.dot(v, wg_ref[...], preferred_element_type=F32)
    b = jnp.dot(v, wu_ref[...], preferred_element_type=F32)
    t = (a * jax.nn.sigmoid(a)) * b
    o_ref[...] += jnp.dot(t.astype(BF16), wd_ref[...], preferred_element_type=F32)

    @pl.when(j == pl.num_programs(1) - 1)
    def _():
        y = _rms(o_ref[...], gn_ref[...])
        if final_norm:
            o_ref[...] = y
        else:
            u_ref[0][...] = y.astype(BF16)


def _ffn(v, h, wg, wu, wd, gn, l, final_norm):
    t = h.shape[0]
    row = lambda: pl.BlockSpec((FFN_TM, D_MODEL), lambda i, j: (i, 0))
    out_specs = [row()] if final_norm else [row(), row()]
    out_shape = [jax.ShapeDtypeStruct((t, D_MODEL), F32)]
    if not final_norm:
        out_shape.append(jax.ShapeDtypeStruct((t, D_MODEL), BF16))
    return pl.pallas_call(
        functools.partial(_ffn_kernel, final_norm=final_norm),
        grid=(t // FFN_TM, D_FF // FFN_TF),
        in_specs=[
            row(), row(),
            pl.BlockSpec((None, D_MODEL, FFN_TF), lambda i, j: (l, 0, j)),
            pl.BlockSpec((None, D_MODEL, FFN_TF), lambda i, j: (l, 0, j)),
            pl.BlockSpec((None, FFN_TF, D_MODEL), lambda i, j: (l, j, 0)),
            pl.BlockSpec((1, D_MODEL), lambda i, j: (0, 0)),
        ],
        out_specs=out_specs,
        out_shape=out_shape,
        compiler_params=_cparams(("parallel", "arbitrary")),
        name="ffn_final" if final_norm else "ffn",
    )(v, h, wg, wu, wd, gn)


def _block_diag(w):
    depth, nb, c, d = w.shape
    eye = jnp.eye(nb, dtype=w.dtype)
    return jnp.einsum('lncd,nm->lncmd', w, eye).reshape(depth, nb * c, nb * d)


def kernel(x, norm_mix, w_in, b_gate, diff_lq1, diff_lk1, diff_lq2, diff_lk2, diff_subln, lru_conv_w, lru_conv_b,
           lru_wa, lru_ba, lru_wx, lru_bx, lru_lambda, fox_b_f, swa_sinks, w_branch, w_out, norm_ffn, w_ffn_gate,
           w_ffn_up, w_ffn_down, norm_final):
    batch, seq, d = x.shape
    depth = w_in.shape[0]
    assert d == D_MODEL and seq % max(A_TQ, C_TQ, LRU_TS, SWA_QB * SWA_BLOCK) == 0 and (batch * seq) % IN_TM == 0
    t = batch * seq

    piece = lambda lo, hi: jnp.swapaxes(w_in[:, :, lo:hi], 1, 2).astype(BF16)
    wt_zm = piece(0, ZM_W)
    wt_gz = piece(_R_GZ0, _R_GZ0 + ZG_W)
    wt_d = piece(_R_DQ0, _R_GZ0)
    wt_cf = jnp.pad(piece(_R_CF0, _R_DQ0), ((0, 0), (0, CF_PAD - C_HEADS), (0, 0)))
    wb = w_branch.astype(BF16)
    wo = w_out.astype(BF16)
    wg = w_ffn_gate.astype(BF16)
    wu = w_ffn_up.astype(BF16)
    wd = w_ffn_down.astype(BF16)
    wa = _block_diag(lru_wa).astype(BF16)
    wx = _block_diag(lru_wx).astype(BF16)
    lvec = jnp.stack([diff_lq1, diff_lk1, diff_lq2, diff_lk2], axis=1).astype(F32)
    bg = b_gate.reshape(depth, N_BRANCH, D_MODEL)
    gf = norm_final.reshape(1, D_MODEL)

    h = x.reshape(t, D_MODEL)
    u = _norm(h, norm_mix[0].reshape(1, D_MODEL))
    for l in range(depth):
        lam_init = 0.8 - 0.6 * math.exp(-0.3 * l)
        zg = _proj(u, wt_gz, l, "proj_gate")
        zm = _proj(u, wt_zm, l, "proj_mix")
        zd, cf = _proj_d(u, wt_d, wt_cf, l)

        ya = _diff_attn(zm, lvec[l], diff_subln[l].reshape(1, 2 * A_HD),
                        jnp.asarray([lam_init, 1.0 - lam_init], F32), batch, seq)
        yb = _lru(zm, lru_conv_w[l], lru_conv_b[l].reshape(1, -1), wa, lru_ba[l].reshape(1, -1), wx,
                  lru_bx[l].reshape(1, -1), lru_lambda[l].reshape(1, -1), l, batch, seq)
        cf_t = cf[:, :C_HEADS].reshape(batch, seq, C_HEADS).transpose(0, 2, 1).reshape(batch * C_HEADS, seq)
        bf_t = jnp.tile(fox_b_f[l], batch).reshape(batch * C_HEADS, 1)
        cum = _fox_cum(cf_t, bf_t).reshape(batch * C_HEADS, 1, seq)
        yc = _fox_attn(zm, cum, batch, seq)
        yd = _swa(zd, swa_sinks[l], batch, seq)

        h, v = _merge(h, ya, yb, yc, yd, zg, bg[l], wb, wo, norm_ffn[l].reshape(1, D_MODEL), l)
        if l == depth - 1:
            (h,) = _ffn(v, h, wg, wu, wd, gf, l, final_norm=True)
        else:
            h, u = _ffn(v, h, wg, wu, wd, norm_mix[l + 1].reshape(1, D_MODEL), l, final_norm=False)
    return h.reshape(batch, seq, D_MODEL)
```

```python
import functools
import math

import jax
import jax.numpy as jnp
import numpy as np
from jax import lax
from jax.experimental import pallas as pl
from jax.experimental.pallas import tpu as pltpu

F32 = jnp.float32
BF16 = jnp.bfloat16
LANES = 128

D_MODEL = 2048
N_BRANCH = 4
BRANCH_W = D_MODEL // 4
A_HEADS = 4
A_HD = BRANCH_W // (2 * A_HEADS)
B_BLOCKS = 8
B_CONV = 4
B_C = 8.0
C_HEADS = 4
C_HD = BRANCH_W // C_HEADS
D_HEADS = 8
D_KV = 2
D_GROUP = D_HEADS // D_KV
D_HD = BRANCH_W // D_HEADS
WINDOW = 128
SWA_BLOCK = 128
SWA_QB = 4
D_FF = -(-8 * D_MODEL // (3 * 256)) * 256
RMS_EPS = 1e-6
NEG_INF = -1e30

AQ0 = 0
AK0 = AQ0 + BRANCH_W
AV0 = AK0 + BRANCH_W
BX0 = AV0 + BRANCH_W
BG0 = BX0 + BRANCH_W
CQ0 = BG0 + BRANCH_W
CK0 = CQ0 + BRANCH_W
CV0 = CK0 + BRANCH_W
ZM_W = CV0 + BRANCH_W
DQ0 = 0
DK0 = DQ0 + BRANCH_W
DV0 = DK0 + D_KV * D_HD
ZD_W = DV0 + D_KV * D_HD
ZG_W = N_BRANCH * D_MODEL
CF_PAD = 128

_R_CF0 = ZM_W
_R_DQ0 = _R_CF0 + C_HEADS
_R_GZ0 = _R_DQ0 + ZD_W

V7X_VMEM_LIMIT = 56 * 1024 * 1024

NORM_TM = 1024
IN_TM, IN_TN = 2048, 1024
FLASH_ROWS = 256
A_TQ, A_HPS = 1024, 4
C_TQ, C_HPS = 1024, 4
LOG2E = math.log2(math.e)
LRU_TS = 512
LRU_PAD = LRU_TS // 2
MERGE_TM = 256
FFN_TM, FFN_TF = 512, 512


def _cparams(sem):
    return pltpu.CompilerParams(dimension_semantics=sem, vmem_limit_bytes=V7X_VMEM_LIMIT)


def _rms(x, g):
    return x * lax.rsqrt(jnp.mean(x * x, axis=-1, keepdims=True) + RMS_EPS) * g


def _softplus(y):
    return jnp.maximum(y, 0.0) + jnp.log1p(jnp.exp(-jnp.abs(y)))


def _norm_kernel(x_ref, g_ref, o_ref):
    o_ref[...] = _rms(x_ref[...], g_ref[...]).astype(BF16)


def _norm(x, g):
    t = x.shape[0]
    return pl.pallas_call(
        _norm_kernel,
        grid=(t // NORM_TM,),
        in_specs=[pl.BlockSpec((NORM_TM, D_MODEL), lambda i: (i, 0)), pl.BlockSpec((1, D_MODEL), lambda i: (0, 0))],
        out_specs=pl.BlockSpec((NORM_TM, D_MODEL), lambda i: (i, 0)),
        out_shape=jax.ShapeDtypeStruct((t, D_MODEL), BF16),
        compiler_params=_cparams(("parallel",)),
        name="norm0",
    )(x, g)


_NT = (((1,), (1,)), ((), ()))


def _proj_kernel(u_ref, wt_ref, z_ref):
    z_ref[...] = lax.dot_general(u_ref[...], wt_ref[...], _NT, preferred_element_type=F32).astype(BF16)


def _proj(u, wt, l, name):
    t = u.shape[0]
    n_cols = wt.shape[1]
    return pl.pallas_call(
        _proj_kernel,
        grid=(t // IN_TM, n_cols // IN_TN),
        in_specs=[
            pl.BlockSpec((IN_TM, D_MODEL), lambda i, j: (i, 0)),
            pl.BlockSpec((None, IN_TN, D_MODEL), lambda i, j: (l, j, 0)),
        ],
        out_specs=pl.BlockSpec((IN_TM, IN_TN), lambda i, j: (i, j)),
        out_shape=jax.ShapeDtypeStruct((t, n_cols), BF16),
        compiler_params=_cparams(("parallel", "arbitrary")),
        name=name,
    )(u, wt)


def _proj_gate_kernel(u_ref, wt_ref, wg_ref, wu_ref, wd_ref, z_ref, og_ref, ou_ref, od_ref):
    z_ref[...] = lax.dot_general(u_ref[...], wt_ref[...], _NT, preferred_element_type=F32).astype(BF16)
    og_ref[...] = wg_ref[...].astype(BF16)
    ou_ref[...] = wu_ref[...].astype(BF16)
    od_ref[...] = wd_ref[...].astype(BF16)


def _proj_gate(u, wt, l, w_gate, w_up, w_down):
    t = u.shape[0]
    n_i, n_j = t // IN_TM, ZG_W // IN_TN
    steps = n_i * n_j
    rows_gu = D_MODEL // steps
    rows_d = 2 * D_FF // steps
    assert rows_gu * steps == D_MODEL and rows_d * steps == 2 * D_FF and rows_gu % 16 == 0 and rows_d % 16 == 0
    step = lambda i, j: i * n_j + j
    d_blk = lambda i, j: jnp.minimum(step(i, j), steps // 2 - 1)
    return pl.pallas_call(
        _proj_gate_kernel,
        grid=(n_i, n_j),
        in_specs=[
            pl.BlockSpec((IN_TM, D_MODEL), lambda i, j: (i, 0)),
            pl.BlockSpec((None, IN_TN, D_MODEL), lambda i, j: (l, j, 0)),
            pl.BlockSpec((None, rows_gu, D_FF), lambda i, j: (l, step(i, j), 0)),
            pl.BlockSpec((None, rows_gu, D_FF), lambda i, j: (l, step(i, j), 0)),
            pl.BlockSpec((None, rows_d, D_MODEL), lambda i, j: (l, d_blk(i, j), 0)),
        ],
        out_specs=[
            pl.BlockSpec((IN_TM, IN_TN), lambda i, j: (i, j)),
            pl.BlockSpec((rows_gu, D_FF), lambda i, j: (step(i, j), 0)),
            pl.BlockSpec((rows_gu, D_FF), lambda i, j: (step(i, j), 0)),
            pl.BlockSpec((rows_d, D_MODEL), lambda i, j: (d_blk(i, j), 0)),
        ],
        out_shape=[
            jax.ShapeDtypeStruct((t, ZG_W), BF16),
            jax.ShapeDtypeStruct((D_MODEL, D_FF), BF16),
            jax.ShapeDtypeStruct((D_MODEL, D_FF), BF16),
            jax.ShapeDtypeStruct((D_FF, D_MODEL), BF16),
        ],
        compiler_params=_cparams(("arbitrary", "arbitrary")),
        name="proj_gate",
    )(u, wt, w_gate, w_up, w_down)


def _proj_d_kernel(u_ref, wt_ref, wcft_ref, z_ref, cf_ref):
    u = u_ref[...]
    z_ref[...] = lax.dot_general(u, wt_ref[...], _NT, preferred_element_type=F32).astype(BF16)
    cf_ref[...] = lax.dot_general(u, wcft_ref[...], _NT, preferred_element_type=F32)


def _proj_d(u, wt, wcft, l):
    t = u.shape[0]
    return pl.pallas_call(
        _proj_d_kernel,
        grid=(t // IN_TM,),
        in_specs=[
            pl.BlockSpec((IN_TM, D_MODEL), lambda i: (i, 0)),
            pl.BlockSpec((None, ZD_W, D_MODEL), lambda i: (l, 0, 0)),
            pl.BlockSpec((None, CF_PAD, D_MODEL), lambda i: (l, 0, 0)),
        ],
        out_specs=[
            pl.BlockSpec((IN_TM, ZD_W), lambda i: (i, 0)),
            pl.BlockSpec((IN_TM, CF_PAD), lambda i: (i, 0)),
        ],
        out_shape=[jax.ShapeDtypeStruct((t, ZD_W), BF16), jax.ShapeDtypeStruct((t, CF_PAD), F32)],
        compiler_params=_cparams(("parallel",)),
        name="proj_d",
    )(u, wt, wcft)


def _tri_tables(n):
    qi = [i for i in range(n) for _ in range(i + 1)]
    kj = [j for i in range(n) for j in range(i + 1)]
    return jnp.asarray(qi, jnp.int32), jnp.asarray(kj, jnp.int32)


def _flash_rows(q_ref, k_ref, v2_sc, bias, m_sc, acc_sc, r0, diag_q0, scale):
    rows = FLASH_ROWS
    nk = k_ref.shape[0] if diag_q0 is None else diag_q0 + rows
    s = lax.dot_general(q_ref[r0:r0 + rows, :], k_ref[0:nk, :], (((1,), (1,)), ((), ())),
                        preferred_element_type=F32)
    s = s * scale + bias[:, 0:nk]
    if diag_q0 is not None:
        qpos = lax.broadcasted_iota(jnp.int32, s.shape, 0) + diag_q0
        kpos = lax.broadcasted_iota(jnp.int32, s.shape, 1)
        s = jnp.where(kpos <= qpos, s, NEG_INF)
    m_prev = m_sc[r0:r0 + rows, :]
    m_new = jnp.maximum(m_prev, jnp.max(s, axis=-1, keepdims=True))
    alpha = jnp.exp2(m_prev - m_new)
    p = jnp.exp2(s - jnp.concatenate([m_new] * (nk // LANES), axis=1))
    pv = jnp.dot(p.astype(BF16), v2_sc[0:nk, :], preferred_element_type=F32)
    acc_sc[r0:r0 + rows, :] = jnp.concatenate([alpha, alpha], axis=1) * acc_sc[r0:r0 + rows, :] + pv
    m_sc[r0:r0 + rows, :] = m_new


def _flash_block(q_ref, k_ref, v2_sc, bias, m_sc, acc_sc, n_rows, tq, diag, scale):
    for r0 in range(0, n_rows, FLASH_ROWS):
        _flash_rows(q_ref, k_ref, v2_sc, bias, m_sc, acc_sc, r0, (r0 % tq) if diag else None, scale)


def _flash_init(v2_sc, m_sc, acc_sc):
    v2_sc[:, LANES:2 * LANES] = jnp.ones((v2_sc.shape[0], LANES), BF16)
    m_sc[...] = jnp.full(m_sc.shape, -jnp.inf, F32)
    acc_sc[...] = jnp.zeros(acc_sc.shape, F32)


def _diff_attn_kernel(qi_ref, kj_ref, slope_ref, lami_ref, q_ref, k_ref, v_ref, lv_ref, sg_ref, o_ref,
                      q2_sc, v2_sc, m_sc, acc_sc):
    tq = A_TQ
    t = pl.program_id(2)
    i = qi_ref[t]
    j = kj_ref[t]
    heads = [(hh, slice(hh * LANES, (hh + 1) * LANES)) for hh in range(A_HPS)]

    @pl.when(j == 0)
    def _():
        for hh, cols in heads:
            q = q_ref[:, cols]
            lane = lax.broadcasted_iota(jnp.int32, q.shape, 1)
            qs = q * jnp.asarray(A_HD ** -0.5, BF16)
            zero = jnp.zeros_like(qs)
            q2_sc[hh, 0:tq, :] = jnp.where(lane < A_HD, qs, zero)
            q2_sc[hh, tq:2 * tq, :] = jnp.where(lane >= A_HD, qs, zero)
            _flash_init(v2_sc.at[hh], m_sc.at[hh], acc_sc.at[hh])

    kpos = (lax.broadcasted_iota(jnp.int32, (1, tq), 1) + (j - i) * tq).astype(F32)

    def block(diag):
        for hh, cols in heads:
            v2_sc[hh, :, 0:LANES] = v_ref[:, cols]
            bias = kpos * (slope_ref[pl.program_id(1) * A_HPS + hh] * LOG2E)
            _flash_block(q2_sc.at[hh], k_ref.at[:, cols], v2_sc.at[hh], bias, m_sc.at[hh], acc_sc.at[hh],
                         2 * tq, tq, diag, LOG2E)

    @pl.when(j < i)
    def _():
        block(False)

    @pl.when(j == i)
    def _():
        block(True)
        lv = lv_ref[...]
        lam = (jnp.exp(jnp.sum(lv[0:1] * lv[1:2], axis=-1, keepdims=True))
               - jnp.exp(jnp.sum(lv[2:3] * lv[3:4], axis=-1, keepdims=True)) + lami_ref[0])
        for hh, cols in heads:
            acc = acc_sc[hh]
            o = acc[0:tq, 0:LANES] / acc[0:tq, LANES:2 * LANES] - lam * (acc[tq:2 * tq, 0:LANES] / acc[tq:2 * tq, LANES:2 * LANES])
            o_ref[:, cols] = (_rms(o, sg_ref[...]) * lami_ref[1]).astype(BF16)


def _diff_attn(z, lvec, subln, lam_init, batch, seq):
    nq = seq // A_TQ
    qi, kj = _tri_tables(nq)
    slopes = jnp.asarray(np.exp2(-8.0 * np.arange(1, A_HEADS + 1, dtype=np.float32) / A_HEADS), F32)
    w = LANES * A_HPS
    qc, kc, vc = AQ0 // w, AK0 // w, AV0 // w
    smem = pl.BlockSpec(memory_space=pltpu.SMEM)
    grid_spec = pltpu.PrefetchScalarGridSpec(
        num_scalar_prefetch=2,
        grid=(batch, A_HEADS // A_HPS, int(qi.shape[0])),
        in_specs=[
            smem, smem,
            pl.BlockSpec((A_TQ, w), lambda b, h, t, qi, kj: (b * nq + qi[t], qc + h)),
            pl.BlockSpec((A_TQ, w), lambda b, h, t, qi, kj: (b * nq + kj[t], kc + h)),
            pl.BlockSpec((A_TQ, w), lambda b, h, t, qi, kj: (b * nq + kj[t], vc + h)),
            pl.BlockSpec((4, A_HD), lambda b, h, t, qi, kj: (0, 0)),
            pl.BlockSpec((1, 2 * A_HD), lambda b, h, t, qi, kj: (0, 0)),
        ],
        out_specs=pl.BlockSpec((A_TQ, w), lambda b, h, t, qi, kj: (b * nq + qi[t], h)),
        scratch_shapes=[
            pltpu.VMEM((A_HPS, 2 * A_TQ, LANES), BF16),
            pltpu.VMEM((A_HPS, A_TQ, 2 * LANES), BF16),
            pltpu.VMEM((A_HPS, 2 * A_TQ, LANES), F32),
            pltpu.VMEM((A_HPS, 2 * A_TQ, 2 * LANES), F32),
        ],
    )
    return pl.pallas_call(
        _diff_attn_kernel,
        grid_spec=grid_spec,
        out_shape=jax.ShapeDtypeStruct((batch * seq, BRANCH_W), BF16),
        compiler_params=_cparams(("parallel", "parallel", "arbitrary")),
        name="diff_attn",
    )(qi, kj, slopes, lam_init, z, z, z, lvec, subln)


def _fox_cum_kernel(cf_ref, bf_ref, o_ref):
    x = cf_ref[...] + bf_ref[...]
    logf = jnp.minimum(x, 0.0) - jnp.log1p(jnp.exp(-jnp.abs(x)))
    rows, seq = logf.shape
    lane = lax.broadcasted_iota(jnp.int32, (rows, LANES), 1)
    carry = jnp.zeros((rows, 1), F32)
    for c in range(seq // LANES):
        y = logf[:, c * LANES:(c + 1) * LANES]
        sh = 1
        while sh < LANES:
            y = y + jnp.where(lane >= sh, pltpu.roll(y, sh, axis=1), 0.0)
            sh *= 2
        y = y + carry
        o_ref[:, c * LANES:(c + 1) * LANES] = y
        carry = y[:, LANES - 1:LANES]


def _fox_cum(cf_t, bf_t):
    rows, seq = cf_t.shape
    return pl.pallas_call(
        _fox_cum_kernel,
        grid=(1,),
        in_specs=[pl.BlockSpec((rows, seq), lambda i: (0, 0)), pl.BlockSpec((rows, 1), lambda i: (0, 0))],
        out_specs=pl.BlockSpec((rows, seq), lambda i: (0, 0)),
        out_shape=jax.ShapeDtypeStruct((rows, seq), F32),
        compiler_params=_cparams(("arbitrary",)),
        name="fox_cum",
    )(cf_t, bf_t)


def _fox_attn_kernel(qi_ref, kj_ref, q_ref, k_ref, v_ref, cum_ref, o_ref, v2_sc, m_sc, acc_sc):
    tq = C_TQ
    t = pl.program_id(2)
    i = qi_ref[t]
    j = kj_ref[t]
    heads = [(hh, slice(hh * LANES, (hh + 1) * LANES)) for hh in range(C_HPS)]

    @pl.when(j == 0)
    def _():
        for hh, _ in heads:
            _flash_init(v2_sc.at[hh], m_sc.at[hh], acc_sc.at[hh])

    def block(diag):
        for hh, cols in heads:
            v2_sc[hh, :, 0:LANES] = v_ref[:, cols]
            _flash_block(q_ref.at[:, cols], k_ref.at[:, cols], v2_sc.at[hh], cum_ref[hh] * (-LOG2E), m_sc.at[hh],
                         acc_sc.at[hh], tq, tq, diag, C_HD ** -0.5 * LOG2E)

    @pl.when(j < i)
    def _():
        block(False)

    @pl.when(j == i)
    def _():
        block(True)
        for hh, cols in heads:
            acc = acc_sc[hh]
            o_ref[:, cols] = (acc[:, 0:LANES] / acc[:, LANES:2 * LANES]).astype(BF16)


def _fox_attn(z, cum, batch, seq):
    nq = seq // C_TQ
    qi, kj = _tri_tables(nq)
    w = LANES * C_HPS
    qc, kc, vc = CQ0 // w, CK0 // w, CV0 // w
    hg = C_HEADS // C_HPS
    grid_spec = pltpu.PrefetchScalarGridSpec(
        num_scalar_prefetch=2,
        grid=(batch, hg, int(qi.shape[0])),
        in_specs=[
            pl.BlockSpec((C_TQ, w), lambda b, h, t, qi, kj: (b * nq + qi[t], qc + h)),
            pl.BlockSpec((C_TQ, w), lambda b, h, t, qi, kj: (b * nq + kj[t], kc + h)),
            pl.BlockSpec((C_TQ, w), lambda b, h, t, qi, kj: (b * nq + kj[t], vc + h)),
            pl.BlockSpec((C_HPS, 1, C_TQ), lambda b, h, t, qi, kj: (b * hg + h, 0, kj[t])),
        ],
        out_specs=pl.BlockSpec((C_TQ, w), lambda b, h, t, qi, kj: (b * nq + qi[t], h)),
        scratch_shapes=[
            pltpu.VMEM((C_HPS, C_TQ, 2 * LANES), BF16),
            pltpu.VMEM((C_HPS, C_TQ, LANES), F32),
            pltpu.VMEM((C_HPS, C_TQ, 2 * LANES), F32),
        ],
    )
    return pl.pallas_call(
        _fox_attn_kernel,
        grid_spec=grid_spec,
        out_shape=jax.ShapeDtypeStruct((batch * seq, BRANCH_W), BF16),
        compiler_params=_cparams(("parallel", "parallel", "arbitrary")),
        name="fox_attn",
    )(qi, kj, z, z, z, cum)


def _swa_kernel(sink_ref, q_ref, kc_ref, kp_ref, vc_ref, vp_ref, bias_ref, o_ref):
    blk = SWA_BLOCK
    first = pl.program_id(1) == 0
    swap = lambda x: jnp.concatenate([x[:, D_HD:2 * D_HD], x[:, 0:D_HD]], axis=1)
    ones = jnp.ones((blk * (SWA_QB + 1), LANES), BF16)
    kk_all = jnp.concatenate([kp_ref[...], kc_ref[...]], axis=0)
    vv_all = jnp.concatenate([vp_ref[...], vc_ref[...]], axis=0)
    kk_var = (kk_all, swap(kk_all))
    vv_var = (jnp.concatenate([vv_all, ones], axis=1), jnp.concatenate([swap(vv_all), ones], axis=1))
    q_all = q_ref[...] * jnp.asarray(D_HD ** -0.5, BF16)
    q_lane = lax.broadcasted_iota(jnp.int32, (2 * blk, LANES), 1)
    o_lane = lax.broadcasted_iota(jnp.int32, (blk, LANES), 1)
    for qb in range(SWA_QB):
        keys = slice(qb * blk, (qb + 2) * blk)
        for kv in range(D_KV):
            res = []
            for half in range(2):
                q2 = jnp.concatenate([q_all[qb * blk:(qb + 1) * blk, (2 * kv + tt) * LANES:(2 * kv + tt + 1) * LANES]
                                      for tt in range(2)], axis=0)
                q2 = jnp.where((q_lane < D_HD) if half == 0 else (q_lane >= D_HD), q2, jnp.zeros_like(q2))
                var = half ^ kv
                s = lax.dot_general(q2, kk_var[var][keys], (((1,), (1,)), ((), ())), preferred_element_type=F32)
                bias = bias_ref[1, kv, half]
                if qb == 0:
                    bias = jnp.where(first, bias_ref[0, kv, half], bias)
                lg = s + bias
                es, sinks = [], []
                for tt in range(2):
                    sink = sink_ref[4 * kv + half + 2 * tt]
                    lg_h = lg[tt * blk:(tt + 1) * blk]
                    m = jnp.maximum(jnp.max(lg_h, axis=-1, keepdims=True), sink)
                    es.append(jnp.exp(lg_h - m))
                    sinks.append(jnp.exp(sink - m))
                pv = jnp.dot(jnp.concatenate(es, axis=0).astype(BF16), vv_var[var][keys],
                             preferred_element_type=F32)
                res.append([pv[tt * blk:(tt + 1) * blk, 0:LANES] / (pv[tt * blk:(tt + 1) * blk, LANES:2 * LANES] + sinks[tt])
                            for tt in range(2)])
            for tt in range(2):
                tile = jnp.where(o_lane < D_HD, res[0][tt], res[1][tt])
                o_ref[qb * blk:(qb + 1) * blk, (2 * kv + tt) * LANES:(2 * kv + tt + 1) * LANES] = tile.astype(BF16)


def _swa_bias():
    slopes = np.exp2(-8.0 * np.arange(1, D_HEADS + 1, dtype=np.float32) / D_HEADS).astype(np.float32)
    qi = np.arange(SWA_BLOCK)
    kj = np.arange(2 * SWA_BLOCK) - SWA_BLOCK
    dist = (qi[:, None] - kj[None, :]).astype(np.float32)
    valid = (dist >= 0) & (dist < WINDOW)
    general = np.where(valid[None], -(slopes[:, None, None] * dist[None]), np.float32(NEG_INF))
    first = np.where((kj >= 0)[None, None, :], general, np.float32(NEG_INF))
    per_head = np.stack([first, general]).astype(np.float32)
    table = np.stack([np.stack([np.concatenate([per_head[:, 4 * kv + half], per_head[:, 4 * kv + half + 2]], axis=1)
                                for half in range(2)], axis=1) for kv in range(D_KV)], axis=1)
    return jnp.asarray(table)


def _swa(z, sinks, batch, seq):
    rows = SWA_QB * SWA_BLOCK
    ns = seq // rows
    qc, kc, vc = DQ0 // BRANCH_W, DK0 // LANES, DV0 // LANES
    cur = lambda b, n: b * ns + n
    prev = lambda b, n: jnp.maximum((b * ns + n) * SWA_QB - 1, 0)
    return pl.pallas_call(
        _swa_kernel,
        grid=(batch, ns),
        in_specs=[
            pl.BlockSpec(memory_space=pltpu.SMEM),
            pl.BlockSpec((rows, BRANCH_W), lambda b, n: (cur(b, n), qc)),
            pl.BlockSpec((rows, LANES), lambda b, n: (cur(b, n), kc)),
            pl.BlockSpec((SWA_BLOCK, LANES), lambda b, n: (prev(b, n), kc)),
            pl.BlockSpec((rows, LANES), lambda b, n: (cur(b, n), vc)),
            pl.BlockSpec((SWA_BLOCK, LANES), lambda b, n: (prev(b, n), vc)),
            pl.BlockSpec((2, D_KV, 2, 2 * SWA_BLOCK, 2 * SWA_BLOCK), lambda b, n: (0, 0, 0, 0, 0)),
        ],
        out_specs=pl.BlockSpec((rows, BRANCH_W), lambda b, n: (cur(b, n), 0)),
        out_shape=jax.ShapeDtypeStruct((batch * seq, BRANCH_W), BF16),
        compiler_params=_cparams(("parallel", "arbitrary")),
        name="swa",
    )(sinks, z, z, z, z, z, _swa_bias())


def _lru_kernel(x_ref, g_ref, cw_ref, cb_ref, wa_ref, ba_ref, wx_ref, bx_ref, lam_ref, o_ref,
                xpad_sc, apad_sc, upad_sc, hc_sc):
    ts, pad = LRU_TS, LRU_PAD

    @pl.when(pl.program_id(1) == 0)
    def _():
        xpad_sc[0:8, :] = jnp.zeros((8, BRANCH_W), F32)
        hc_sc[...] = jnp.zeros(hc_sc.shape, F32)
        apad_sc[0:pad, :] = jnp.ones((pad, BRANCH_W), F32)
        upad_sc[0:pad, :] = jnp.zeros((pad, BRANCH_W), F32)

    x = x_ref[...].astype(F32)
    xpad_sc[8:8 + ts, :] = x
    cw = cw_ref[...]
    xc = (cw[3:4] * x + cw[2:3] * xpad_sc[7:7 + ts, :] + cw[1:2] * xpad_sc[6:6 + ts, :]
          + cw[0:1] * xpad_sc[5:5 + ts, :] + cb_ref[...])
    xpad_sc[0:8, :] = x[ts - 8:ts]

    xcb = xc.astype(BF16)
    r = jax.nn.sigmoid(jnp.dot(xcb, wa_ref[...], preferred_element_type=F32) + ba_ref[...])
    gi = jax.nn.sigmoid(jnp.dot(xcb, wx_ref[...], preferred_element_type=F32) + bx_ref[...])
    log_a = (-B_C * r) * _softplus(-lam_ref[...])
    a = jnp.exp(log_a)
    u = jnp.sqrt(1.0 - a * a) * (gi * xc)

    sh = 1
    while sh < ts:
        apad_sc[pad:pad + ts, :] = a
        upad_sc[pad:pad + ts, :] = u
        u = a * upad_sc[pad - sh:pad - sh + ts, :] + u
        a = a * apad_sc[pad - sh:pad - sh + ts, :]
        sh *= 2
    h = a * hc_sc[...] + u
    hc_sc[...] = h[ts - 1:ts]
    g = g_ref[...].astype(F32)
    gelu = g * jax.nn.sigmoid((2.0 * math.sqrt(2.0 / math.pi)) * (g + 0.044715 * (g * g * g)))
    o_ref[...] = (gelu * h).astype(BF16)


def _lru(z, cw, cb, wa, ba, wx, bx, lam, l, batch, seq):
    ns = seq // LRU_TS
    xc_, gc_ = BX0 // BRANCH_W, BG0 // BRANCH_W
    const = lambda shape: pl.BlockSpec(shape, lambda b, s: (0, 0))
    return pl.pallas_call(
        _lru_kernel,
        grid=(batch, ns),
        in_specs=[
            pl.BlockSpec((LRU_TS, BRANCH_W), lambda b, s: (b * ns + s, xc_)),
            pl.BlockSpec((LRU_TS, BRANCH_W), lambda b, s: (b * ns + s, gc_)),
            const((B_CONV, BRANCH_W)), const((1, BRANCH_W)),
            pl.BlockSpec((None, BRANCH_W, BRANCH_W), lambda b, s: (l, 0, 0)), const((1, BRANCH_W)),
            pl.BlockSpec((None, BRANCH_W, BRANCH_W), lambda b, s: (l, 0, 0)), const((1, BRANCH_W)),
            const((1, BRANCH_W)),
        ],
        out_specs=pl.BlockSpec((LRU_TS, BRANCH_W), lambda b, s: (b * ns + s, 0)),
        out_shape=jax.ShapeDtypeStruct((batch * seq, BRANCH_W), BF16),
        scratch_shapes=[
            pltpu.VMEM((LRU_TS + 8, BRANCH_W), F32),
            pltpu.VMEM((LRU_TS + LRU_PAD, BRANCH_W), F32),
            pltpu.VMEM((LRU_TS + LRU_PAD, BRANCH_W), F32),
            pltpu.VMEM((1, BRANCH_W), F32),
        ],
        compiler_params=_cparams(("parallel", "arbitrary")),
        name="lru",
    )(z, z, cw, cb, wa, ba, wx, bx, lam)


def _merge_kernel(h_ref, ya_ref, yb_ref, yc_ref, yd_ref, zg_ref, bg_ref, wb_ref, wo_ref, gn_ref, o_ref, v_ref):
    ys = (ya_ref, yb_ref, yc_ref, yd_ref)
    mixed = None
    for n in range(N_BRANCH):
        proj = jnp.dot(ys[n][...], wb_ref[n], preferred_element_type=F32)
        gate = zg_ref[:, n * D_MODEL:(n + 1) * D_MODEL].astype(F32) + bg_ref[n:n + 1, :]
        term = jax.nn.sigmoid(gate) * proj
        mixed = term if mixed is None else mixed + term
    o = h_ref[...] + jnp.dot(mixed.astype(BF16), wo_ref[...], preferred_element_type=F32)
    o_ref[...] = o
    v_ref[...] = _rms(o, gn_ref[...]).astype(BF16)


def _merge(h, ya, yb, yc, yd, zg, bg, wb, wo, gn, l):
    t = h.shape[0]
    row = lambda shape: pl.BlockSpec(shape, lambda i: (i, 0))
    return pl.pallas_call(
        _merge_kernel,
        grid=(t // MERGE_TM,),
        in_specs=[
            row((MERGE_TM, D_MODEL)),
            row((MERGE_TM, BRANCH_W)), row((MERGE_TM, BRANCH_W)), row((MERGE_TM, BRANCH_W)),
            row((MERGE_TM, BRANCH_W)),
            row((MERGE_TM, ZG_W)),
            pl.BlockSpec((N_BRANCH, D_MODEL), lambda i: (0, 0)),
            pl.BlockSpec((None, N_BRANCH, BRANCH_W, D_MODEL), lambda i: (l, 0, 0, 0), pipeline_mode=pl.Buffered(1)),
            pl.BlockSpec((None, D_MODEL, D_MODEL), lambda i: (l, 0, 0), pipeline_mode=pl.Buffered(1)),
            pl.BlockSpec((1, D_MODEL), lambda i: (0, 0)),
        ],
        out_specs=[row((MERGE_TM, D_MODEL)), row((MERGE_TM, D_MODEL))],
        out_shape=[jax.ShapeDtypeStruct((t, D_MODEL), F32), jax.ShapeDtypeStruct((t, D_MODEL), BF16)],
        compiler_params=_cparams(("parallel",)),
        name="merge",
    )(h, ya, yb, yc, yd, zg, bg, wb, wo, gn)


def _ffn_kernel(v_ref, h_ref, wg_ref, wu_ref, wd_ref, gn_ref, o_ref, *u_ref, final_norm):
    j = pl.program_id(1)

    @pl.when(j == 0)
    def _():
        o_ref[...] = h_ref[...]

    v = v_ref[...]
    a = jnp.dot(v, wg_ref[...], preferred_element_type=F32)
    b = jnp.dot(v, wu_ref[...], preferred_element_type=F32)
    t = (a * jax.nn.sigmoid(a)) * b
    o_ref[...] += jnp.dot(t.astype(BF16), wd_ref[...], preferred_element_type=F32)

    @pl.when(j == pl.num_programs(1) - 1)
    def _():
        y = _rms(o_ref[...], gn_ref[...])
        if final_norm:
            o_ref[...] = y
        else:
            u_ref[0][...] = y.astype(BF16)


def _ffn(v, h, wg, wu, wd, gn, final_norm):
    t = h.shape[0]
    row = lambda: pl.BlockSpec((FFN_TM, D_MODEL), lambda i, j: (i, 0))
    out_specs = [row()] if final_norm else [row(), row()]
    out_shape = [jax.ShapeDtypeStruct((t, D_MODEL), F32)]
    if not final_norm:
        out_shape.append(jax.ShapeDtypeStruct((t, D_MODEL), BF16))
    return pl.pallas_call(
        functools.partial(_ffn_kernel, final_norm=final_norm),
        grid=(t // FFN_TM, D_FF // FFN_TF),
        in_specs=[
            row(), row(),
            pl.BlockSpec((D_MODEL, FFN_TF), lambda i, j: (0, j)),
            pl.BlockSpec((D_MODEL, FFN_TF), lambda i, j: (0, j)),
            pl.BlockSpec((FFN_TF, D_MODEL), lambda i, j: (j, 0)),
            pl.BlockSpec((1, D_MODEL), lambda i, j: (0, 0)),
        ],
        out_specs=out_specs,
        out_shape=out_shape,
        compiler_params=_cparams(("parallel", "arbitrary")),
        name="ffn_final" if final_norm else "ffn",
    )(v, h, wg, wu, wd, gn)


def _block_diag(w):
    depth, nb, c, d = w.shape
    eye = jnp.eye(nb, dtype=w.dtype)
    return jnp.einsum('lncd,nm->lncmd', w, eye).reshape(depth, nb * c, nb * d)


def kernel(x, norm_mix, w_in, b_gate, diff_lq1, diff_lk1, diff_lq2, diff_lk2, diff_subln, lru_conv_w, lru_conv_b,
           lru_wa, lru_ba, lru_wx, lru_bx, lru_lambda, fox_b_f, swa_sinks, w_branch, w_out, norm_ffn, w_ffn_gate,
           w_ffn_up, w_ffn_down, norm_final):
    batch, seq, d = x.shape
    depth = w_in.shape[0]
    assert d == D_MODEL and seq % max(A_TQ, C_TQ, LRU_TS, SWA_QB * SWA_BLOCK) == 0 and (batch * seq) % IN_TM == 0
    t = batch * seq

    piece = lambda lo, hi: jnp.swapaxes(w_in[:, :, lo:hi], 1, 2).astype(BF16)
    wt_zm = piece(0, ZM_W)
    wt_gz = piece(_R_GZ0, _R_GZ0 + ZG_W)
    wt_d = piece(_R_DQ0, _R_GZ0)
    wt_cf = jnp.pad(piece(_R_CF0, _R_DQ0), ((0, 0), (0, CF_PAD - C_HEADS), (0, 0)))
    wb = w_branch.astype(BF16)
    wo = w_out.astype(BF16)
    wa = _block_diag(lru_wa).astype(BF16)
    wx = _block_diag(lru_wx).astype(BF16)
    lvec = jnp.stack([diff_lq1, diff_lk1, diff_lq2, diff_lk2], axis=1).astype(F32)
    bg = b_gate.reshape(depth, N_BRANCH, D_MODEL)
    gf = norm_final.reshape(1, D_MODEL)

    h = x.reshape(t, D_MODEL)
    u = _norm(h, norm_mix[0].reshape(1, D_MODEL))
    for l in range(depth):
        lam_init = 0.8 - 0.6 * math.exp(-0.3 * l)
        zg, wg, wu, wd = _proj_gate(u, wt_gz, l, w_ffn_gate, w_ffn_up, w_ffn_down)
        zm = _proj(u, wt_zm, l, "proj_mix")
        zd, cf = _proj_d(u, wt_d, wt_cf, l)

        ya = _diff_attn(zm, lvec[l], diff_subln[l].reshape(1, 2 * A_HD),
                        jnp.asarray([lam_init, 1.0 - lam_init], F32), batch, seq)
        yb = _lru(zm, lru_conv_w[l], lru_conv_b[l].reshape(1, -1), wa, lru_ba[l].reshape(1, -1), wx,
                  lru_bx[l].reshape(1, -1), lru_lambda[l].reshape(1, -1), l, batch, seq)
        cf_t = cf[:, :C_HEADS].reshape(batch, seq, C_HEADS).transpose(0, 2, 1).reshape(batch * C_HEADS, seq)
        bf_t = jnp.tile(fox_b_f[l], batch).reshape(batch * C_HEADS, 1)
        cum = _fox_cum(cf_t, bf_t).reshape(batch * C_HEADS, 1, seq)
        yc = _fox_attn(zm, cum, batch, seq)
        yd = _swa(zd, swa_sinks[l], batch, seq)

        h, v = _merge(h, ya, yb, yc, yd, zg, bg[l], wb, wo, norm_ffn[l].reshape(1, D_MODEL), l)
        if l == depth - 1:
            (h,) = _ffn(v, h, wg, wu, wd, gf, final_norm=True)
        else:
            h, u = _ffn(v, h, wg, wu, wd, norm_mix[l + 1].reshape(1, D_MODEL), final_norm=False)
    return h.reshape(batch, seq, D_MODEL)
```

```python
import functools
import math

import jax
import jax.numpy as jnp
import numpy as np
from jax import lax
from jax.experimental import pallas as pl
from jax.experimental.pallas import tpu as pltpu

F32 = jnp.float32
BF16 = jnp.bfloat16
LANES = 128
BF16_SUBLANES = 16

D_MODEL = 2048
N_BRANCH = 4
BRANCH_W = D_MODEL // 4
A_HEADS = 4
A_HD = BRANCH_W // (2 * A_HEADS)
B_BLOCKS = 8
B_CONV = 4
B_C = 8.0
C_HEADS = 4
C_HD = BRANCH_W // C_HEADS
D_HEADS = 8
D_KV = 2
D_GROUP = D_HEADS // D_KV
D_HD = BRANCH_W // D_HEADS
WINDOW = 128
SWA_BLOCK = 128
SWA_QB = 4
D_FF = -(-8 * D_MODEL // (3 * 256)) * 256
RMS_EPS = 1e-6
NEG_INF = -1e30

AQ0 = 0
AK0 = AQ0 + BRANCH_W
AV0 = AK0 + BRANCH_W
BX0 = AV0 + BRANCH_W
BG0 = BX0 + BRANCH_W
CQ0 = BG0 + BRANCH_W
CK0 = CQ0 + BRANCH_W
CV0 = CK0 + BRANCH_W
ZM_W = CV0 + BRANCH_W
DQ0 = 0
DK0 = DQ0 + BRANCH_W
DV0 = DK0 + D_KV * D_HD
ZD_W = DV0 + D_KV * D_HD
ZG_W = N_BRANCH * D_MODEL
CF_PAD = 128

_R_CF0 = ZM_W
_R_DQ0 = _R_CF0 + C_HEADS
_R_GZ0 = _R_DQ0 + ZD_W

V7X_VMEM_LIMIT = 56 * 1024 * 1024

NORM_TM = 1024
IN_TM, IN_TN = 2048, 1024
FLASH_ROWS = 256
A_TQ, A_HPS = 1024, 4
C_TQ, C_HPS = 1024, 4
LOG2E = math.log2(math.e)
LRU_TS = 512
LRU_PAD = LRU_TS // 2
MERGE_TM = 256
FFN_TM, FFN_TF = 512, 512


def _cparams(sem):
    return pltpu.CompilerParams(dimension_semantics=sem, vmem_limit_bytes=V7X_VMEM_LIMIT)


def _rms(x, g):
    return x * lax.rsqrt(jnp.mean(x * x, axis=-1, keepdims=True) + RMS_EPS) * g


def _softplus(y):
    return jnp.maximum(y, 0.0) + jnp.log1p(jnp.exp(-jnp.abs(y)))


def _norm_kernel(x_ref, g_ref, o_ref):
    o_ref[...] = _rms(x_ref[...], g_ref[...]).astype(BF16)


def _norm(x, g):
    t = x.shape[0]
    return pl.pallas_call(
        _norm_kernel,
        grid=(t // NORM_TM,),
        in_specs=[pl.BlockSpec((NORM_TM, D_MODEL), lambda i: (i, 0)), pl.BlockSpec((1, D_MODEL), lambda i: (0, 0))],
        out_specs=pl.BlockSpec((NORM_TM, D_MODEL), lambda i: (i, 0)),
        out_shape=jax.ShapeDtypeStruct((t, D_MODEL), BF16),
        compiler_params=_cparams(("parallel",)),
        name="norm0",
    )(x, g)


_NT = (((1,), (1,)), ((), ()))


def _proj_kernel(u_ref, wt_ref, *rest):
    n = len(rest) // 2
    z_ref = rest[n]
    z_ref[...] = lax.dot_general(u_ref[...], wt_ref[...], _NT, preferred_element_type=F32).astype(BF16)
    for w_ref, o_ref in zip(rest[:n], rest[n + 1:]):
        o_ref[...] = w_ref[...].astype(BF16)


def _cast_rows(n_rows, steps):
    for n_blocks in range(steps, 0, -1):
        if n_rows % n_blocks == 0 and (n_rows // n_blocks) % BF16_SUBLANES == 0:
            return n_rows // n_blocks, n_blocks
    raise ValueError("no bf16-tile-aligned row split")


def _proj(u, wt, l, name, cast=()):
    t = u.shape[0]
    n_cols = wt.shape[1]
    n_i, n_j = t // IN_TM, n_cols // IN_TN
    in_specs = [
        pl.BlockSpec((IN_TM, D_MODEL), lambda i, j: (i, 0)),
        pl.BlockSpec((None, IN_TN, D_MODEL), lambda i, j: (l, j, 0)),
    ]
    out_specs = [pl.BlockSpec((IN_TM, IN_TN), lambda i, j: (i, j))]
    out_shape = [jax.ShapeDtypeStruct((t, n_cols), BF16)]
    for w in cast:
        _, n_rows, n_c = w.shape
        rows, n_blocks = _cast_rows(n_rows, n_i * n_j)
        blk = lambda i, j, n_blocks=n_blocks: jnp.minimum(i * n_j + j, n_blocks - 1)
        in_specs.append(pl.BlockSpec((None, rows, n_c), lambda i, j, blk=blk: (l, blk(i, j), 0)))
        out_specs.append(pl.BlockSpec((rows, n_c), lambda i, j, blk=blk: (blk(i, j), 0)))
        out_shape.append(jax.ShapeDtypeStruct((n_rows, n_c), BF16))
    out = pl.pallas_call(
        _proj_kernel,
        grid=(n_i, n_j),
        in_specs=in_specs,
        out_specs=out_specs,
        out_shape=out_shape,
        compiler_params=_cparams(("arbitrary", "arbitrary")),
        name=name,
    )(u, wt, *cast)
    return out if cast else out[0]


def _proj_d_kernel(u_ref, wt_ref, wcft_ref, z_ref, cf_ref):
    u = u_ref[...]
    z_ref[...] = lax.dot_general(u, wt_ref[...], _NT, preferred_element_type=F32).astype(BF16)
    cf_ref[...] = lax.dot_general(u, wcft_ref[...], _NT, preferred_element_type=F32)


def _proj_d(u, wt, wcft, l):
    t = u.shape[0]
    return pl.pallas_call(
        _proj_d_kernel,
        grid=(t // IN_TM,),
        in_specs=[
            pl.BlockSpec((IN_TM, D_MODEL), lambda i: (i, 0)),
            pl.BlockSpec((None, ZD_W, D_MODEL), lambda i: (l, 0, 0)),
            pl.BlockSpec((None, CF_PAD, D_MODEL), lambda i: (l, 0, 0)),
        ],
        out_specs=[
            pl.BlockSpec((IN_TM, ZD_W), lambda i: (i, 0)),
            pl.BlockSpec((IN_TM, CF_PAD), lambda i: (i, 0)),
        ],
        out_shape=[jax.ShapeDtypeStruct((t, ZD_W), BF16), jax.ShapeDtypeStruct((t, CF_PAD), F32)],
        compiler_params=_cparams(("parallel",)),
        name="proj_d",
    )(u, wt, wcft)


def _tri_tables(n):
    qi = [i for i in range(n) for _ in range(i + 1)]
    kj = [j for i in range(n) for j in range(i + 1)]
    return jnp.asarray(qi, jnp.int32), jnp.asarray(kj, jnp.int32)


def _flash_rows(q_ref, k_ref, v2_sc, bias, m_sc, acc_sc, r0, diag_q0, scale):
    rows = FLASH_ROWS
    nk = k_ref.shape[0] if diag_q0 is None else diag_q0 + rows
    s = lax.dot_general(q_ref[r0:r0 + rows, :], k_ref[0:nk, :], (((1,), (1,)), ((), ())),
                        preferred_element_type=F32)
    s = s * scale + bias[:, 0:nk]
    if diag_q0 is not None:
        qpos = lax.broadcasted_iota(jnp.int32, s.shape, 0) + diag_q0
        kpos = lax.broadcasted_iota(jnp.int32, s.shape, 1)
        s = jnp.where(kpos <= qpos, s, NEG_INF)
    m_prev = m_sc[r0:r0 + rows, :]
    m_new = jnp.maximum(m_prev, jnp.max(s, axis=-1, keepdims=True))
    alpha = jnp.exp2(m_prev - m_new)
    p = jnp.exp2(s - jnp.concatenate([m_new] * (nk // LANES), axis=1))
    pv = jnp.dot(p.astype(BF16), v2_sc[0:nk, :], preferred_element_type=F32)
    acc_sc[r0:r0 + rows, :] = jnp.concatenate([alpha, alpha], axis=1) * acc_sc[r0:r0 + rows, :] + pv
    m_sc[r0:r0 + rows, :] = m_new


def _flash_block(q_ref, k_ref, v2_sc, bias, m_sc, acc_sc, n_rows, tq, diag, scale):
    for r0 in range(0, n_rows, FLASH_ROWS):
        _flash_rows(q_ref, k_ref, v2_sc, bias, m_sc, acc_sc, r0, (r0 % tq) if diag else None, scale)


def _flash_init(v2_sc, m_sc, acc_sc):
    v2_sc[:, LANES:2 * LANES] = jnp.ones((v2_sc.shape[0], LANES), BF16)
    m_sc[...] = jnp.full(m_sc.shape, -jnp.inf, F32)
    acc_sc[...] = jnp.zeros(acc_sc.shape, F32)


def _diff_attn_kernel(qi_ref, kj_ref, slope_ref, lami_ref, q_ref, k_ref, v_ref, lv_ref, sg_ref, o_ref,
                      q2_sc, v2_sc, m_sc, acc_sc):
    tq = A_TQ
    t = pl.program_id(2)
    i = qi_ref[t]
    j = kj_ref[t]
    heads = [(hh, slice(hh * LANES, (hh + 1) * LANES)) for hh in range(A_HPS)]

    @pl.when(j == 0)
    def _():
        for hh, cols in heads:
            q = q_ref[:, cols]
            lane = lax.broadcasted_iota(jnp.int32, q.shape, 1)
            qs = q * jnp.asarray(A_HD ** -0.5, BF16)
            zero = jnp.zeros_like(qs)
            q2_sc[hh, 0:tq, :] = jnp.where(lane < A_HD, qs, zero)
            q2_sc[hh, tq:2 * tq, :] = jnp.where(lane >= A_HD, qs, zero)
            _flash_init(v2_sc.at[hh], m_sc.at[hh], acc_sc.at[hh])

    kpos = (lax.broadcasted_iota(jnp.int32, (1, tq), 1) + (j - i) * tq).astype(F32)

    def block(diag):
        for hh, cols in heads:
            v2_sc[hh, :, 0:LANES] = v_ref[:, cols]
            bias = kpos * (slope_ref[pl.program_id(1) * A_HPS + hh] * LOG2E)
            _flash_block(q2_sc.at[hh], k_ref.at[:, cols], v2_sc.at[hh], bias, m_sc.at[hh], acc_sc.at[hh],
                         2 * tq, tq, diag, LOG2E)

    @pl.when(j < i)
    def _():
        block(False)

    @pl.when(j == i)
    def _():
        block(True)
        lv = lv_ref[...]
        lam = (jnp.exp(jnp.sum(lv[0:1] * lv[1:2], axis=-1, keepdims=True))
               - jnp.exp(jnp.sum(lv[2:3] * lv[3:4], axis=-1, keepdims=True)) + lami_ref[0])
        for hh, cols in heads:
            acc = acc_sc[hh]
            o = acc[0:tq, 0:LANES] / acc[0:tq, LANES:2 * LANES] - lam * (acc[tq:2 * tq, 0:LANES] / acc[tq:2 * tq, LANES:2 * LANES])
            o_ref[:, cols] = (_rms(o, sg_ref[...]) * lami_ref[1]).astype(BF16)


def _diff_attn(z, lvec, subln, lam_init, batch, seq):
    nq = seq // A_TQ
    qi, kj = _tri_tables(nq)
    slopes = jnp.asarray(np.exp2(-8.0 * np.arange(1, A_HEADS + 1, dtype=np.float32) / A_HEADS), F32)
    w = LANES * A_HPS
    qc, kc, vc = AQ0 // w, AK0 // w, AV0 // w
    smem = pl.BlockSpec(memory_space=pltpu.SMEM)
    grid_spec = pltpu.PrefetchScalarGridSpec(
        num_scalar_prefetch=2,
        grid=(batch, A_HEADS // A_HPS, int(qi.shape[0])),
        in_specs=[
            smem, smem,
            pl.BlockSpec((A_TQ, w), lambda b, h, t, qi, kj: (b * nq + qi[t], qc + h)),
            pl.BlockSpec((A_TQ, w), lambda b, h, t, qi, kj: (b * nq + kj[t], kc + h)),
            pl.BlockSpec((A_TQ, w), lambda b, h, t, qi, kj: (b * nq + kj[t], vc + h)),
            pl.BlockSpec((4, A_HD), lambda b, h, t, qi, kj: (0, 0)),
            pl.BlockSpec((1, 2 * A_HD), lambda b, h, t, qi, kj: (0, 0)),
        ],
        out_specs=pl.BlockSpec((A_TQ, w), lambda b, h, t, qi, kj: (b * nq + qi[t], h)),
        scratch_shapes=[
            pltpu.VMEM((A_HPS, 2 * A_TQ, LANES), BF16),
            pltpu.VMEM((A_HPS, A_TQ, 2 * LANES), BF16),
            pltpu.VMEM((A_HPS, 2 * A_TQ, LANES), F32),
            pltpu.VMEM((A_HPS, 2 * A_TQ, 2 * LANES), F32),
        ],
    )
    return pl.pallas_call(
        _diff_attn_kernel,
        grid_spec=grid_spec,
        out_shape=jax.ShapeDtypeStruct((batch * seq, BRANCH_W), BF16),
        compiler_params=_cparams(("parallel", "parallel", "arbitrary")),
        name="diff_attn",
    )(qi, kj, slopes, lam_init, z, z, z, lvec, subln)


def _fox_cum_kernel(cf_ref, bf_ref, o_ref):
    x = cf_ref[...] + bf_ref[...]
    logf = jnp.minimum(x, 0.0) - jnp.log1p(jnp.exp(-jnp.abs(x)))
    rows, seq = logf.shape
    lane = lax.broadcasted_iota(jnp.int32, (rows, LANES), 1)
    carry = jnp.zeros((rows, 1), F32)
    for c in range(seq // LANES):
        y = logf[:, c * LANES:(c + 1) * LANES]
        sh = 1
        while sh < LANES:
            y = y + jnp.where(lane >= sh, pltpu.roll(y, sh, axis=1), 0.0)
            sh *= 2
        y = y + carry
        o_ref[:, c * LANES:(c + 1) * LANES] = y
        carry = y[:, LANES - 1:LANES]


def _fox_cum(cf_t, bf_t):
    rows, seq = cf_t.shape
    return pl.pallas_call(
        _fox_cum_kernel,
        grid=(1,),
        in_specs=[pl.BlockSpec((rows, seq), lambda i: (0, 0)), pl.BlockSpec((rows, 1), lambda i: (0, 0))],
        out_specs=pl.BlockSpec((rows, seq), lambda i: (0, 0)),
        out_shape=jax.ShapeDtypeStruct((rows, seq), F32),
        compiler_params=_cparams(("arbitrary",)),
        name="fox_cum",
    )(cf_t, bf_t)


def _fox_attn_kernel(qi_ref, kj_ref, q_ref, k_ref, v_ref, cum_ref, o_ref, v2_sc, m_sc, acc_sc):
    tq = C_TQ
    t = pl.program_id(2)
    i = qi_ref[t]
    j = kj_ref[t]
    heads = [(hh, slice(hh * LANES, (hh + 1) * LANES)) for hh in range(C_HPS)]

    @pl.when(j == 0)
    def _():
        for hh, _ in heads:
            _flash_init(v2_sc.at[hh], m_sc.at[hh], acc_sc.at[hh])

    def block(diag):
        for hh, cols in heads:
            v2_sc[hh, :, 0:LANES] = v_ref[:, cols]
            _flash_block(q_ref.at[:, cols], k_ref.at[:, cols], v2_sc.at[hh], cum_ref[hh] * (-LOG2E), m_sc.at[hh],
                         acc_sc.at[hh], tq, tq, diag, C_HD ** -0.5 * LOG2E)

    @pl.when(j < i)
    def _():
        block(False)

    @pl.when(j == i)
    def _():
        block(True)
        for hh, cols in heads:
            acc = acc_sc[hh]
            o_ref[:, cols] = (acc[:, 0:LANES] / acc[:, LANES:2 * LANES]).astype(BF16)


def _fox_attn(z, cum, batch, seq):
    nq = seq // C_TQ
    qi, kj = _tri_tables(nq)
    w = LANES * C_HPS
    qc, kc, vc = CQ0 // w, CK0 // w, CV0 // w
    hg = C_HEADS // C_HPS
    grid_spec = pltpu.PrefetchScalarGridSpec(
        num_scalar_prefetch=2,
        grid=(batch, hg, int(qi.shape[0])),
        in_specs=[
            pl.BlockSpec((C_TQ, w), lambda b, h, t, qi, kj: (b * nq + qi[t], qc + h)),
            pl.BlockSpec((C_TQ, w), lambda b, h, t, qi, kj: (b * nq + kj[t], kc + h)),
            pl.BlockSpec((C_TQ, w), lambda b, h, t, qi, kj: (b * nq + kj[t], vc + h)),
            pl.BlockSpec((C_HPS, 1, C_TQ), lambda b, h, t, qi, kj: (b * hg + h, 0, kj[t])),
        ],
        out_specs=pl.BlockSpec((C_TQ, w), lambda b, h, t, qi, kj: (b * nq + qi[t], h)),
        scratch_shapes=[
            pltpu.VMEM((C_HPS, C_TQ, 2 * LANES), BF16),
            pltpu.VMEM((C_HPS, C_TQ, LANES), F32),
            pltpu.VMEM((C_HPS, C_TQ, 2 * LANES), F32),
        ],
    )
    return pl.pallas_call(
        _fox_attn_kernel,
        grid_spec=grid_spec,
        out_shape=jax.ShapeDtypeStruct((batch * seq, BRANCH_W), BF16),
        compiler_params=_cparams(("parallel", "parallel", "arbitrary")),
        name="fox_attn",
    )(qi, kj, z, z, z, cum)


def _swa_kernel(sink_ref, q_ref, kc_ref, kp_ref, vc_ref, vp_ref, bias_ref, o_ref):
    blk = SWA_BLOCK
    first = pl.program_id(1) == 0
    swap = lambda x: jnp.concatenate([x[:, D_HD:2 * D_HD], x[:, 0:D_HD]], axis=1)
    ones = jnp.ones((blk * (SWA_QB + 1), LANES), BF16)
    kk_all = jnp.concatenate([kp_ref[...], kc_ref[...]], axis=0)
    vv_all = jnp.concatenate([vp_ref[...], vc_ref[...]], axis=0)
    kk_var = (kk_all, swap(kk_all))
    vv_var = (jnp.concatenate([vv_all, ones], axis=1), jnp.concatenate([swap(vv_all), ones], axis=1))
    q_all = q_ref[...] * jnp.asarray(D_HD ** -0.5, BF16)
    q_lane = lax.broadcasted_iota(jnp.int32, (2 * blk, LANES), 1)
    o_lane = lax.broadcasted_iota(jnp.int32, (blk, LANES), 1)
    for qb in range(SWA_QB):
        keys = slice(qb * blk, (qb + 2) * blk)
        for kv in range(D_KV):
            res = []
            for half in range(2):
                q2 = jnp.concatenate([q_all[qb * blk:(qb + 1) * blk, (2 * kv + tt) * LANES:(2 * kv + tt + 1) * LANES]
                                      for tt in range(2)], axis=0)
                q2 = jnp.where((q_lane < D_HD) if half == 0 else (q_lane >= D_HD), q2, jnp.zeros_like(q2))
                var = half ^ kv
                s = lax.dot_general(q2, kk_var[var][keys], (((1,), (1,)), ((), ())), preferred_element_type=F32)
                bias = bias_ref[1, kv, half]
                if qb == 0:
                    bias = jnp.where(first, bias_ref[0, kv, half], bias)
                lg = s + bias
                es, sinks = [], []
                for tt in range(2):
                    sink = sink_ref[4 * kv + half + 2 * tt]
                    lg_h = lg[tt * blk:(tt + 1) * blk]
                    m = jnp.maximum(jnp.max(lg_h, axis=-1, keepdims=True), sink)
                    es.append(jnp.exp(lg_h - m))
                    sinks.append(jnp.exp(sink - m))
                pv = jnp.dot(jnp.concatenate(es, axis=0).astype(BF16), vv_var[var][keys],
                             preferred_element_type=F32)
                res.append([pv[tt * blk:(tt + 1) * blk, 0:LANES] / (pv[tt * blk:(tt + 1) * blk, LANES:2 * LANES] + sinks[tt])
                            for tt in range(2)])
            for tt in range(2):
                tile = jnp.where(o_lane < D_HD, res[0][tt], res[1][tt])
                o_ref[qb * blk:(qb + 1) * blk, (2 * kv + tt) * LANES:(2 * kv + tt + 1) * LANES] = tile.astype(BF16)


def _swa_bias():
    slopes = np.exp2(-8.0 * np.arange(1, D_HEADS + 1, dtype=np.float32) / D_HEADS).astype(np.float32)
    qi = np.arange(SWA_BLOCK)
    kj = np.arange(2 * SWA_BLOCK) - SWA_BLOCK
    dist = (qi[:, None] - kj[None, :]).astype(np.float32)
    valid = (dist >= 0) & (dist < WINDOW)
    general = np.where(valid[None], -(slopes[:, None, None] * dist[None]), np.float32(NEG_INF))
    first = np.where((kj >= 0)[None, None, :], general, np.float32(NEG_INF))
    per_head = np.stack([first, general]).astype(np.float32)
    table = np.stack([np.stack([np.concatenate([per_head[:, 4 * kv + half], per_head[:, 4 * kv + half + 2]], axis=1)
                                for half in range(2)], axis=1) for kv in range(D_KV)], axis=1)
    return jnp.asarray(table)


def _swa(z, sinks, batch, seq):
    rows = SWA_QB * SWA_BLOCK
    ns = seq // rows
    qc, kc, vc = DQ0 // BRANCH_W, DK0 // LANES, DV0 // LANES
    cur = lambda b, n: b * ns + n
    prev = lambda b, n: jnp.maximum((b * ns + n) * SWA_QB - 1, 0)
    return pl.pallas_call(
        _swa_kernel,
        grid=(batch, ns),
        in_specs=[
            pl.BlockSpec(memory_space=pltpu.SMEM),
            pl.BlockSpec((rows, BRANCH_W), lambda b, n: (cur(b, n), qc)),
            pl.BlockSpec((rows, LANES), lambda b, n: (cur(b, n), kc)),
            pl.BlockSpec((SWA_BLOCK, LANES), lambda b, n: (prev(b, n), kc)),
            pl.BlockSpec((rows, LANES), lambda b, n: (cur(b, n), vc)),
            pl.BlockSpec((SWA_BLOCK, LANES), lambda b, n: (prev(b, n), vc)),
            pl.BlockSpec((2, D_KV, 2, 2 * SWA_BLOCK, 2 * SWA_BLOCK), lambda b, n: (0, 0, 0, 0, 0)),
        ],
        out_specs=pl.BlockSpec((rows, BRANCH_W), lambda b, n: (cur(b, n), 0)),
        out_shape=jax.ShapeDtypeStruct((batch * seq, BRANCH_W), BF16),
        compiler_params=_cparams(("parallel", "arbitrary")),
        name="swa",
    )(sinks, z, z, z, z, z, _swa_bias())


def _lru_kernel(x_ref, g_ref, cw_ref, cb_ref, wa_ref, ba_ref, wx_ref, bx_ref, lam_ref, o_ref,
                xpad_sc, apad_sc, upad_sc, hc_sc):
    ts, pad = LRU_TS, LRU_PAD

    @pl.when(pl.program_id(1) == 0)
    def _():
        xpad_sc[0:8, :] = jnp.zeros((8, BRANCH_W), F32)
        hc_sc[...] = jnp.zeros(hc_sc.shape, F32)
        apad_sc[0:pad, :] = jnp.ones((pad, BRANCH_W), F32)
        upad_sc[0:pad, :] = jnp.zeros((pad, BRANCH_W), F32)

    x = x_ref[...].astype(F32)
    xpad_sc[8:8 + ts, :] = x
    cw = cw_ref[...]
    xc = (cw[3:4] * x + cw[2:3] * xpad_sc[7:7 + ts, :] + cw[1:2] * xpad_sc[6:6 + ts, :]
          + cw[0:1] * xpad_sc[5:5 + ts, :] + cb_ref[...])
    xpad_sc[0:8, :] = x[ts - 8:ts]

    xcb = xc.astype(BF16)
    r = jax.nn.sigmoid(jnp.dot(xcb, wa_ref[...], preferred_element_type=F32) + ba_ref[...])
    gi = jax.nn.sigmoid(jnp.dot(xcb, wx_ref[...], preferred_element_type=F32) + bx_ref[...])
    log_a = (-B_C * r) * _softplus(-lam_ref[...])
    a = jnp.exp(log_a)
    u = jnp.sqrt(1.0 - a * a) * (gi * xc)

    sh = 1
    while sh < ts:
        apad_sc[pad:pad + ts, :] = a
        upad_sc[pad:pad + ts, :] = u
        u = a * upad_sc[pad - sh:pad - sh + ts, :] + u
        a = a * apad_sc[pad - sh:pad - sh + ts, :]
        sh *= 2
    h = a * hc_sc[...] + u
    hc_sc[...] = h[ts - 1:ts]
    g = g_ref[...].astype(F32)
    gelu = g * jax.nn.sigmoid((2.0 * math.sqrt(2.0 / math.pi)) * (g + 0.044715 * (g * g * g)))
    o_ref[...] = (gelu * h).astype(BF16)


def _lru(z, cw, cb, wa, ba, wx, bx, lam, l, batch, seq):
    ns = seq // LRU_TS
    xc_, gc_ = BX0 // BRANCH_W, BG0 // BRANCH_W
    const = lambda shape: pl.BlockSpec(shape, lambda b, s: (0, 0))
    return pl.pallas_call(
        _lru_kernel,
        grid=(batch, ns),
        in_specs=[
            pl.BlockSpec((LRU_TS, BRANCH_W), lambda b, s: (b * ns + s, xc_)),
            pl.BlockSpec((LRU_TS, BRANCH_W), lambda b, s: (b * ns + s, gc_)),
            const((B_CONV, BRANCH_W)), const((1, BRANCH_W)),
            pl.BlockSpec((None, BRANCH_W, BRANCH_W), lambda b, s: (l, 0, 0)), const((1, BRANCH_W)),
            pl.BlockSpec((None, BRANCH_W, BRANCH_W), lambda b, s: (l, 0, 0)), const((1, BRANCH_W)),
            const((1, BRANCH_W)),
        ],
        out_specs=pl.BlockSpec((LRU_TS, BRANCH_W), lambda b, s: (b * ns + s, 0)),
        out_shape=jax.ShapeDtypeStruct((batch * seq, BRANCH_W), BF16),
        scratch_shapes=[
            pltpu.VMEM((LRU_TS + 8, BRANCH_W), F32),
            pltpu.VMEM((LRU_TS + LRU_PAD, BRANCH_W), F32),
            pltpu.VMEM((LRU_TS + LRU_PAD, BRANCH_W), F32),
            pltpu.VMEM((1, BRANCH_W), F32),
        ],
        compiler_params=_cparams(("parallel", "arbitrary")),
        name="lru",
    )(z, z, cw, cb, wa, ba, wx, bx, lam)


def _merge_kernel(h_ref, ya_ref, yb_ref, yc_ref, yd_ref, zg_ref, bg_ref, wb_ref, wo_ref, gn_ref, o_ref, v_ref):
    ys = (ya_ref, yb_ref, yc_ref, yd_ref)
    mixed = None
    for n in range(N_BRANCH):
        proj = jnp.dot(ys[n][...], wb_ref[n], preferred_element_type=F32)
        gate = zg_ref[:, n * D_MODEL:(n + 1) * D_MODEL].astype(F32) + bg_ref[n:n + 1, :]
        term = jax.nn.sigmoid(gate) * proj
        mixed = term if mixed is None else mixed + term
    o = h_ref[...] + jnp.dot(mixed.astype(BF16), wo_ref[...], preferred_element_type=F32)
    o_ref[...] = o
    v_ref[...] = _rms(o, gn_ref[...]).astype(BF16)


def _merge(h, ya, yb, yc, yd, zg, bg, wb, wo, gn):
    t = h.shape[0]
    row = lambda shape: pl.BlockSpec(shape, lambda i: (i, 0))
    return pl.pallas_call(
        _merge_kernel,
        grid=(t // MERGE_TM,),
        in_specs=[
            row((MERGE_TM, D_MODEL)),
            row((MERGE_TM, BRANCH_W)), row((MERGE_TM, BRANCH_W)), row((MERGE_TM, BRANCH_W)),
            row((MERGE_TM, BRANCH_W)),
            row((MERGE_TM, ZG_W)),
            pl.BlockSpec((N_BRANCH, D_MODEL), lambda i: (0, 0)),
            pl.BlockSpec((N_BRANCH, BRANCH_W, D_MODEL), lambda i: (0, 0, 0), pipeline_mode=pl.Buffered(1)),
            pl.BlockSpec((D_MODEL, D_MODEL), lambda i: (0, 0), pipeline_mode=pl.Buffered(1)),
            pl.BlockSpec((1, D_MODEL), lambda i: (0, 0)),
        ],
        out_specs=[row((MERGE_TM, D_MODEL)), row((MERGE_TM, D_MODEL))],
        out_shape=[jax.ShapeDtypeStruct((t, D_MODEL), F32), jax.ShapeDtypeStruct((t, D_MODEL), BF16)],
        compiler_params=_cparams(("parallel",)),
        name="merge",
    )(h, ya, yb, yc, yd, zg, bg, wb, wo, gn)


def _ffn_kernel(v_ref, h_ref, wg_ref, wu_ref, wd_ref, gn_ref, o_ref, *u_ref, final_norm):
    j = pl.program_id(1)

    @pl.when(j == 0)
    def _():
        o_ref[...] = h_ref[...]

    v = v_ref[...]
    a = jnp.dot(v, wg_ref[...], preferred_element_type=F32)
    b = jnp.dot(v, wu_ref[...], preferred_element_type=F32)
    t = (a * jax.nn.sigmoid(a)) * b
    o_ref[...] += jnp.dot(t.astype(BF16), wd_ref[...], preferred_element_type=F32)

    @pl.when(j == pl.num_programs(1) - 1)
    def _():
        y = _rms(o_ref[...], gn_ref[...])
        if final_norm:
            o_ref[...] = y
        else:
            u_ref[0][...] = y.astype(BF16)


def _ffn(v, h, wg, wu, wd, gn, final_norm):
    t = h.shape[0]
    row = lambda: pl.BlockSpec((FFN_TM, D_MODEL), lambda i, j: (i, 0))
    out_specs = [row()] if final_norm else [row(), row()]
    out_shape = [jax.ShapeDtypeStruct((t, D_MODEL), F32)]
    if not final_norm:
        out_shape.append(jax.ShapeDtypeStruct((t, D_MODEL), BF16))
    return pl.pallas_call(
        functools.partial(_ffn_kernel, final_norm=final_norm),
        grid=(t // FFN_TM, D_FF // FFN_TF),
        in_specs=[
            row(), row(),
            pl.BlockSpec((D_MODEL, FFN_TF), lambda i, j: (0, j)),
            pl.BlockSpec((D_MODEL, FFN_TF), lambda i, j: (0, j)),
            pl.BlockSpec((FFN_TF, D_MODEL), lambda i, j: (j, 0)),
            pl.BlockSpec((1, D_MODEL), lambda i, j: (0, 0)),
        ],
        out_specs=out_specs,
        out_shape=out_shape,
        compiler_params=_cparams(("parallel", "arbitrary")),
        name="ffn_final" if final_norm else "ffn",
    )(v, h, wg, wu, wd, gn)


def _block_diag(w):
    depth, nb, c, d = w.shape
    eye = jnp.eye(nb, dtype=w.dtype)
    return jnp.einsum('lncd,nm->lncmd', w, eye).reshape(depth, nb * c, nb * d)


def kernel(x, norm_mix, w_in, b_gate, diff_lq1, diff_lk1, diff_lq2, diff_lk2, diff_subln, lru_conv_w, lru_conv_b,
           lru_wa, lru_ba, lru_wx, lru_bx, lru_lambda, fox_b_f, swa_sinks, w_branch, w_out, norm_ffn, w_ffn_gate,
           w_ffn_up, w_ffn_down, norm_final):
    batch, seq, d = x.shape
    depth = w_in.shape[0]
    assert d == D_MODEL and seq % max(A_TQ, C_TQ, LRU_TS, SWA_QB * SWA_BLOCK) == 0 and (batch * seq) % IN_TM == 0
    t = batch * seq

    piece = lambda lo, hi: jnp.swapaxes(w_in[:, :, lo:hi], 1, 2).astype(BF16)
    wt_zm = piece(0, ZM_W)
    wt_gz = piece(_R_GZ0, _R_GZ0 + ZG_W)
    wt_d = piece(_R_DQ0, _R_GZ0)
    wt_cf = jnp.pad(piece(_R_CF0, _R_DQ0), ((0, 0), (0, CF_PAD - C_HEADS), (0, 0)))
    wa = _block_diag(lru_wa).astype(BF16)
    wx = _block_diag(lru_wx).astype(BF16)
    lvec = jnp.stack([diff_lq1, diff_lk1, diff_lq2, diff_lk2], axis=1).astype(F32)
    bg = b_gate.reshape(depth, N_BRANCH, D_MODEL)
    gf = norm_final.reshape(1, D_MODEL)

    h = x.reshape(t, D_MODEL)
    u = _norm(h, norm_mix[0].reshape(1, D_MODEL))
    for l in range(depth):
        lam_init = 0.8 - 0.6 * math.exp(-0.3 * l)
        zg, wg, wu, wd = _proj(u, wt_gz, l, "proj_gate", cast=(w_ffn_gate, w_ffn_up, w_ffn_down))
        zm, wb, wo = _proj(u, wt_zm, l, "proj_mix", cast=(w_branch.reshape(depth, N_BRANCH * BRANCH_W, D_MODEL), w_out))
        zd, cf = _proj_d(u, wt_d, wt_cf, l)

        ya = _diff_attn(zm, lvec[l], diff_subln[l].reshape(1, 2 * A_HD),
                        jnp.asarray([lam_init, 1.0 - lam_init], F32), batch, seq)
        yb = _lru(zm, lru_conv_w[l], lru_conv_b[l].reshape(1, -1), wa, lru_ba[l].reshape(1, -1), wx,
                  lru_bx[l].reshape(1, -1), lru_lambda[l].reshape(1, -1), l, batch, seq)
        cf_t = cf[:, :C_HEADS].reshape(batch, seq, C_HEADS).transpose(0, 2, 1).reshape(batch * C_HEADS, seq)
        bf_t = jnp.tile(fox_b_f[l], batch).reshape(batch * C_HEADS, 1)
        cum = _fox_cum(cf_t, bf_t).reshape(batch * C_HEADS, 1, seq)
        yc = _fox_attn(zm, cum, batch, seq)
        yd = _swa(zd, swa_sinks[l], batch, seq)

        h, v = _merge(h, ya, yb, yc, yd, zg, bg[l], wb.reshape(N_BRANCH, BRANCH_W, D_MODEL), wo,
                      norm_ffn[l].reshape(1, D_MODEL))
        if l == depth - 1:
            (h,) = _ffn(v, h, wg, wu, wd, gf, final_norm=True)
        else:
            h, u = _ffn(v, h, wg, wu, wd, norm_mix[l + 1].reshape(1, D_MODEL), final_norm=False)
    return h.reshape(batch, seq, D_MODEL)
```

```python
import functools
import math

import jax
import jax.numpy as jnp
import numpy as np
from jax import lax
from jax.experimental import pallas as pl
from jax.experimental.pallas import tpu as pltpu

F32 = jnp.float32
BF16 = jnp.bfloat16
LANES = 128
BF16_SUBLANES = 16

D_MODEL = 2048
N_BRANCH = 4
BRANCH_W = D_MODEL // 4
A_HEADS = 4
A_HD = BRANCH_W // (2 * A_HEADS)
B_BLOCKS = 8
B_CONV = 4
B_C = 8.0
C_HEADS = 4
C_HD = BRANCH_W // C_HEADS
D_HEADS = 8
D_KV = 2
D_GROUP = D_HEADS // D_KV
D_HD = BRANCH_W // D_HEADS
WINDOW = 128
SWA_BLOCK = 128
SWA_QB = 8
D_FF = -(-8 * D_MODEL // (3 * 256)) * 256
RMS_EPS = 1e-6
NEG_INF = -1e30

AQ0 = 0
AK0 = AQ0 + BRANCH_W
AV0 = AK0 + BRANCH_W
BX0 = AV0 + BRANCH_W
BG0 = BX0 + BRANCH_W
CQ0 = BG0 + BRANCH_W
CK0 = CQ0 + BRANCH_W
CV0 = CK0 + BRANCH_W
ZM_W = CV0 + BRANCH_W
DQ0 = 0
DK0 = DQ0 + BRANCH_W
DV0 = DK0 + D_KV * D_HD
ZD_W = DV0 + D_KV * D_HD
ZG_W = N_BRANCH * D_MODEL
CF_PAD = 128

_R_CF0 = ZM_W
_R_DQ0 = _R_CF0 + C_HEADS
_R_GZ0 = _R_DQ0 + ZD_W

V7X_VMEM_LIMIT = 56 * 1024 * 1024

NORM_TM = 1024
IN_TM, IN_TN = 2048, 1024
FLASH_ROWS = 256
FLASH_ROWS_OFFDIAG = 128
A_TQ, A_HPS = 1024, 4
C_TQ, C_HPS = 1024, 4
LOG2E = math.log2(math.e)
LRU_TS = 512
LRU_PAD = LRU_TS // 2
MERGE_TM = 256
FFN_TM, FFN_TF = 512, 512


def _cparams(sem):
    return pltpu.CompilerParams(dimension_semantics=sem, vmem_limit_bytes=V7X_VMEM_LIMIT)


def _rms(x, g):
    return x * lax.rsqrt(jnp.mean(x * x, axis=-1, keepdims=True) + RMS_EPS) * g


def _softplus(y):
    return jnp.maximum(y, 0.0) + jnp.log1p(jnp.exp(-jnp.abs(y)))


def _norm_kernel(x_ref, g_ref, o_ref):
    o_ref[...] = _rms(x_ref[...], g_ref[...]).astype(BF16)


def _norm(x, g):
    t = x.shape[0]
    return pl.pallas_call(
        _norm_kernel,
        grid=(t // NORM_TM,),
        in_specs=[pl.BlockSpec((NORM_TM, D_MODEL), lambda i: (i, 0)), pl.BlockSpec((1, D_MODEL), lambda i: (0, 0))],
        out_specs=pl.BlockSpec((NORM_TM, D_MODEL), lambda i: (i, 0)),
        out_shape=jax.ShapeDtypeStruct((t, D_MODEL), BF16),
        compiler_params=_cparams(("parallel",)),
        name="norm0",
    )(x, g)


_NT = (((1,), (1,)), ((), ()))


def _proj_kernel(u_ref, wt_ref, *rest):
    n = len(rest) // 2
    z_ref = rest[n]
    z_ref[...] = lax.dot_general(u_ref[...], wt_ref[...], _NT, preferred_element_type=F32).astype(BF16)
    for w_ref, o_ref in zip(rest[:n], rest[n + 1:]):
        o_ref[...] = w_ref[...].astype(BF16)


def _cast_rows(n_rows, steps):
    for n_blocks in range(steps, 0, -1):
        if n_rows % n_blocks == 0 and (n_rows // n_blocks) % BF16_SUBLANES == 0:
            return n_rows // n_blocks, n_blocks
    raise ValueError("no bf16-tile-aligned row split")


def _proj(u, wt, l, name, cast=()):
    t = u.shape[0]
    n_cols = wt.shape[1]
    n_i, n_j = t // IN_TM, n_cols // IN_TN
    in_specs = [
        pl.BlockSpec((IN_TM, D_MODEL), lambda i, j: (i, 0)),
        pl.BlockSpec((None, IN_TN, D_MODEL), lambda i, j: (l, j, 0)),
    ]
    out_specs = [pl.BlockSpec((IN_TM, IN_TN), lambda i, j: (i, j))]
    out_shape = [jax.ShapeDtypeStruct((t, n_cols), BF16)]
    for w in cast:
        _, n_rows, n_c = w.shape
        rows, n_blocks = _cast_rows(n_rows, n_i * n_j)
        blk = lambda i, j, n_blocks=n_blocks: jnp.minimum(i * n_j + j, n_blocks - 1)
        in_specs.append(pl.BlockSpec((None, rows, n_c), lambda i, j, blk=blk: (l, blk(i, j), 0)))
        out_specs.append(pl.BlockSpec((rows, n_c), lambda i, j, blk=blk: (blk(i, j), 0)))
        out_shape.append(jax.ShapeDtypeStruct((n_rows, n_c), BF16))
    out = pl.pallas_call(
        _proj_kernel,
        grid=(n_i, n_j),
        in_specs=in_specs,
        out_specs=out_specs,
        out_shape=out_shape,
        compiler_params=_cparams(("arbitrary", "arbitrary")),
        name=name,
    )(u, wt, *cast)
    return out if cast else out[0]


def _proj_d_kernel(u_ref, wt_ref, wcft_ref, z_ref, cf_ref):
    u = u_ref[...]
    z_ref[...] = lax.dot_general(u, wt_ref[...], _NT, preferred_element_type=F32).astype(BF16)
    cf_ref[...] = lax.dot_general(u, wcft_ref[...], _NT, preferred_element_type=F32)


def _proj_d(u, wt, wcft, l):
    t = u.shape[0]
    return pl.pallas_call(
        _proj_d_kernel,
        grid=(t // IN_TM,),
        in_specs=[
            pl.BlockSpec((IN_TM, D_MODEL), lambda i: (i, 0)),
            pl.BlockSpec((None, ZD_W, D_MODEL), lambda i: (l, 0, 0)),
            pl.BlockSpec((None, CF_PAD, D_MODEL), lambda i: (l, 0, 0)),
        ],
        out_specs=[
            pl.BlockSpec((IN_TM, ZD_W), lambda i: (i, 0)),
            pl.BlockSpec((IN_TM, CF_PAD), lambda i: (i, 0)),
        ],
        out_shape=[jax.ShapeDtypeStruct((t, ZD_W), BF16), jax.ShapeDtypeStruct((t, CF_PAD), F32)],
        compiler_params=_cparams(("parallel",)),
        name="proj_d",
    )(u, wt, wcft)


def _tri_tables(n):
    qi = [i for i in range(n) for _ in range(i + 1)]
    kj = [j for i in range(n) for j in range(i + 1)]
    return jnp.asarray(qi, jnp.int32), jnp.asarray(kj, jnp.int32)


def _flash_rows(q_ref, k_ref, v2_sc, bias, m_sc, acc_sc, r0, rows, diag_q0, scale):
    nk = k_ref.shape[0] if diag_q0 is None else diag_q0 + rows
    s = lax.dot_general(q_ref[r0:r0 + rows, :], k_ref[0:nk, :], (((1,), (1,)), ((), ())),
                        preferred_element_type=F32)
    s = s * scale + bias[:, 0:nk]
    if diag_q0 is not None:
        qpos = lax.broadcasted_iota(jnp.int32, s.shape, 0) + diag_q0
        kpos = lax.broadcasted_iota(jnp.int32, s.shape, 1)
        s = jnp.where(kpos <= qpos, s, NEG_INF)
    m_prev = m_sc[r0:r0 + rows, :]
    m_new = jnp.maximum(m_prev, jnp.max(s, axis=-1, keepdims=True))
    alpha = jnp.exp2(m_prev - m_new)
    p = jnp.exp2(s - jnp.concatenate([m_new] * (nk // LANES), axis=1))
    pv = jnp.dot(p.astype(BF16), v2_sc[0:nk, :], preferred_element_type=F32)
    acc_sc[r0:r0 + rows, :] = jnp.concatenate([alpha, alpha], axis=1) * acc_sc[r0:r0 + rows, :] + pv
    m_sc[r0:r0 + rows, :] = m_new


def _flash_block(q_ref, k_ref, v2_sc, bias, m_sc, acc_sc, n_rows, tq, diag, scale):
    rows = FLASH_ROWS if diag else FLASH_ROWS_OFFDIAG
    for r0 in range(0, n_rows, rows):
        _flash_rows(q_ref, k_ref, v2_sc, bias, m_sc, acc_sc, r0, rows, (r0 % tq) if diag else None, scale)


def _flash_init(v2_sc, m_sc, acc_sc):
    v2_sc[:, LANES:2 * LANES] = jnp.ones((v2_sc.shape[0], LANES), BF16)
    m_sc[...] = jnp.full(m_sc.shape, -jnp.inf, F32)
    acc_sc[...] = jnp.zeros(acc_sc.shape, F32)


def _diff_attn_kernel(qi_ref, kj_ref, slope_ref, lami_ref, q_ref, k_ref, v_ref, lv_ref, sg_ref, o_ref,
                      q2_sc, v2_sc, m_sc, acc_sc):
    tq = A_TQ
    t = pl.program_id(2)
    i = qi_ref[t]
    j = kj_ref[t]
    heads = [(hh, slice(hh * LANES, (hh + 1) * LANES)) for hh in range(A_HPS)]

    @pl.when(j == 0)
    def _():
        for hh, cols in heads:
            q = q_ref[:, cols]
            lane = lax.broadcasted_iota(jnp.int32, q.shape, 1)
            qs = q * jnp.asarray(A_HD ** -0.5, BF16)
            zero = jnp.zeros_like(qs)
            q2_sc[hh, 0:tq, :] = jnp.where(lane < A_HD, qs, zero)
            q2_sc[hh, tq:2 * tq, :] = jnp.where(lane >= A_HD, qs, zero)
            _flash_init(v2_sc.at[hh], m_sc.at[hh], acc_sc.at[hh])

    kpos = (lax.broadcasted_iota(jnp.int32, (1, tq), 1) + (j - i) * tq).astype(F32)

    def block(diag):
        for hh, cols in heads:
            v2_sc[hh, :, 0:LANES] = v_ref[:, cols]
            bias = kpos * (slope_ref[pl.program_id(1) * A_HPS + hh] * LOG2E)
            _flash_block(q2_sc.at[hh], k_ref.at[:, cols], v2_sc.at[hh], bias, m_sc.at[hh], acc_sc.at[hh],
                         2 * tq, tq, diag, LOG2E)

    @pl.when(j < i)
    def _():
        block(False)

    @pl.when(j == i)
    def _():
        block(True)
        lv = lv_ref[...]
        lam = (jnp.exp(jnp.sum(lv[0:1] * lv[1:2], axis=-1, keepdims=True))
               - jnp.exp(jnp.sum(lv[2:3] * lv[3:4], axis=-1, keepdims=True)) + lami_ref[0])
        for hh, cols in heads:
            acc = acc_sc[hh]
            o = acc[0:tq, 0:LANES] / acc[0:tq, LANES:2 * LANES] - lam * (acc[tq:2 * tq, 0:LANES] / acc[tq:2 * tq, LANES:2 * LANES])
            o_ref[:, cols] = (_rms(o, sg_ref[...]) * lami_ref[1]).astype(BF16)


def _diff_attn(z, lvec, subln, lam_init, batch, seq):
    nq = seq // A_TQ
    qi, kj = _tri_tables(nq)
    slopes = jnp.asarray(np.exp2(-8.0 * np.arange(1, A_HEADS + 1, dtype=np.float32) / A_HEADS), F32)
    w = LANES * A_HPS
    qc, kc, vc = AQ0 // w, AK0 // w, AV0 // w
    smem = pl.BlockSpec(memory_space=pltpu.SMEM)
    grid_spec = pltpu.PrefetchScalarGridSpec(
        num_scalar_prefetch=2,
        grid=(batch, A_HEADS // A_HPS, int(qi.shape[0])),
        in_specs=[
            smem, smem,
            pl.BlockSpec((A_TQ, w), lambda b, h, t, qi, kj: (b * nq + qi[t], qc + h)),
            pl.BlockSpec((A_TQ, w), lambda b, h, t, qi, kj: (b * nq + kj[t], kc + h)),
            pl.BlockSpec((A_TQ, w), lambda b, h, t, qi, kj: (b * nq + kj[t], vc + h)),
            pl.BlockSpec((4, A_HD), lambda b, h, t, qi, kj: (0, 0)),
            pl.BlockSpec((1, 2 * A_HD), lambda b, h, t, qi, kj: (0, 0)),
        ],
        out_specs=pl.BlockSpec((A_TQ, w), lambda b, h, t, qi, kj: (b * nq + qi[t], h)),
        scratch_shapes=[
            pltpu.VMEM((A_HPS, 2 * A_TQ, LANES), BF16),
            pltpu.VMEM((A_HPS, A_TQ, 2 * LANES), BF16),
            pltpu.VMEM((A_HPS, 2 * A_TQ, LANES), F32),
            pltpu.VMEM((A_HPS, 2 * A_TQ, 2 * LANES), F32),
        ],
    )
    return pl.pallas_call(
        _diff_attn_kernel,
        grid_spec=grid_spec,
        out_shape=jax.ShapeDtypeStruct((batch * seq, BRANCH_W), BF16),
        compiler_params=_cparams(("parallel", "parallel", "arbitrary")),
        name="diff_attn",
    )(qi, kj, slopes, lam_init, z, z, z, lvec, subln)


def _fox_cum_kernel(cf_ref, bf_ref, o_ref):
    x = cf_ref[...] + bf_ref[...]
    logf = jnp.minimum(x, 0.0) - jnp.log1p(jnp.exp(-jnp.abs(x)))
    rows, seq = logf.shape
    lane = lax.broadcasted_iota(jnp.int32, (rows, LANES), 1)
    carry = jnp.zeros((rows, 1), F32)
    for c in range(seq // LANES):
        y = logf[:, c * LANES:(c + 1) * LANES]
        sh = 1
        while sh < LANES:
            y = y + jnp.where(lane >= sh, pltpu.roll(y, sh, axis=1), 0.0)
            sh *= 2
        y = y + carry
        o_ref[:, c * LANES:(c + 1) * LANES] = y
        carry = y[:, LANES - 1:LANES]


def _fox_cum(cf_t, bf_t):
    rows, seq = cf_t.shape
    return pl.pallas_call(
        _fox_cum_kernel,
        grid=(1,),
        in_specs=[pl.BlockSpec((rows, seq), lambda i: (0, 0)), pl.BlockSpec((rows, 1), lambda i: (0, 0))],
        out_specs=pl.BlockSpec((rows, seq), lambda i: (0, 0)),
        out_shape=jax.ShapeDtypeStruct((rows, seq), F32),
        compiler_params=_cparams(("arbitrary",)),
        name="fox_cum",
    )(cf_t, bf_t)


def _fox_attn_kernel(qi_ref, kj_ref, q_ref, k_ref, v_ref, cum_ref, o_ref, v2_sc, m_sc, acc_sc):
    tq = C_TQ
    t = pl.program_id(2)
    i = qi_ref[t]
    j = kj_ref[t]
    heads = [(hh, slice(hh * LANES, (hh + 1) * LANES)) for hh in range(C_HPS)]

    @pl.when(j == 0)
    def _():
        for hh, _ in heads:
            _flash_init(v2_sc.at[hh], m_sc.at[hh], acc_sc.at[hh])

    def block(diag):
        for hh, cols in heads:
            v2_sc[hh, :, 0:LANES] = v_ref[:, cols]
            _flash_block(q_ref.at[:, cols], k_ref.at[:, cols], v2_sc.at[hh], cum_ref[hh] * (-LOG2E), m_sc.at[hh],
                         acc_sc.at[hh], tq, tq, diag, C_HD ** -0.5 * LOG2E)

    @pl.when(j < i)
    def _():
        block(False)

    @pl.when(j == i)
    def _():
        block(True)
        for hh, cols in heads:
            acc = acc_sc[hh]
            o_ref[:, cols] = (acc[:, 0:LANES] / acc[:, LANES:2 * LANES]).astype(BF16)


def _fox_attn(z, cum, batch, seq):
    nq = seq // C_TQ
    qi, kj = _tri_tables(nq)
    w = LANES * C_HPS
    qc, kc, vc = CQ0 // w, CK0 // w, CV0 // w
    hg = C_HEADS // C_HPS
    grid_spec = pltpu.PrefetchScalarGridSpec(
        num_scalar_prefetch=2,
        grid=(batch, hg, int(qi.shape[0])),
        in_specs=[
            pl.BlockSpec((C_TQ, w), lambda b, h, t, qi, kj: (b * nq + qi[t], qc + h)),
            pl.BlockSpec((C_TQ, w), lambda b, h, t, qi, kj: (b * nq + kj[t], kc + h)),
            pl.BlockSpec((C_TQ, w), lambda b, h, t, qi, kj: (b * nq + kj[t], vc + h)),
            pl.BlockSpec((C_HPS, 1, C_TQ), lambda b, h, t, qi, kj: (b * hg + h, 0, kj[t])),
        ],
        out_specs=pl.BlockSpec((C_TQ, w), lambda b, h, t, qi, kj: (b * nq + qi[t], h)),
        scratch_shapes=[
            pltpu.VMEM((C_HPS, C_TQ, 2 * LANES), BF16),
            pltpu.VMEM((C_HPS, C_TQ, LANES), F32),
            pltpu.VMEM((C_HPS, C_TQ, 2 * LANES), F32),
        ],
    )
    return pl.pallas_call(
        _fox_attn_kernel,
        grid_spec=grid_spec,
        out_shape=jax.ShapeDtypeStruct((batch * seq, BRANCH_W), BF16),
        compiler_params=_cparams(("parallel", "parallel", "arbitrary")),
        name="fox_attn",
    )(qi, kj, z, z, z, cum)


def _swa_kernel(sink_ref, q_ref, kc_ref, kp_ref, vc_ref, vp_ref, bias_ref, o_ref):
    blk = SWA_BLOCK
    first = pl.program_id(1) == 0
    swap = lambda x: jnp.concatenate([x[:, D_HD:2 * D_HD], x[:, 0:D_HD]], axis=1)
    ones = jnp.ones((blk * (SWA_QB + 1), LANES), BF16)
    kk_all = jnp.concatenate([kp_ref[...], kc_ref[...]], axis=0)
    vv_all = jnp.concatenate([vp_ref[...], vc_ref[...]], axis=0)
    kk_var = (kk_all, swap(kk_all))
    vv_var = (jnp.concatenate([vv_all, ones], axis=1), jnp.concatenate([swap(vv_all), ones], axis=1))
    q_all = q_ref[...] * jnp.asarray(D_HD ** -0.5, BF16)
    q_lane = lax.broadcasted_iota(jnp.int32, (2 * blk, LANES), 1)
    o_lane = lax.broadcasted_iota(jnp.int32, (blk, LANES), 1)
    for qb in range(SWA_QB):
        keys = slice(qb * blk, (qb + 2) * blk)
        for kv in range(D_KV):
            res = []
            for half in range(2):
                q2 = jnp.concatenate([q_all[qb * blk:(qb + 1) * blk, (2 * kv + tt) * LANES:(2 * kv + tt + 1) * LANES]
                                      for tt in range(2)], axis=0)
                q2 = jnp.where((q_lane < D_HD) if half == 0 else (q_lane >= D_HD), q2, jnp.zeros_like(q2))
                var = half ^ kv
                s = lax.dot_general(q2, kk_var[var][keys], (((1,), (1,)), ((), ())), preferred_element_type=F32)
                bias = bias_ref[1, kv, half]
                if qb == 0:
                    bias = jnp.where(first, bias_ref[0, kv, half], bias)
                lg = s + bias
                es, sinks = [], []
                for tt in range(2):
                    sink = sink_ref[4 * kv + half + 2 * tt]
                    lg_h = lg[tt * blk:(tt + 1) * blk]
                    m = jnp.maximum(jnp.max(lg_h, axis=-1, keepdims=True), sink)
                    es.append(jnp.exp(lg_h - m))
                    sinks.append(jnp.exp(sink - m))
                pv = jnp.dot(jnp.concatenate(es, axis=0).astype(BF16), vv_var[var][keys],
                             preferred_element_type=F32)
                res.append([pv[tt * blk:(tt + 1) * blk, 0:LANES] / (pv[tt * blk:(tt + 1) * blk, LANES:2 * LANES] + sinks[tt])
                            for tt in range(2)])
            for tt in range(2):
                tile = jnp.where(o_lane < D_HD, res[0][tt], res[1][tt])
                o_ref[qb * blk:(qb + 1) * blk, (2 * kv + tt) * LANES:(2 * kv + tt + 1) * LANES] = tile.astype(BF16)


def _swa_bias():
    slopes = np.exp2(-8.0 * np.arange(1, D_HEADS + 1, dtype=np.float32) / D_HEADS).astype(np.float32)
    qi = np.arange(SWA_BLOCK)
    kj = np.arange(2 * SWA_BLOCK) - SWA_BLOCK
    dist = (qi[:, None] - kj[None, :]).astype(np.float32)
    valid = (dist >= 0) & (dist < WINDOW)
    general = np.where(valid[None], -(slopes[:, None, None] * dist[None]), np.float32(NEG_INF))
    first = np.where((kj >= 0)[None, None, :], general, np.float32(NEG_INF))
    per_head = np.stack([first, general]).astype(np.float32)
    table = np.stack([np.stack([np.concatenate([per_head[:, 4 * kv + half], per_head[:, 4 * kv + half + 2]], axis=1)
                                for half in range(2)], axis=1) for kv in range(D_KV)], axis=1)
    return jnp.asarray(table)


def _swa(z, sinks, batch, seq):
    rows = SWA_QB * SWA_BLOCK
    ns = seq // rows
    qc, kc, vc = DQ0 // BRANCH_W, DK0 // LANES, DV0 // LANES
    cur = lambda b, n: b * ns + n
    prev = lambda b, n: jnp.maximum((b * ns + n) * SWA_QB - 1, 0)
    return pl.pallas_call(
        _swa_kernel,
        grid=(batch, ns),
        in_specs=[
            pl.BlockSpec(memory_space=pltpu.SMEM),
            pl.BlockSpec((rows, BRANCH_W), lambda b, n: (cur(b, n), qc)),
            pl.BlockSpec((rows, LANES), lambda b, n: (cur(b, n), kc)),
            pl.BlockSpec((SWA_BLOCK, LANES), lambda b, n: (prev(b, n), kc)),
            pl.BlockSpec((rows, LANES), lambda b, n: (cur(b, n), vc)),
            pl.BlockSpec((SWA_BLOCK, LANES), lambda b, n: (prev(b, n), vc)),
            pl.BlockSpec((2, D_KV, 2, 2 * SWA_BLOCK, 2 * SWA_BLOCK), lambda b, n: (0, 0, 0, 0, 0)),
        ],
        out_specs=pl.BlockSpec((rows, BRANCH_W), lambda b, n: (cur(b, n), 0)),
        out_shape=jax.ShapeDtypeStruct((batch * seq, BRANCH_W), BF16),
        compiler_params=_cparams(("parallel", "arbitrary")),
        name="swa",
    )(sinks, z, z, z, z, z, _swa_bias())


def _lru_kernel(x_ref, g_ref, cw_ref, cb_ref, wa_ref, ba_ref, wx_ref, bx_ref, lam_ref, o_ref,
                xpad_sc, apad_sc, upad_sc, hc_sc):
    ts, pad = LRU_TS, LRU_PAD

    @pl.when(pl.program_id(1) == 0)
    def _():
        xpad_sc[0:8, :] = jnp.zeros((8, BRANCH_W), F32)
        hc_sc[...] = jnp.zeros(hc_sc.shape, F32)
        apad_sc[0:pad, :] = jnp.ones((pad, BRANCH_W), F32)
        upad_sc[0:pad, :] = jnp.zeros((pad, BRANCH_W), F32)

    x = x_ref[...].astype(F32)
    xpad_sc[8:8 + ts, :] = x
    cw = cw_ref[...]
    xc = (cw[3:4] * x + cw[2:3] * xpad_sc[7:7 + ts, :] + cw[1:2] * xpad_sc[6:6 + ts, :]
          + cw[0:1] * xpad_sc[5:5 + ts, :] + cb_ref[...])
    xpad_sc[0:8, :] = x[ts - 8:ts]

    xcb = xc.astype(BF16)
    r = jax.nn.sigmoid(jnp.dot(xcb, wa_ref[...], preferred_element_type=F32) + ba_ref[...])
    gi = jax.nn.sigmoid(jnp.dot(xcb, wx_ref[...], preferred_element_type=F32) + bx_ref[...])
    log_a = (-B_C * r) * _softplus(-lam_ref[...])
    a = jnp.exp(log_a)
    u = jnp.sqrt(1.0 - a * a) * (gi * xc)

    sh = 1
    while sh < ts:
        apad_sc[pad:pad + ts, :] = a
        upad_sc[pad:pad + ts, :] = u
        u = a * upad_sc[pad - sh:pad - sh + ts, :] + u
        a = a * apad_sc[pad - sh:pad - sh + ts, :]
        sh *= 2
    h = a * hc_sc[...] + u
    hc_sc[...] = h[ts - 1:ts]
    g = g_ref[...].astype(F32)
    gelu = g * jax.nn.sigmoid((2.0 * math.sqrt(2.0 / math.pi)) * (g + 0.044715 * (g * g * g)))
    o_ref[...] = (gelu * h).astype(BF16)


def _lru(z, cw, cb, wa, ba, wx, bx, lam, l, batch, seq):
    ns = seq // LRU_TS
    xc_, gc_ = BX0 // BRANCH_W, BG0 // BRANCH_W
    const = lambda shape: pl.BlockSpec(shape, lambda b, s: (0, 0))
    return pl.pallas_call(
        _lru_kernel,
        grid=(batch, ns),
        in_specs=[
            pl.BlockSpec((LRU_TS, BRANCH_W), lambda b, s: (b * ns + s, xc_)),
            pl.BlockSpec((LRU_TS, BRANCH_W), lambda b, s: (b * ns + s, gc_)),
            const((B_CONV, BRANCH_W)), const((1, BRANCH_W)),
            pl.BlockSpec((None, BRANCH_W, BRANCH_W), lambda b, s: (l, 0, 0)), const((1, BRANCH_W)),
            pl.BlockSpec((None, BRANCH_W, BRANCH_W), lambda b, s: (l, 0, 0)), const((1, BRANCH_W)),
            const((1, BRANCH_W)),
        ],
        out_specs=pl.BlockSpec((LRU_TS, BRANCH_W), lambda b, s: (b * ns + s, 0)),
        out_shape=jax.ShapeDtypeStruct((batch * seq, BRANCH_W), BF16),
        scratch_shapes=[
            pltpu.VMEM((LRU_TS + 8, BRANCH_W), F32),
            pltpu.VMEM((LRU_TS + LRU_PAD, BRANCH_W), F32),
            pltpu.VMEM((LRU_TS + LRU_PAD, BRANCH_W), F32),
            pltpu.VMEM((1, BRANCH_W), F32),
        ],
        compiler_params=_cparams(("parallel", "arbitrary")),
        name="lru",
    )(z, z, cw, cb, wa, ba, wx, bx, lam)


def _merge_kernel(h_ref, ya_ref, yb_ref, yc_ref, yd_ref, zg_ref, bg_ref, wb_ref, wo_ref, gn_ref, o_ref, v_ref):
    ys = (ya_ref, yb_ref, yc_ref, yd_ref)
    mixed = None
    for n in range(N_BRANCH):
        proj = jnp.dot(ys[n][...], wb_ref[n], preferred_element_type=F32)
        gate = zg_ref[:, n * D_MODEL:(n + 1) * D_MODEL].astype(F32) + bg_ref[n:n + 1, :]
        term = jax.nn.sigmoid(gate) * proj
        mixed = term if mixed is None else mixed + term
    o = h_ref[...] + jnp.dot(mixed.astype(BF16), wo_ref[...], preferred_element_type=F32)
    o_ref[...] = o
    v_ref[...] = _rms(o, gn_ref[...]).astype(BF16)


def _merge(h, ya, yb, yc, yd, zg, bg, wb, wo, gn):
    t = h.shape[0]
    row = lambda shape: pl.BlockSpec(shape, lambda i: (i, 0))
    return pl.pallas_call(
        _merge_kernel,
        grid=(t // MERGE_TM,),
        in_specs=[
            row((MERGE_TM, D_MODEL)),
            row((MERGE_TM, BRANCH_W)), row((MERGE_TM, BRANCH_W)), row((MERGE_TM, BRANCH_W)),
            row((MERGE_TM, BRANCH_W)),
            row((MERGE_TM, ZG_W)),
            pl.BlockSpec((N_BRANCH, D_MODEL), lambda i: (0, 0)),
            pl.BlockSpec((N_BRANCH, BRANCH_W, D_MODEL), lambda i: (0, 0, 0), pipeline_mode=pl.Buffered(1)),
            pl.BlockSpec((D_MODEL, D_MODEL), lambda i: (0, 0), pipeline_mode=pl.Buffered(1)),
            pl.BlockSpec((1, D_MODEL), lambda i: (0, 0)),
        ],
        out_specs=[row((MERGE_TM, D_MODEL)), row((MERGE_TM, D_MODEL))],
        out_shape=[jax.ShapeDtypeStruct((t, D_MODEL), F32), jax.ShapeDtypeStruct((t, D_MODEL), BF16)],
        compiler_params=_cparams(("parallel",)),
        name="merge",
    )(h, ya, yb, yc, yd, zg, bg, wb, wo, gn)


def _ffn_kernel(v_ref, h_ref, wg_ref, wu_ref, wd_ref, gn_ref, o_ref, *u_ref, final_norm):
    j = pl.program_id(1)

    @pl.when(j == 0)
    def _():
        o_ref[...] = h_ref[...]

    v = v_ref[...]
    a = jnp.dot(v, wg_ref[...], preferred_element_type=F32)
    b = jnp.dot(v, wu_ref[...], preferred_element_type=F32)
    t = (a * jax.nn.sigmoid(a)) * b
    o_ref[...] += jnp.dot(t.astype(BF16), wd_ref[...], preferred_element_type=F32)

    @pl.when(j == pl.num_programs(1) - 1)
    def _():
        y = _rms(o_ref[...], gn_ref[...])
        if final_norm:
            o_ref[...] = y
        else:
            u_ref[0][...] = y.astype(BF16)


def _ffn(v, h, wg, wu, wd, gn, final_norm):
    t = h.shape[0]
    row = lambda: pl.BlockSpec((FFN_TM, D_MODEL), lambda i, j: (i, 0))
    out_specs = [row()] if final_norm else [row(), row()]
    out_shape = [jax.ShapeDtypeStruct((t, D_MODEL), F32)]
    if not final_norm:
        out_shape.append(jax.ShapeDtypeStruct((t, D_MODEL), BF16))
    return pl.pallas_call(
        functools.partial(_ffn_kernel, final_norm=final_norm),
        grid=(t // FFN_TM, D_FF // FFN_TF),
        in_specs=[
            row(), row(),
            pl.BlockSpec((D_MODEL, FFN_TF), lambda i, j: (0, j)),
            pl.BlockSpec((D_MODEL, FFN_TF), lambda i, j: (0, j)),
            pl.BlockSpec((FFN_TF, D_MODEL), lambda i, j: (j, 0)),
            pl.BlockSpec((1, D_MODEL), lambda i, j: (0, 0)),
        ],
        out_specs=out_specs,
        out_shape=out_shape,
        compiler_params=_cparams(("parallel", "arbitrary")),
        name="ffn_final" if final_norm else "ffn",
    )(v, h, wg, wu, wd, gn)


def _block_diag(w):
    depth, nb, c, d = w.shape
    eye = jnp.eye(nb, dtype=w.dtype)
    return jnp.einsum('lncd,nm->lncmd', w, eye).reshape(depth, nb * c, nb * d)


def kernel(x, norm_mix, w_in, b_gate, diff_lq1, diff_lk1, diff_lq2, diff_lk2, diff_subln, lru_conv_w, lru_conv_b,
           lru_wa, lru_ba, lru_wx, lru_bx, lru_lambda, fox_b_f, swa_sinks, w_branch, w_out, norm_ffn, w_ffn_gate,
           w_ffn_up, w_ffn_down, norm_final):
    batch, seq, d = x.shape
    depth = w_in.shape[0]
    assert d == D_MODEL and seq % max(A_TQ, C_TQ, LRU_TS, SWA_QB * SWA_BLOCK) == 0 and (batch * seq) % IN_TM == 0
    t = batch * seq

    piece = lambda lo, hi: jnp.swapaxes(w_in[:, :, lo:hi], 1, 2).astype(BF16)
    wt_zm = piece(0, ZM_W)
    wt_gz = piece(_R_GZ0, _R_GZ0 + ZG_W)
    wt_d = piece(_R_DQ0, _R_GZ0)
    wt_cf = jnp.pad(piece(_R_CF0, _R_DQ0), ((0, 0), (0, CF_PAD - C_HEADS), (0, 0)))
    wa = _block_diag(lru_wa).astype(BF16)
    wx = _block_diag(lru_wx).astype(BF16)
    lvec = jnp.stack([diff_lq1, diff_lk1, diff_lq2, diff_lk2], axis=1).astype(F32)
    bg = b_gate.reshape(depth, N_BRANCH, D_MODEL)
    gf = norm_final.reshape(1, D_MODEL)

    h = x.reshape(t, D_MODEL)
    u = _norm(h, norm_mix[0].reshape(1, D_MODEL))
    for l in range(depth):
        lam_init = 0.8 - 0.6 * math.exp(-0.3 * l)
        zg, wg, wu, wd = _proj(u, wt_gz, l, "proj_gate", cast=(w_ffn_gate, w_ffn_up, w_ffn_down))
        zm, wb, wo = _proj(u, wt_zm, l, "proj_mix", cast=(w_branch.reshape(depth, N_BRANCH * BRANCH_W, D_MODEL), w_out))
        zd, cf = _proj_d(u, wt_d, wt_cf, l)

        ya = _diff_attn(zm, lvec[l], diff_subln[l].reshape(1, 2 * A_HD),
                        jnp.asarray([lam_init, 1.0 - lam_init], F32), batch, seq)
        yb = _lru(zm, lru_conv_w[l], lru_conv_b[l].reshape(1, -1), wa, lru_ba[l].reshape(1, -1), wx,
                  lru_bx[l].reshape(1, -1), lru_lambda[l].reshape(1, -1), l, batch, seq)
        cf_t = cf[:, :C_HEADS].reshape(batch, seq, C_HEADS).transpose(0, 2, 1).reshape(batch * C_HEADS, seq)
        bf_t = jnp.tile(fox_b_f[l], batch).reshape(batch * C_HEADS, 1)
        cum = _fox_cum(cf_t, bf_t).reshape(batch * C_HEADS, 1, seq)
        yc = _fox_attn(zm, cum, batch, seq)
        yd = _swa(zd, swa_sinks[l], batch, seq)

        h, v = _merge(h, ya, yb, yc, yd, zg, bg[l], wb.reshape(N_BRANCH, BRANCH_W, D_MODEL), wo,
                      norm_ffn[l].reshape(1, D_MODEL))
        if l == depth - 1:
            (h,) = _ffn(v, h, wg, wu, wd, gf, final_norm=True)
        else:
            h, u = _ffn(v, h, wg, wu, wd, norm_mix[l + 1].reshape(1, D_MODEL), final_norm=False)
    return h.reshape(batch, seq, D_MODEL)
```

```python
import functools
import math

import jax
import jax.numpy as jnp
import numpy as np
from jax import lax
from jax.experimental import pallas as pl
from jax.experimental.pallas import tpu as pltpu

F32 = jnp.float32
BF16 = jnp.bfloat16
LANES = 128
BF16_SUBLANES = 16

D_MODEL = 2048
N_BRANCH = 4
BRANCH_W = D_MODEL // 4
A_HEADS = 4
A_HD = BRANCH_W // (2 * A_HEADS)
B_BLOCKS = 8
B_CONV = 4
B_C = 8.0
C_HEADS = 4
C_HD = BRANCH_W // C_HEADS
D_HEADS = 8
D_KV = 2
D_GROUP = D_HEADS // D_KV
D_HD = BRANCH_W // D_HEADS
WINDOW = 128
SWA_BLOCK = 128
SWA_QB = 4
D_FF = -(-8 * D_MODEL // (3 * 256)) * 256
RMS_EPS = 1e-6
NEG_INF = -1e30

AQ0 = 0
AK0 = AQ0 + BRANCH_W
AV0 = AK0 + BRANCH_W
BX0 = AV0 + BRANCH_W
BG0 = BX0 + BRANCH_W
CQ0 = BG0 + BRANCH_W
CK0 = CQ0 + BRANCH_W
CV0 = CK0 + BRANCH_W
ZM_W = CV0 + BRANCH_W
DQ0 = 0
DK0 = DQ0 + BRANCH_W
DV0 = DK0 + D_KV * D_HD
ZD_W = DV0 + D_KV * D_HD
ZG_W = N_BRANCH * D_MODEL
CF_PAD = 128

_R_CF0 = ZM_W
_R_DQ0 = _R_CF0 + C_HEADS
_R_GZ0 = _R_DQ0 + ZD_W

V7X_VMEM_LIMIT = 56 * 1024 * 1024

NORM_TM = 1024
IN_TM, IN_TN = 2048, 1024
FLASH_ROWS = 256
A_TQ, A_HPS = 1024, 4
C_TQ, C_HPS = 1024, 4
LOG2E = math.log2(math.e)
LRU_TS = 512
LRU_PAD = LRU_TS // 2
MERGE_TM = 256
FFN_TM, FFN_TF = 512, 512


def _cparams(sem):
    return pltpu.CompilerParams(dimension_semantics=sem, vmem_limit_bytes=V7X_VMEM_LIMIT)


def _rms(x, g):
    return x * lax.rsqrt(jnp.mean(x * x, axis=-1, keepdims=True) + RMS_EPS) * g


def _softplus(y):
    return jnp.maximum(y, 0.0) + jnp.log1p(jnp.exp(-jnp.abs(y)))


def _norm_kernel(x_ref, g_ref, o_ref):
    o_ref[...] = _rms(x_ref[...], g_ref[...]).astype(BF16)


def _norm(x, g):
    t = x.shape[0]
    return pl.pallas_call(
        _norm_kernel,
        grid=(t // NORM_TM,),
        in_specs=[pl.BlockSpec((NORM_TM, D_MODEL), lambda i: (i, 0)), pl.BlockSpec((1, D_MODEL), lambda i: (0, 0))],
        out_specs=pl.BlockSpec((NORM_TM, D_MODEL), lambda i: (i, 0)),
        out_shape=jax.ShapeDtypeStruct((t, D_MODEL), BF16),
        compiler_params=_cparams(("parallel",)),
        name="norm0",
    )(x, g)


_NT = (((1,), (1,)), ((), ()))


def _proj_kernel(u_ref, wt_ref, *rest):
    n = len(rest) // 2
    z_ref = rest[n]
    z_ref[...] = lax.dot_general(u_ref[...], wt_ref[...], _NT, preferred_element_type=F32).astype(BF16)
    for w_ref, o_ref in zip(rest[:n], rest[n + 1:]):
        o_ref[...] = w_ref[...].astype(BF16)


def _cast_rows(n_rows, steps):
    for n_blocks in range(steps, 0, -1):
        if n_rows % n_blocks == 0 and (n_rows // n_blocks) % BF16_SUBLANES == 0:
            return n_rows // n_blocks, n_blocks
    raise ValueError("no bf16-tile-aligned row split")


def _proj(u, wt, l, name, cast=(), n_cols=None):
    t = u.shape[0]
    n_cols = wt.shape[1] if n_cols is None else n_cols
    n_i, n_j = t // IN_TM, n_cols // IN_TN
    in_specs = [
        pl.BlockSpec((IN_TM, D_MODEL), lambda i, j: (i, 0)),
        pl.BlockSpec((None, IN_TN, D_MODEL), lambda i, j: (l, j, 0)),
    ]
    out_specs = [pl.BlockSpec((IN_TM, IN_TN), lambda i, j: (i, j))]
    out_shape = [jax.ShapeDtypeStruct((t, n_cols), BF16)]
    for w in cast:
        _, n_rows, n_c = w.shape
        rows, n_blocks = _cast_rows(n_rows, n_i * n_j)
        blk = lambda i, j, n_blocks=n_blocks: jnp.minimum(i * n_j + j, n_blocks - 1)
        in_specs.append(pl.BlockSpec((None, rows, n_c), lambda i, j, blk=blk: (l, blk(i, j), 0)))
        out_specs.append(pl.BlockSpec((rows, n_c), lambda i, j, blk=blk: (blk(i, j), 0)))
        out_shape.append(jax.ShapeDtypeStruct((n_rows, n_c), BF16))
    out = pl.pallas_call(
        _proj_kernel,
        grid=(n_i, n_j),
        in_specs=in_specs,
        out_specs=out_specs,
        out_shape=out_shape,
        compiler_params=_cparams(("arbitrary", "arbitrary")),
        name=name,
    )(u, wt, *cast)
    return out if cast else out[0]


def _proj_d_kernel(u_ref, wt_ref, wcft_ref, z_ref, cf_ref):
    u = u_ref[...]
    z_ref[...] = lax.dot_general(u, wt_ref[...], _NT, preferred_element_type=F32).astype(BF16)
    cf_ref[...] = lax.dot_general(u, wcft_ref[...], _NT, preferred_element_type=F32)


def _proj_d(u, wt, wcft, l):
    t = u.shape[0]
    return pl.pallas_call(
        _proj_d_kernel,
        grid=(t // IN_TM,),
        in_specs=[
            pl.BlockSpec((IN_TM, D_MODEL), lambda i: (i, 0)),
            pl.BlockSpec((None, ZD_W, D_MODEL), lambda i: (l, 0, 0)),
            pl.BlockSpec((None, CF_PAD, D_MODEL), lambda i: (l, 0, 0)),
        ],
        out_specs=[
            pl.BlockSpec((IN_TM, ZD_W), lambda i: (i, 0)),
            pl.BlockSpec((IN_TM, CF_PAD), lambda i: (i, 0)),
        ],
        out_shape=[jax.ShapeDtypeStruct((t, ZD_W), BF16), jax.ShapeDtypeStruct((t, CF_PAD), F32)],
        compiler_params=_cparams(("parallel",)),
        name="proj_d",
    )(u, wt, wcft)


def _tri_tables(n):
    qi = [i for i in range(n) for _ in range(i + 1)]
    kj = [j for i in range(n) for j in range(i + 1)]
    return jnp.asarray(qi, jnp.int32), jnp.asarray(kj, jnp.int32)


def _flash_rows(q_ref, k_ref, v2_sc, bias, m_sc, acc_sc, r0, diag_q0, scale):
    rows = FLASH_ROWS
    nk = k_ref.shape[0] if diag_q0 is None else diag_q0 + rows
    s = lax.dot_general(q_ref[r0:r0 + rows, :], k_ref[0:nk, :], (((1,), (1,)), ((), ())),
                        preferred_element_type=F32)
    s = s * scale + bias[:, 0:nk]
    if diag_q0 is not None:
        qpos = lax.broadcasted_iota(jnp.int32, s.shape, 0) + diag_q0
        kpos = lax.broadcasted_iota(jnp.int32, s.shape, 1)
        s = jnp.where(kpos <= qpos, s, NEG_INF)
    m_prev = m_sc[r0:r0 + rows, :]
    m_new = jnp.maximum(m_prev, jnp.max(s, axis=-1, keepdims=True))
    alpha = jnp.exp2(m_prev - m_new)
    p = jnp.exp2(s - jnp.concatenate([m_new] * (nk // LANES), axis=1))
    pv = jnp.dot(p.astype(BF16), v2_sc[0:nk, :], preferred_element_type=F32)
    acc_sc[r0:r0 + rows, :] = jnp.concatenate([alpha, alpha], axis=1) * acc_sc[r0:r0 + rows, :] + pv
    m_sc[r0:r0 + rows, :] = m_new


def _flash_block(q_ref, k_ref, v2_sc, bias, m_sc, acc_sc, n_rows, tq, diag, scale):
    for r0 in range(0, n_rows, FLASH_ROWS):
        _flash_rows(q_ref, k_ref, v2_sc, bias, m_sc, acc_sc, r0, (r0 % tq) if diag else None, scale)


def _flash_init(v2_sc, m_sc, acc_sc):
    v2_sc[:, LANES:2 * LANES] = jnp.ones((v2_sc.shape[0], LANES), BF16)
    m_sc[...] = jnp.full(m_sc.shape, -jnp.inf, F32)
    acc_sc[...] = jnp.zeros(acc_sc.shape, F32)


def _diff_attn_kernel(qi_ref, kj_ref, slope_ref, lami_ref, q_ref, k_ref, v_ref, lv_ref, sg_ref, o_ref,
                      q2_sc, v2_sc, m_sc, acc_sc):
    tq = A_TQ
    t = pl.program_id(2)
    i = qi_ref[t]
    j = kj_ref[t]
    heads = [(hh, slice(hh * LANES, (hh + 1) * LANES)) for hh in range(A_HPS)]

    @pl.when(j == 0)
    def _():
        for hh, cols in heads:
            q = q_ref[:, cols]
            lane = lax.broadcasted_iota(jnp.int32, q.shape, 1)
            qs = q * jnp.asarray(A_HD ** -0.5, BF16)
            zero = jnp.zeros_like(qs)
            q2_sc[hh, 0:tq, :] = jnp.where(lane < A_HD, qs, zero)
            q2_sc[hh, tq:2 * tq, :] = jnp.where(lane >= A_HD, qs, zero)
            _flash_init(v2_sc.at[hh], m_sc.at[hh], acc_sc.at[hh])

    kpos = (lax.broadcasted_iota(jnp.int32, (1, tq), 1) + (j - i) * tq).astype(F32)

    def block(diag):
        for hh, cols in heads:
            v2_sc[hh, :, 0:LANES] = v_ref[:, cols]
            bias = kpos * (slope_ref[pl.program_id(1) * A_HPS + hh] * LOG2E)
            _flash_block(q2_sc.at[hh], k_ref.at[:, cols], v2_sc.at[hh], bias, m_sc.at[hh], acc_sc.at[hh],
                         2 * tq, tq, diag, LOG2E)

    @pl.when(j < i)
    def _():
        block(False)

    @pl.when(j == i)
    def _():
        block(True)
        lv = lv_ref[...]
        lam = (jnp.exp(jnp.sum(lv[0:1] * lv[1:2], axis=-1, keepdims=True))
               - jnp.exp(jnp.sum(lv[2:3] * lv[3:4], axis=-1, keepdims=True)) + lami_ref[0])
        for hh, cols in heads:
            acc = acc_sc[hh]
            o = acc[0:tq, 0:LANES] / acc[0:tq, LANES:2 * LANES] - lam * (acc[tq:2 * tq, 0:LANES] / acc[tq:2 * tq, LANES:2 * LANES])
            o_ref[:, cols] = (_rms(o, sg_ref[...]) * lami_ref[1]).astype(BF16)


def _diff_attn(z, lvec, subln, lam_init, batch, seq):
    nq = seq // A_TQ
    qi, kj = _tri_tables(nq)
    slopes = jnp.asarray(np.exp2(-8.0 * np.arange(1, A_HEADS + 1, dtype=np.float32) / A_HEADS), F32)
    w = LANES * A_HPS
    qc, kc, vc = AQ0 // w, AK0 // w, AV0 // w
    smem = pl.BlockSpec(memory_space=pltpu.SMEM)
    grid_spec = pltpu.PrefetchScalarGridSpec(
        num_scalar_prefetch=2,
        grid=(batch, A_HEADS // A_HPS, int(qi.shape[0])),
        in_specs=[
            smem, smem,
            pl.BlockSpec((A_TQ, w), lambda b, h, t, qi, kj: (b * nq + qi[t], qc + h)),
            pl.BlockSpec((A_TQ, w), lambda b, h, t, qi, kj: (b * nq + kj[t], kc + h)),
            pl.BlockSpec((A_TQ, w), lambda b, h, t, qi, kj: (b * nq + kj[t], vc + h)),
            pl.BlockSpec((4, A_HD), lambda b, h, t, qi, kj: (0, 0)),
            pl.BlockSpec((1, 2 * A_HD), lambda b, h, t, qi, kj: (0, 0)),
        ],
        out_specs=pl.BlockSpec((A_TQ, w), lambda b, h, t, qi, kj: (b * nq + qi[t], h)),
        scratch_shapes=[
            pltpu.VMEM((A_HPS, 2 * A_TQ, LANES), BF16),
            pltpu.VMEM((A_HPS, A_TQ, 2 * LANES), BF16),
            pltpu.VMEM((A_HPS, 2 * A_TQ, LANES), F32),
            pltpu.VMEM((A_HPS, 2 * A_TQ, 2 * LANES), F32),
        ],
    )
    return pl.pallas_call(
        _diff_attn_kernel,
        grid_spec=grid_spec,
        out_shape=jax.ShapeDtypeStruct((batch * seq, BRANCH_W), BF16),
        compiler_params=_cparams(("parallel", "parallel", "arbitrary")),
        name="diff_attn",
    )(qi, kj, slopes, lam_init, z, z, z, lvec, subln)


def _fox_cum_kernel(cf_ref, bf_ref, o_ref):
    x = cf_ref[...] + bf_ref[...]
    logf = jnp.minimum(x, 0.0) - jnp.log1p(jnp.exp(-jnp.abs(x)))
    rows, seq = logf.shape
    lane = lax.broadcasted_iota(jnp.int32, (rows, LANES), 1)
    carry = jnp.zeros((rows, 1), F32)
    for c in range(seq // LANES):
        y = logf[:, c * LANES:(c + 1) * LANES]
        sh = 1
        while sh < LANES:
            y = y + jnp.where(lane >= sh, pltpu.roll(y, sh, axis=1), 0.0)
            sh *= 2
        y = y + carry
        o_ref[:, c * LANES:(c + 1) * LANES] = y
        carry = y[:, LANES - 1:LANES]


def _fox_cum(cf_t, bf_t):
    rows, seq = cf_t.shape
    return pl.pallas_call(
        _fox_cum_kernel,
        grid=(1,),
        in_specs=[pl.BlockSpec((rows, seq), lambda i: (0, 0)), pl.BlockSpec((rows, 1), lambda i: (0, 0))],
        out_specs=pl.BlockSpec((rows, seq), lambda i: (0, 0)),
        out_shape=jax.ShapeDtypeStruct((rows, seq), F32),
        compiler_params=_cparams(("arbitrary",)),
        name="fox_cum",
    )(cf_t, bf_t)


def _fox_attn_kernel(qi_ref, kj_ref, q_ref, k_ref, v_ref, cum_ref, o_ref, v2_sc, m_sc, acc_sc):
    tq = C_TQ
    t = pl.program_id(2)
    i = qi_ref[t]
    j = kj_ref[t]
    heads = [(hh, slice(hh * LANES, (hh + 1) * LANES)) for hh in range(C_HPS)]

    @pl.when(j == 0)
    def _():
        for hh, _ in heads:
            _flash_init(v2_sc.at[hh], m_sc.at[hh], acc_sc.at[hh])

    def block(diag):
        for hh, cols in heads:
            v2_sc[hh, :, 0:LANES] = v_ref[:, cols]
            _flash_block(q_ref.at[:, cols], k_ref.at[:, cols], v2_sc.at[hh], cum_ref[hh] * (-LOG2E), m_sc.at[hh],
                         acc_sc.at[hh], tq, tq, diag, C_HD ** -0.5 * LOG2E)

    @pl.when(j < i)
    def _():
        block(False)

    @pl.when(j == i)
    def _():
        block(True)
        for hh, cols in heads:
            acc = acc_sc[hh]
            o_ref[:, cols] = (acc[:, 0:LANES] / acc[:, LANES:2 * LANES]).astype(BF16)


def _fox_attn(z, cum, batch, seq):
    nq = seq // C_TQ
    qi, kj = _tri_tables(nq)
    w = LANES * C_HPS
    qc, kc, vc = CQ0 // w, CK0 // w, CV0 // w
    hg = C_HEADS // C_HPS
    grid_spec = pltpu.PrefetchScalarGridSpec(
        num_scalar_prefetch=2,
        grid=(batch, hg, int(qi.shape[0])),
        in_specs=[
            pl.BlockSpec((C_TQ, w), lambda b, h, t, qi, kj: (b * nq + qi[t], qc + h)),
            pl.BlockSpec((C_TQ, w), lambda b, h, t, qi, kj: (b * nq + kj[t], kc + h)),
            pl.BlockSpec((C_TQ, w), lambda b, h, t, qi, kj: (b * nq + kj[t], vc + h)),
            pl.BlockSpec((C_HPS, 1, C_TQ), lambda b, h, t, qi, kj: (b * hg + h, 0, kj[t])),
        ],
        out_specs=pl.BlockSpec((C_TQ, w), lambda b, h, t, qi, kj: (b * nq + qi[t], h)),
        scratch_shapes=[
            pltpu.VMEM((C_HPS, C_TQ, 2 * LANES), BF16),
            pltpu.VMEM((C_HPS, C_TQ, LANES), F32),
            pltpu.VMEM((C_HPS, C_TQ, 2 * LANES), F32),
        ],
    )
    return pl.pallas_call(
        _fox_attn_kernel,
        grid_spec=grid_spec,
        out_shape=jax.ShapeDtypeStruct((batch * seq, BRANCH_W), BF16),
        compiler_params=_cparams(("parallel", "parallel", "arbitrary")),
        name="fox_attn",
    )(qi, kj, z, z, z, cum)


def _swa_kernel(sink_ref, q_ref, kc_ref, kp_ref, vc_ref, vp_ref, bias_ref, o_ref):
    blk = SWA_BLOCK
    first = pl.program_id(1) == 0
    swap = lambda x: jnp.concatenate([x[:, D_HD:2 * D_HD], x[:, 0:D_HD]], axis=1)
    ones = jnp.ones((blk * (SWA_QB + 1), LANES), BF16)
    kk_all = jnp.concatenate([kp_ref[...], kc_ref[...]], axis=0)
    vv_all = jnp.concatenate([vp_ref[...], vc_ref[...]], axis=0)
    kk_var = (kk_all, swap(kk_all))
    vv_var = (jnp.concatenate([vv_all, ones], axis=1), jnp.concatenate([swap(vv_all), ones], axis=1))
    q_all = q_ref[...] * jnp.asarray(D_HD ** -0.5, BF16)
    q_lane = lax.broadcasted_iota(jnp.int32, (2 * blk, LANES), 1)
    o_lane = lax.broadcasted_iota(jnp.int32, (blk, LANES), 1)
    for qb in range(SWA_QB):
        keys = slice(qb * blk, (qb + 2) * blk)
        for kv in range(D_KV):
            res = []
            for half in range(2):
                q2 = jnp.concatenate([q_all[qb * blk:(qb + 1) * blk, (2 * kv + tt) * LANES:(2 * kv + tt + 1) * LANES]
                                      for tt in range(2)], axis=0)
                q2 = jnp.where((q_lane < D_HD) if half == 0 else (q_lane >= D_HD), q2, jnp.zeros_like(q2))
                var = half ^ kv
                s = lax.dot_general(q2, kk_var[var][keys], (((1,), (1,)), ((), ())), preferred_element_type=F32)
                bias = bias_ref[1, kv, half]
                if qb == 0:
                    bias = jnp.where(first, bias_ref[0, kv, half], bias)
                lg = s + bias
                es, sinks = [], []
                for tt in range(2):
                    sink = sink_ref[4 * kv + half + 2 * tt]
                    lg_h = lg[tt * blk:(tt + 1) * blk]
                    m = jnp.maximum(jnp.max(lg_h, axis=-1, keepdims=True), sink)
                    es.append(jnp.exp(lg_h - m))
                    sinks.append(jnp.exp(sink - m))
                pv = jnp.dot(jnp.concatenate(es, axis=0).astype(BF16), vv_var[var][keys],
                             preferred_element_type=F32)
                res.append([pv[tt * blk:(tt + 1) * blk, 0:LANES] / (pv[tt * blk:(tt + 1) * blk, LANES:2 * LANES] + sinks[tt])
                            for tt in range(2)])
            for tt in range(2):
                tile = jnp.where(o_lane < D_HD, res[0][tt], res[1][tt])
                o_ref[qb * blk:(qb + 1) * blk, (2 * kv + tt) * LANES:(2 * kv + tt + 1) * LANES] = tile.astype(BF16)


def _swa_bias():
    slopes = np.exp2(-8.0 * np.arange(1, D_HEADS + 1, dtype=np.float32) / D_HEADS).astype(np.float32)
    qi = np.arange(SWA_BLOCK)
    kj = np.arange(2 * SWA_BLOCK) - SWA_BLOCK
    dist = (qi[:, None] - kj[None, :]).astype(np.float32)
    valid = (dist >= 0) & (dist < WINDOW)
    general = np.where(valid[None], -(slopes[:, None, None] * dist[None]), np.float32(NEG_INF))
    first = np.where((kj >= 0)[None, None, :], general, np.float32(NEG_INF))
    per_head = np.stack([first, general]).astype(np.float32)
    table = np.stack([np.stack([np.concatenate([per_head[:, 4 * kv + half], per_head[:, 4 * kv + half + 2]], axis=1)
                                for half in range(2)], axis=1) for kv in range(D_KV)], axis=1)
    return jnp.asarray(table)


def _swa(z, sinks, batch, seq):
    rows = SWA_QB * SWA_BLOCK
    ns = seq // rows
    qc, kc, vc = DQ0 // BRANCH_W, DK0 // LANES, DV0 // LANES
    cur = lambda b, n: b * ns + n
    prev = lambda b, n: jnp.maximum((b * ns + n) * SWA_QB - 1, 0)
    return pl.pallas_call(
        _swa_kernel,
        grid=(batch, ns),
        in_specs=[
            pl.BlockSpec(memory_space=pltpu.SMEM),
            pl.BlockSpec((rows, BRANCH_W), lambda b, n: (cur(b, n), qc)),
            pl.BlockSpec((rows, LANES), lambda b, n: (cur(b, n), kc)),
            pl.BlockSpec((SWA_BLOCK, LANES), lambda b, n: (prev(b, n), kc)),
            pl.BlockSpec((rows, LANES), lambda b, n: (cur(b, n), vc)),
            pl.BlockSpec((SWA_BLOCK, LANES), lambda b, n: (prev(b, n), vc)),
            pl.BlockSpec((2, D_KV, 2, 2 * SWA_BLOCK, 2 * SWA_BLOCK), lambda b, n: (0, 0, 0, 0, 0)),
        ],
        out_specs=pl.BlockSpec((rows, BRANCH_W), lambda b, n: (cur(b, n), 0)),
        out_shape=jax.ShapeDtypeStruct((batch * seq, BRANCH_W), BF16),
        compiler_params=_cparams(("parallel", "arbitrary")),
        name="swa",
    )(sinks, z, z, z, z, z, _swa_bias())


def _lru_kernel(x_ref, g_ref, cw_ref, cb_ref, wa_ref, ba_ref, wx_ref, bx_ref, lam_ref, o_ref,
                xpad_sc, apad_sc, upad_sc, hc_sc):
    ts, pad = LRU_TS, LRU_PAD

    @pl.when(pl.program_id(1) == 0)
    def _():
        xpad_sc[0:8, :] = jnp.zeros((8, BRANCH_W), F32)
        hc_sc[...] = jnp.zeros(hc_sc.shape, F32)
        apad_sc[0:pad, :] = jnp.ones((pad, BRANCH_W), F32)
        upad_sc[0:pad, :] = jnp.zeros((pad, BRANCH_W), F32)

    x = x_ref[...].astype(F32)
    xpad_sc[8:8 + ts, :] = x
    cw = cw_ref[...]
    xc = (cw[3:4] * x + cw[2:3] * xpad_sc[7:7 + ts, :] + cw[1:2] * xpad_sc[6:6 + ts, :]
          + cw[0:1] * xpad_sc[5:5 + ts, :] + cb_ref[...])
    xpad_sc[0:8, :] = x[ts - 8:ts]

    xcb = xc.astype(BF16)
    r = jax.nn.sigmoid(jnp.dot(xcb, wa_ref[...], preferred_element_type=F32) + ba_ref[...])
    gi = jax.nn.sigmoid(jnp.dot(xcb, wx_ref[...], preferred_element_type=F32) + bx_ref[...])
    log_a = (-B_C * r) * _softplus(-lam_ref[...])
    a = jnp.exp(log_a)
    u = jnp.sqrt(1.0 - a * a) * (gi * xc)

    sh = 1
    while sh < ts:
        apad_sc[pad:pad + ts, :] = a
        upad_sc[pad:pad + ts, :] = u
        u = a * upad_sc[pad - sh:pad - sh + ts, :] + u
        a = a * apad_sc[pad - sh:pad - sh + ts, :]
        sh *= 2
    h = a * hc_sc[...] + u
    hc_sc[...] = h[ts - 1:ts]
    g = g_ref[...].astype(F32)
    gelu = g * jax.nn.sigmoid((2.0 * math.sqrt(2.0 / math.pi)) * (g + 0.044715 * (g * g * g)))
    o_ref[...] = (gelu * h).astype(BF16)


def _lru(z, cw, cb, wa, ba, wx, bx, lam, l, batch, seq):
    ns = seq // LRU_TS
    xc_, gc_ = BX0 // BRANCH_W, BG0 // BRANCH_W
    const = lambda shape: pl.BlockSpec(shape, lambda b, s: (0, 0))
    return pl.pallas_call(
        _lru_kernel,
        grid=(batch, ns),
        in_specs=[
            pl.BlockSpec((LRU_TS, BRANCH_W), lambda b, s: (b * ns + s, xc_)),
            pl.BlockSpec((LRU_TS, BRANCH_W), lambda b, s: (b * ns + s, gc_)),
            const((B_CONV, BRANCH_W)), const((1, BRANCH_W)),
            pl.BlockSpec((None, BRANCH_W, BRANCH_W), lambda b, s: (l, 0, 0)), const((1, BRANCH_W)),
            pl.BlockSpec((None, BRANCH_W, BRANCH_W), lambda b, s: (l, 0, 0)), const((1, BRANCH_W)),
            const((1, BRANCH_W)),
        ],
        out_specs=pl.BlockSpec((LRU_TS, BRANCH_W), lambda b, s: (b * ns + s, 0)),
        out_shape=jax.ShapeDtypeStruct((batch * seq, BRANCH_W), BF16),
        scratch_shapes=[
            pltpu.VMEM((LRU_TS + 8, BRANCH_W), F32),
            pltpu.VMEM((LRU_TS + LRU_PAD, BRANCH_W), F32),
            pltpu.VMEM((LRU_TS + LRU_PAD, BRANCH_W), F32),
            pltpu.VMEM((1, BRANCH_W), F32),
        ],
        compiler_params=_cparams(("parallel", "arbitrary")),
        name="lru",
    )(z, z, cw, cb, wa, ba, wx, bx, lam)


def _merge_kernel(h_ref, ya_ref, yb_ref, yc_ref, yd_ref, zg_ref, bg_ref, wb_ref, wo_ref, gn_ref, o_ref, v_ref):
    ys = (ya_ref, yb_ref, yc_ref, yd_ref)
    mixed = None
    for n in range(N_BRANCH):
        proj = jnp.dot(ys[n][...], wb_ref[n], preferred_element_type=F32)
        gate = zg_ref[:, n * D_MODEL:(n + 1) * D_MODEL].astype(F32) + bg_ref[n:n + 1, :]
        term = jax.nn.sigmoid(gate) * proj
        mixed = term if mixed is None else mixed + term
    o = h_ref[...] + jnp.dot(mixed.astype(BF16), wo_ref[...], preferred_element_type=F32)
    o_ref[...] = o
    v_ref[...] = _rms(o, gn_ref[...]).astype(BF16)


def _merge(h, ya, yb, yc, yd, zg, bg, wb, wo, gn):
    t = h.shape[0]
    row = lambda shape: pl.BlockSpec(shape, lambda i: (i, 0))
    return pl.pallas_call(
        _merge_kernel,
        grid=(t // MERGE_TM,),
        in_specs=[
            row((MERGE_TM, D_MODEL)),
            row((MERGE_TM, BRANCH_W)), row((MERGE_TM, BRANCH_W)), row((MERGE_TM, BRANCH_W)),
            row((MERGE_TM, BRANCH_W)),
            row((MERGE_TM, ZG_W)),
            pl.BlockSpec((N_BRANCH, D_MODEL), lambda i: (0, 0)),
            pl.BlockSpec((N_BRANCH, BRANCH_W, D_MODEL), lambda i: (0, 0, 0), pipeline_mode=pl.Buffered(1)),
            pl.BlockSpec((D_MODEL, D_MODEL), lambda i: (0, 0), pipeline_mode=pl.Buffered(1)),
            pl.BlockSpec((1, D_MODEL), lambda i: (0, 0)),
        ],
        out_specs=[row((MERGE_TM, D_MODEL)), row((MERGE_TM, D_MODEL))],
        out_shape=[jax.ShapeDtypeStruct((t, D_MODEL), F32), jax.ShapeDtypeStruct((t, D_MODEL), BF16)],
        compiler_params=_cparams(("parallel",)),
        name="merge",
    )(h, ya, yb, yc, yd, zg, bg, wb, wo, gn)


def _ffn_kernel(v_ref, h_ref, wg_ref, wu_ref, wd_ref, gn_ref, o_ref, *u_ref, final_norm):
    j = pl.program_id(1)

    @pl.when(j == 0)
    def _():
        o_ref[...] = h_ref[...]

    v = v_ref[...]
    a = jnp.dot(v, wg_ref[...], preferred_element_type=F32)
    b = jnp.dot(v, wu_ref[...], preferred_element_type=F32)
    t = (a * jax.nn.sigmoid(a)) * b
    o_ref[...] += jnp.dot(t.astype(BF16), wd_ref[...], preferred_element_type=F32)

    @pl.when(j == pl.num_programs(1) - 1)
    def _():
        y = _rms(o_ref[...], gn_ref[...])
        if final_norm:
            o_ref[...] = y
        else:
            u_ref[0][...] = y.astype(BF16)


def _ffn(v, h, wg, wu, wd, gn, final_norm):
    t = h.shape[0]
    row = lambda: pl.BlockSpec((FFN_TM, D_MODEL), lambda i, j: (i, 0))
    out_specs = [row()] if final_norm else [row(), row()]
    out_shape = [jax.ShapeDtypeStruct((t, D_MODEL), F32)]
    if not final_norm:
        out_shape.append(jax.ShapeDtypeStruct((t, D_MODEL), BF16))
    return pl.pallas_call(
        functools.partial(_ffn_kernel, final_norm=final_norm),
        grid=(t // FFN_TM, D_FF // FFN_TF),
        in_specs=[
            row(), row(),
            pl.BlockSpec((D_MODEL, FFN_TF), lambda i, j: (0, j)),
            pl.BlockSpec((D_MODEL, FFN_TF), lambda i, j: (0, j)),
            pl.BlockSpec((FFN_TF, D_MODEL), lambda i, j: (j, 0)),
            pl.BlockSpec((1, D_MODEL), lambda i, j: (0, 0)),
        ],
        out_specs=out_specs,
        out_shape=out_shape,
        compiler_params=_cparams(("parallel", "arbitrary")),
        name="ffn_final" if final_norm else "ffn",
    )(v, h, wg, wu, wd, gn)


def _block_diag(w):
    depth, nb, c, d = w.shape
    eye = jnp.eye(nb, dtype=w.dtype)
    return jnp.einsum('lncd,nm->lncmd', w, eye).reshape(depth, nb * c, nb * d)


def kernel(x, norm_mix, w_in, b_gate, diff_lq1, diff_lk1, diff_lq2, diff_lk2, diff_subln, lru_conv_w, lru_conv_b,
           lru_wa, lru_ba, lru_wx, lru_bx, lru_lambda, fox_b_f, swa_sinks, w_branch, w_out, norm_ffn, w_ffn_gate,
           w_ffn_up, w_ffn_down, norm_final):
    batch, seq, d = x.shape
    depth = w_in.shape[0]
    assert d == D_MODEL and seq % max(A_TQ, C_TQ, LRU_TS, SWA_QB * SWA_BLOCK) == 0 and (batch * seq) % IN_TM == 0
    t = batch * seq

    wt_all = jnp.swapaxes(w_in, 1, 2).astype(BF16)
    piece = lambda lo, hi: wt_all[:, lo:hi]
    wt_gz = piece(_R_GZ0, _R_GZ0 + ZG_W)
    wt_d = piece(_R_DQ0, _R_GZ0)
    wt_cf = jnp.pad(piece(_R_CF0, _R_DQ0), ((0, 0), (0, CF_PAD - C_HEADS), (0, 0)))
    wa = _block_diag(lru_wa).astype(BF16)
    wx = _block_diag(lru_wx).astype(BF16)
    lvec = jnp.stack([diff_lq1, diff_lk1, diff_lq2, diff_lk2], axis=1).astype(F32)
    bg = b_gate.reshape(depth, N_BRANCH, D_MODEL)
    gf = norm_final.reshape(1, D_MODEL)

    h = x.reshape(t, D_MODEL)
    u = _norm(h, norm_mix[0].reshape(1, D_MODEL))
    for l in range(depth):
        lam_init = 0.8 - 0.6 * math.exp(-0.3 * l)
        zg, wg, wu, wd = _proj(u, wt_gz, l, "proj_gate", cast=(w_ffn_gate, w_ffn_up, w_ffn_down))
        zm, wb, wo = _proj(u, wt_all, l, "proj_mix", n_cols=ZM_W,
                           cast=(w_branch.reshape(depth, N_BRANCH * BRANCH_W, D_MODEL), w_out))
        zd, cf = _proj_d(u, wt_d, wt_cf, l)

        ya = _diff_attn(zm, lvec[l], diff_subln[l].reshape(1, 2 * A_HD),
                        jnp.asarray([lam_init, 1.0 - lam_init], F32), batch, seq)
        yb = _lru(zm, lru_conv_w[l], lru_conv_b[l].reshape(1, -1), wa, lru_ba[l].reshape(1, -1), wx,
                  lru_bx[l].reshape(1, -1), lru_lambda[l].reshape(1, -1), l, batch, seq)
        cf_t = cf[:, :C_HEADS].reshape(batch, seq, C_HEADS).transpose(0, 2, 1).reshape(batch * C_HEADS, seq)
        bf_t = jnp.tile(fox_b_f[l], batch).reshape(batch * C_HEADS, 1)
        cum = _fox_cum(cf_t, bf_t).reshape(batch * C_HEADS, 1, seq)
        yc = _fox_attn(zm, cum, batch, seq)
        yd = _swa(zd, swa_sinks[l], batch, seq)

        h, v = _merge(h, ya, yb, yc, yd, zg, bg[l], wb.reshape(N_BRANCH, BRANCH_W, D_MODEL), wo,
                      norm_ffn[l].reshape(1, D_MODEL))
        if l == depth - 1:
            (h,) = _ffn(v, h, wg, wu, wd, gf, final_norm=True)
        else:
            h, u = _ffn(v, h, wg, wu, wd, norm_mix[l + 1].reshape(1, D_MODEL), final_norm=False)
    return h.reshape(batch, seq, D_MODEL)
```
